```python
import math
import jax
import jax.numpy as jnp
from jax import lax
import numpy as np

D_MODEL = 1024
BATCH = 16
SEQ = 2048
DEPTH = 4

GRID_W = 64
CTX_LEN = 256
NORM_EPS = 1e-6
HEAD_DIM = D_MODEL // 16

GLA_HEADS = 4
GLA_DV = HEAD_DIM
GLA_DK = HEAD_DIM // 2
GLA_GATE_RANK = 16
GLA_TAU = 16.0
GLA_CHUNK = 64

NAT_HEADS = 4
NAT_DH = HEAD_DIM
NAT_WIN_R = 8
NAT_WIN_C = 16

DIFF_HEADS = 4
DIFF_DH = HEAD_DIM
Q_BLOCK = 128
ROPE_BASE = 10000.0

PEER_HEADS = 8
PEER_NKEYS = 128
PEER_EXPERTS = PEER_NKEYS * PEER_NKEYS
PEER_QDIM = 128
PEER_HALF = PEER_QDIM // 2
PEER_TOPK = 16
PEER_TOKEN_BLOCK = 128

GLA_WIDTH = GLA_HEADS * GLA_DV
NAT_WIDTH = NAT_HEADS * NAT_DH
DIFF_WIDTH = DIFF_HEADS * 2 * DIFF_DH
MIX_WIDTH = GLA_WIDTH + NAT_WIDTH + DIFF_WIDTH
PROJ_SIZES = (GLA_HEADS * GLA_DK, GLA_HEADS * GLA_DK, GLA_WIDTH, GLA_WIDTH, GLA_GATE_RANK, GLA_GATE_RANK,
              NAT_WIDTH, NAT_WIDTH, NAT_WIDTH, DIFF_WIDTH, DIFF_WIDTH, DIFF_WIDTH)
PROJ_WIDTH = sum(PROJ_SIZES)

kernel_name = "hybrid_gla_natten_diffattn_peer_dit"


def rms_norm(x, g):
    xf = x.astype(jnp.float32)
    y = xf * lax.rsqrt(jnp.mean(xf * xf, axis=-1, keepdims=True) + NORM_EPS)
    return (y * g.astype(jnp.float32)).astype(x.dtype)


def modulate(x, g, shift, scale):
    return rms_norm(x, g) * (1 + scale) + shift


def heads(t, n):
    return t.reshape(t.shape[:-1] + (n, t.shape[-1] // n))


def flip(t):
    return jnp.flip(t, axis=1)


def split_proj(p):
    points = np.cumsum(PROJ_SIZES)[:-1].tolist()
    return jnp.split(p, points, axis=-1)


def head_rms(o, g):
    of = o.astype(jnp.float32)
    of = of * lax.rsqrt(jnp.mean(of * of, axis=-1, keepdims=True) + NORM_EPS)
    return of.reshape(of.shape[:-2] + (-1,)) * g.astype(jnp.float32)


def gla_scan(q, k, v, g, s0):
    B, L, H, K = q.shape
    V = v.shape[-1]
    n = L // GLA_CHUNK

    def to_chunks(t):
        return t.reshape(B, n, GLA_CHUNK, H, t.shape[-1]).transpose(1, 0, 3, 2, 4).astype(jnp.float32)

    qc, kc, vc, gc = (to_chunks(t) for t in (q, k, v, g))
    mask = jnp.tril(jnp.ones((GLA_CHUNK, GLA_CHUNK), dtype=bool))[:, :, None]

    def step(s, inp):
        qi, ki, vi, gi = inp
        b = jnp.cumsum(gi, axis=2)
        diff = b[:, :, :, None, :] - b[:, :, None, :, :]
        decay = jnp.exp(jnp.where(mask, diff, -jnp.inf))
        a = jnp.einsum('bhik,bhjk,bhijk->bhij', qi, ki, decay)
        o = jnp.einsum('bhij,bhjv->bhiv', a, vi) + jnp.einsum('bhik,bhkv->bhiv', qi * jnp.exp(b), s)
        b_last = b[:, :, -1, :]
        s_new = jnp.exp(b_last)[..., None] * s + jnp.einsum(
            'bhjk,bhjv->bhkv', ki * jnp.exp(b_last[:, :, None, :] - b), vi)
        return s_new, o

    s_fin, o = lax.scan(step, s0.astype(jnp.float32), (qc, kc, vc, gc))
    o = o.transpose(1, 0, 3, 2, 4).reshape(B, L, H, V)
    return o, s_fin


def gla_final_state(k, v, g):
    b = jnp.cumsum(g.astype(jnp.float32), axis=1)
    w = jnp.exp(b[:, -1:] - b)
    return jnp.einsum('blhk,blhv->bhkv', k.astype(jnp.float32) * w, v.astype(jnp.float32))


def gla_inputs(p, gate_up, gate_b):
    q = heads(p[0], GLA_HEADS) * GLA_DK ** -0.5
    k = heads(p[1], GLA_HEADS)
    v = heads(p[2], GLA_HEADS)
    g_f = heads(jax.nn.log_sigmoid((p[4] @ gate_up[0] + gate_b[0]).astype(jnp.float32)) / GLA_TAU, GLA_HEADS)
    g_b = heads(jax.nn.log_sigmoid((p[5] @ gate_up[1] + gate_b[1]).astype(jnp.float32)) / GLA_TAU, GLA_HEADS)
    return q, k, v, g_f, g_b


def gla_output(o, r, norm_g):
    return (head_rms(o, norm_g) * jax.nn.silu(r.astype(jnp.float32))).astype(r.dtype)


def gla_mix(p, pc, gate_up, gate_b, norm_g, need_ctx):
    q, k, v, g_f, g_b = gla_inputs(p, gate_up, gate_b)
    qc, kc, vc, gc_f, gc_b = gla_inputs(pc, gate_up, gate_b)
    s_f = gla_final_state(kc, vc, gc_f)
    s_b = gla_final_state(flip(kc), flip(vc), flip(gc_b))
    o = gla_scan(q, k, v, g_f, s_f)[0] + flip(gla_scan(flip(q), flip(k), flip(v), flip(g_b), s_b)[0])
    y = gla_output(o, p[3], norm_g)
    if not need_ctx:
        return y, None
    zero = jnp.zeros_like(s_f)
    oc = gla_scan(qc, kc, vc, gc_f, zero)[0] + flip(gla_scan(flip(qc), flip(kc), flip(vc), flip(gc_b), zero)[0])
    return y, gla_output(oc, pc[3], norm_g)


def neighbourhood_attention(q, k, v, kc, vc, rpb):
    B, S, H, d = q.shape
    rows = S // GRID_W
    wr = min(NAT_WIN_R, rows)
    r = jnp.arange(rows)
    row_start = jnp.clip(r - wr // 2, 0, rows - wr)
    row_idx = row_start[:, None] + jnp.arange(wr)[None, :]
    cidx = jnp.arange(GRID_W)
    col_start = jnp.clip(cidx - NAT_WIN_C // 2, 0, GRID_W - NAT_WIN_C)
    col_mask = (cidx[None, :] >= col_start[:, None]) & (cidx[None, :] < col_start[:, None] + NAT_WIN_C)
    dr = row_idx - r[:, None] + NAT_WIN_R - 1
    dc = jnp.clip(cidx[None, :] - cidx[:, None] + NAT_WIN_C - 1, 0, 2 * NAT_WIN_C - 2)
    bias = rpb[:, dr[:, None, :, None], dc[None, :, None, :]].astype(jnp.float32)

    scale = d ** -0.5
    qg = q.reshape(B, rows, GRID_W, H, d)
    kg = k.reshape(B, rows, GRID_W, H, d)[:, row_idx]
    vg = v.reshape(B, rows, GRID_W, H, d)[:, row_idx]
    s_win = jnp.einsum('brchd,brjkhd->bhrcjk', qg, kg).astype(jnp.float32) * scale + bias[None]
    s_win = jnp.where(col_mask[:, None, :], s_win, -jnp.inf)
    s_ctx = jnp.einsum('brchd,bmhd->bhrcm', qg, kc).astype(jnp.float32) * scale
    n_win = wr * GRID_W
    s = jnp.concatenate([s_win.reshape(B, H, rows, GRID_W, n_win), s_ctx], axis=-1)
    p = jax.nn.softmax(s, axis=-1).astype(v.dtype)
    p_win = p[..., :n_win].reshape(B, H, rows, GRID_W, wr, GRID_W)
    p_ctx = p[..., n_win:]
    o = jnp.einsum('bhrcjk,brjkhd->brchd', p_win, vg) + jnp.einsum('bhrcm,bmhd->brchd', p_ctx, vc)
    return o.reshape(B, S, H * d)


def softmax_attention(q, k, v):
    s = jnp.einsum('bqhd,bkhd->bhqk', q, k).astype(jnp.float32) * q.shape[-1] ** -0.5
    p = jax.nn.softmax(s, axis=-1).astype(v.dtype)
    return jnp.einsum('bhqk,bkhd->bqhd', p, v)


def rope_1d(x, pos):
    half = x.shape[-1] // 2
    inv = ROPE_BASE ** (-jnp.arange(half, dtype=jnp.float32) / half)
    ang = pos.astype(jnp.float32)[:, None] * inv[None, :]
    cos = jnp.cos(ang)[None, :, None, :]
    sin = jnp.sin(ang)[None, :, None, :]
    x1 = x[..., :half].astype(jnp.float32)
    x2 = x[..., half:].astype(jnp.float32)
    return jnp.concatenate([x1 * cos - x2 * sin, x2 * cos + x1 * sin], axis=-1).astype(x.dtype)


def rope_axial(x, row, col):
    h = x.shape[-1] // 2
    return jnp.concatenate([rope_1d(x[..., :h], row), rope_1d(x[..., h:], col)], axis=-1)


def diff_inputs(p):
    q = p[9].reshape(p[9].shape[:-1] + (DIFF_HEADS, 2, DIFF_DH))
    k = p[10].reshape(p[10].shape[:-1] + (DIFF_HEADS, 2, DIFF_DH))
    v = heads(p[11], DIFF_HEADS)
    return q[..., 0, :], q[..., 1, :], k[..., 0, :], k[..., 1, :], v


def diff_core(q1, q2, k1, k2, v, lam):
    scale = q1.shape[-1] ** -0.5
    a1 = jax.nn.softmax(jnp.einsum('bqhd,bkhd->bhqk', q1, k1).astype(jnp.float32) * scale, axis=-1)
    a2 = jax.nn.softmax(jnp.einsum('bqhd,bkhd->bhqk', q2, k2).astype(jnp.float32) * scale, axis=-1)
    return jnp.einsum('bhqk,bkhe->bqhe', (a1 - lam * a2).astype(v.dtype), v)


def diff_attention_latent(q1, q2, k1, k2, v, k1c, k2c, vc, lam):
    B, S, H, d = q1.shape
    k1a = jnp.concatenate([k1c, k1], axis=1)
    k2a = jnp.concatenate([k2c, k2], axis=1)
    va = jnp.concatenate([vc, v], axis=1)
    nb = S // Q_BLOCK

    def to_blocks(t):
        return jnp.swapaxes(t.reshape(B, nb, Q_BLOCK, H, d), 0, 1)

    def attend(qs):
        return diff_core(qs[0], qs[1], k1a, k2a, va, lam)

    o = lax.map(attend, (to_blocks(q1), to_blocks(q2)))
    return jnp.swapaxes(o, 0, 1).reshape(B, S, H, 2 * d)


def token_mix(h, hc, row, col, w_in, gate_up, gate_b, gla_g, rpb, lam_vec, diff_g, lam_init, need_ctx):
    p = split_proj(h @ w_in)
    pc = split_proj(hc @ w_in)
    y_gla, yc_gla = gla_mix(p, pc, gate_up, gate_b, gla_g, need_ctx)
    nq, nk, nv = (heads(t, NAT_HEADS) for t in p[6:9])
    nqc, nkc, nvc = (heads(t, NAT_HEADS) for t in pc[6:9])
    y_nat = neighbourhood_attention(nq, nk, nv, nkc, nvc, rpb)
    lv = lam_vec.astype(jnp.float32)
    lam = jnp.exp(jnp.sum(lv[0] * lv[1])) - jnp.exp(jnp.sum(lv[2] * lv[3])) + lam_init
    q1, q2, k1, k2, dv = diff_inputs(p)
    q1c, q2c, k1c, k2c, dvc = diff_inputs(pc)
    q1, q2, k1, k2 = (rope_axial(t, row, col) for t in (q1, q2, k1, k2))
    o_diff = diff_attention_latent(q1, q2, k1, k2, dv, k1c, k2c, dvc, lam)
    y_diff = head_rms(o_diff, diff_g) * (1.0 - lam_init)
    y = jnp.concatenate([y_gla, y_nat.astype(h.dtype), y_diff.astype(h.dtype)], axis=-1)
    if not need_ctx:
        return y, None
    yc_nat = softmax_attention(nqc, nkc, nvc).reshape(hc.shape[:-1] + (NAT_WIDTH,))
    yc_diff = head_rms(diff_core(q1c, q2c, k1c, k2c, dvc, lam), diff_g) * (1.0 - lam_init)
    yc = jnp.concatenate([yc_gla, yc_nat.astype(hc.dtype), yc_diff.astype(hc.dtype)], axis=-1)
    return y, yc


def peer_ffn(h, wq, subkeys, u, v):
    B, L, D = h.shape
    hb = h.reshape(-1, PEER_TOKEN_BLOCK, D)

    def block(hx):
        tb = hx.shape[0]
        q = (hx @ wq).reshape(tb, PEER_HEADS, 2, PEER_HALF)
        s1 = jnp.einsum('thd,nd->thn', q[:, :, 0], subkeys[0]).astype(jnp.float32)
        s2 = jnp.einsum('thd,nd->thn', q[:, :, 1], subkeys[1]).astype(jnp.float32)
        v1, i1 = lax.top_k(s1, PEER_TOPK)
        v2, i2 = lax.top_k(s2, PEER_TOPK)
        cand = (v1[..., :, None] + v2[..., None, :]).reshape(tb, PEER_HEADS, -1)
        cid = (i1[..., :, None] * PEER_NKEYS + i2[..., None, :]).reshape(tb, PEER_HEADS, -1)
        top, pos = lax.top_k(cand, PEER_TOPK)
        eid = jnp.take_along_axis(cid, pos, axis=-1)
        gate = jax.nn.softmax(top, axis=-1)
        act = jax.nn.gelu(jnp.einsum('td,thkd->thk', hx, u[eid]).astype(jnp.float32))
        return jnp.einsum('thk,thkd->td', (gate * act).astype(hx.dtype), v[eid])

    return lax.map(block, hb).reshape(B, L, D)


def setup_inputs(seed: int = 0) -> dict:
    key = jax.random.key(seed)
    ks = jax.random.split(key, 21)
    D = D_MODEL

    def nrm(k, shape, s):
        return jax.random.normal(k, shape, jnp.float32) * s

    return {
        "x": nrm(ks[0], (BATCH, SEQ, D), 1.0),
        "c": nrm(ks[1], (BATCH, D), 1.0),
        "ctx": nrm(ks[2], (BATCH, CTX_LEN, D), 1.0),
        "c_ctx": nrm(ks[3], (D,), 1.0),
        "ada_w": nrm(ks[4], (DEPTH, D, 6 * D), 0.5 * D ** -0.5),
        "ada_b": nrm(ks[5], (DEPTH, 6 * D), 0.01),
        "norm1_g": 1.0 + nrm(ks[6], (DEPTH, D), 0.02),
        "norm2_g": 1.0 + nrm(ks[7], (DEPTH, D), 0.02),
        "w_in": nrm(ks[8], (DEPTH, D, PROJ_WIDTH), D ** -0.5),
        "gla_gate_up": nrm(ks[9], (DEPTH, 2, GLA_GATE_RANK, GLA_HEADS * GLA_DK), GLA_GATE_RANK ** -0.5),
        "gla_gate_b": 1.0 + nrm(ks[10], (DEPTH, 2, GLA_HEADS * GLA_DK), 0.1),
        "gla_norm_g": 1.0 + nrm(ks[11], (DEPTH, GLA_WIDTH), 0.02),
        "nat_rpb": nrm(ks[12], (DEPTH, NAT_HEADS, 2 * NAT_WIN_R - 1, 2 * NAT_WIN_C - 1), 0.1),
        "diff_lambda": nrm(ks[13], (DEPTH, 4, DIFF_DH), 0.1),
        "diff_norm_g": 1.0 + nrm(ks[14], (DEPTH, DIFF_WIDTH), 0.02),
        "w_out": nrm(ks[15], (DEPTH, MIX_WIDTH, D), MIX_WIDTH ** -0.5),
        "peer_wq": nrm(ks[16], (DEPTH, D, PEER_HEADS * PEER_QDIM), D ** -0.5),
        "peer_subkeys": nrm(ks[17], (DEPTH, 2, PEER_NKEYS, PEER_HALF), PEER_HALF ** -0.5),
        "peer_u": nrm(ks[18], (DEPTH, PEER_EXPERTS, D), D ** -0.5),
        "peer_v": nrm(ks[19], (DEPTH, PEER_EXPERTS, D), PEER_HEADS ** -0.5),
        "final_g": 1.0 + nrm(ks[20], (D,), 0.02),
    }


def reference(x, c, ctx, c_ctx, ada_w, ada_b, norm1_g, norm2_g, w_in, gla_gate_up, gla_gate_b, gla_norm_g,
              nat_rpb, diff_lambda, diff_norm_g, w_out, peer_wq, peer_subkeys, peer_u, peer_v, final_g):
    B, S, D = x.shape
    t = jnp.arange(S)
    row = t // GRID_W
    col = t % GRID_W
    xc = ctx
    for l in range(DEPTH):
        last = l == DEPTH - 1
        lam_init = 0.8 - 0.6 * math.exp(-0.3 * l)
        mod = jnp.split((jax.nn.silu(c) @ ada_w[l] + ada_b[l])[:, None, :], 6, axis=-1)
        mod_c = jnp.split(jax.nn.silu(c_ctx) @ ada_w[l] + ada_b[l], 6, axis=-1)
        h = modulate(x, norm1_g[l], mod[0], mod[1])
        hc = modulate(xc, norm1_g[l], mod_c[0], mod_c[1])
        y, yc = token_mix(h, hc, row, col, w_in[l], gla_gate_up[l], gla_gate_b[l], gla_norm_g[l],
                          nat_rpb[l], diff_lambda[l], diff_norm_g[l], lam_init, not last)
        x = x + mod[2] * (y @ w_out[l])
        h2 = modulate(x, norm2_g[l], mod[3], mod[4])
        x = x + mod[5] * peer_ffn(h2, peer_wq[l], peer_subkeys[l], peer_u[l], peer_v[l])
        if not last:
            xc = xc + mod_c[2] * (yc @ w_out[l])
            hc2 = modulate(xc, norm2_g[l], mod_c[3], mod_c[4])
            xc = xc + mod_c[5] * peer_ffn(hc2, peer_wq[l], peer_subkeys[l], peer_u[l], peer_v[l])
    return rms_norm(x, final_g)
```

```python
import functools
import math

import numpy as np
import jax
import jax.numpy as jnp
from jax import lax
from jax.experimental import pallas as pl
from jax.experimental.pallas import tpu as pltpu

F32 = jnp.float32
BF16 = jnp.bfloat16
HIGHEST = lax.Precision.HIGHEST

D = 1024
SEQ = 2048
CTX = 256
T = SEQ + CTX
GRID_W = 64
GRID_ROWS = SEQ // GRID_W
EPS = 1e-6
TM = 256
NT_TILES = T // TM

GLA_HEADS, GLA_DK, GLA_DV, GLA_RANK, GLA_TAU, GLA_CHUNK = 4, 32, 64, 16, 16.0, 64
NAT_HEADS, NAT_DH, NAT_WIN_R, NAT_WIN_C = 4, 64, 8, 16
DIFF_HEADS, DIFF_DH = 4, 64
ROPE_BASE = 10000.0
PEER_HEADS, PEER_NKEYS, PEER_HALF, PEER_TOPK = 8, 128, 64, 16
PEER_EXPERTS = PEER_NKEYS * PEER_NKEYS
PEER_TB = 512
PEER_ROWS = 8
PEER_EB = PEER_ROWS * PEER_NKEYS

C_GQ, C_GK, C_GV, C_GR, C_GD = 0, 128, 256, 512, 768
C_NQ, C_NK, C_NV = 896, 1152, 1408
C_DQ, C_DK, C_DV, C_DQR, C_DKR = 1664, 2176, 2688, 3200, 3712
PROJ_COLS = 4224

_NT_DIMS = (((1,), (1,)), ((), ()))


def _nt(a, b):
    return lax.dot_general(a, b, _NT_DIMS, preferred_element_type=F32)


def _params(*sem):
    return pltpu.CompilerParams(dimension_semantics=sem, vmem_limit_bytes=56 * 1024 * 1024)


def _rms(x):
    return x * lax.rsqrt(jnp.mean(x * x, axis=-1, keepdims=True) + EPS)


def _sigmoid(x):
    return 1.0 / (1.0 + jnp.exp(-x))


def _log_sigmoid(x):
    return jnp.minimum(x, 0.0) - jnp.log1p(jnp.exp(-jnp.abs(x)))


def _mod_kernel(c_ref, w_ref, b_ref, o_ref):
    c = c_ref[...]
    o_ref[0] = jnp.dot(c * _sigmoid(c), w_ref[0], precision=HIGHEST, preferred_element_type=F32) + b_ref[0]


def _modulation(cc, ada_w, ada_b):
    depth = ada_w.shape[0]
    rows = cc.shape[0]
    return pl.pallas_call(
        _mod_kernel,
        grid=(depth, 6),
        in_specs=[pl.BlockSpec((rows, D), lambda l, j: (0, 0)),
                  pl.BlockSpec((1, D, D), lambda l, j: (l, 0, j)),
                  pl.BlockSpec((1, 1, D), lambda l, j: (l, 0, j))],
        out_specs=pl.BlockSpec((1, rows, D), lambda l, j: (l, 0, j)),
        out_shape=jax.ShapeDtypeStruct((depth, rows, 6 * D), F32),
        compiler_params=_params("arbitrary", "arbitrary"),
        name="adaln_mod",
    )(cc, ada_w, ada_b.reshape(depth, 1, 6 * D))


def _proj_kernel(*refs, fuse_res):
    if fuse_res:
        x_ref, pr_ref, modp_ref = refs[:3]
        refs = refs[3:]
    else:
        x_ref = refs[0]
        refs = refs[1:]
    (mod_ref, g_ref, w_ref, gf_ref, gb_ref, gbias_ref, cos_ref, sin_ref) = refs[:8]
    outs = refs[8:]
    if fuse_res:
        xo_ref = outs[0]
        outs = outs[1:]
    (gq_ref, gk_ref, ggf_ref, ggb_ref, gv_ref, gr_ref, nq_ref, nk_ref, nv_ref, dq_ref, dk_ref, dv_ref) = outs

    x = x_ref[0]
    if fuse_res:
        x = x + modp_ref[0][:, 5 * D:6 * D] * pr_ref[0]
        xo_ref[0] = x
    m = mod_ref[0]
    h = _rms(x) * g_ref[...] * (1.0 + m[:, D:2 * D]) + m[:, 0:D]
    p = jnp.dot(h.astype(BF16), w_ref[...], preferred_element_type=F32)

    gq_ref[0] = p[:, C_GQ:C_GQ + 128] * (GLA_DK ** -0.5)
    gk_ref[0] = p[:, C_GK:C_GK + 128]
    gv_ref[0] = p[:, C_GV:C_GV + 256]
    gr_ref[0] = p[:, C_GR:C_GR + 256]
    pd = p[:, C_GD:C_GD + 128].astype(BF16)
    gbias = gbias_ref[...]
    xf = jnp.dot(pd, gf_ref[...], preferred_element_type=F32) + gbias[0:1]
    xb = jnp.dot(pd, gb_ref[...], preferred_element_type=F32) + gbias[1:2]
    ggf_ref[0] = _log_sigmoid(xf) / GLA_TAU
    ggb_ref[0] = _log_sigmoid(xb) / GLA_TAU

    nq_ref[0] = (p[:, C_NQ:C_NQ + 256] * (NAT_DH ** -0.5)).astype(BF16)
    nk_ref[0] = p[:, C_NK:C_NK + 256].astype(BF16)
    nv_ref[0] = p[:, C_NV:C_NV + 256].astype(BF16)

    cos = cos_ref[...]
    sin = sin_ref[...]
    dq = p[:, C_DQ:C_DQ + 512] * cos + p[:, C_DQR:C_DQR + 512] * sin
    dq_ref[0] = (dq * (DIFF_DH ** -0.5)).astype(BF16)
    dk_ref[0] = (p[:, C_DK:C_DK + 512] * cos + p[:, C_DKR:C_DKR + 512] * sin).astype(BF16)
    dv_ref[0] = p[:, C_DV:C_DV + 512].astype(BF16)


def _mod_spec(nb):
    return pl.BlockSpec((1, 1, 6 * D), lambda b, t: (jnp.where(t == 0, nb, b), 0, 0))


def _norm_proj(x, peer, mod_prev, mod, g1, w_all, gf, gb, gbias, cos, sin):
    nb = x.shape[0]
    fuse = peer is not None
    tile = lambda w: pl.BlockSpec((1, TM, w), lambda b, t: (b, t, 0))
    full = lambda a: pl.BlockSpec(a.shape, lambda b, t: (0,) * a.ndim)
    ins, specs = [x], [tile(D)]
    if fuse:
        ins += [peer, mod_prev]
        specs += [tile(D), _mod_spec(nb)]
    ins += [mod, g1, w_all, gf, gb, gbias, cos, sin]
    specs += [_mod_spec(nb), full(g1), full(w_all), full(gf), full(gb), full(gbias),
              pl.BlockSpec((TM, 512), lambda b, t: (t, 0)), pl.BlockSpec((TM, 512), lambda b, t: (t, 0))]
    widths = [(128, F32)] * 4 + [(256, F32)] * 2 + [(256, BF16)] * 3 + [(512, BF16)] * 3
    out_shape = [jax.ShapeDtypeStruct((nb, T, w), dt) for w, dt in widths]
    out_specs = [tile(w) for w, _ in widths]
    if fuse:
        out_shape = [jax.ShapeDtypeStruct((nb, T, D), F32)] + out_shape
        out_specs = [tile(D)] + out_specs
    res = pl.pallas_call(
        functools.partial(_proj_kernel, fuse_res=fuse),
        grid=(nb, NT_TILES),
        in_specs=specs, out_specs=out_specs, out_shape=out_shape,
        compiler_params=_params("arbitrary", "arbitrary"),
        name="norm_proj",
    )(*ins)
    if fuse:
        return res[0], res[1:]
    return x, res


def _gla_kernel(q_ref, k_ref, gf_ref, gb_ref, v_ref, r_ref, ng_ref, o_ref, of_s, ob_s, sf_s, sb_s):
    C = GLA_CHUNK
    nch = T // C
    nctx = CTX // C
    ii = lax.broadcasted_iota(jnp.int32, (C, C), 0)
    jj = lax.broadcasted_iota(jnp.int32, (C, C), 1)
    tri_f = (jj <= ii).astype(F32)
    tri_b = (jj >= ii).astype(F32)
    it = lax.broadcasted_iota(jnp.int32, (C, 4 * C), 0)
    jt = lax.broadcasted_iota(jnp.int32, (C, 4 * C), 1) & (C - 1)
    cm_f = jt <= it
    cm_b = jt >= it
    kmask = ((lax.broadcasted_iota(jnp.int32, (4 * C, 128), 0) >> 6)
             == (lax.broadcasted_iota(jnp.int32, (4 * C, 128), 1) >> 5)).astype(F32)
    vmask = ((lax.broadcasted_iota(jnp.int32, (4 * C, 256), 0) >> 6)
             == (lax.broadcasted_iota(jnp.int32, (4 * C, 256), 1) >> 6)).astype(F32)
    smask = ((lax.broadcasted_iota(jnp.int32, (256, 128), 0) >> 6)
             == (lax.broadcasted_iota(jnp.int32, (256, 128), 1) >> 5)).astype(F32)

    sf_s[...] = jnp.zeros_like(sf_s)
    sb_s[...] = jnp.zeros_like(sb_s)

    def chunk(off, g_ref, s_ref, dst, tri, cm, last, mid):
        rows = pl.ds(off, C)
        q = q_ref[0, rows, :]
        k = k_ref[0, rows, :]
        v = v_ref[0, rows, :]
        g = g_ref[0, rows, :]
        st = s_ref[...]
        b = jnp.dot(tri, g, precision=HIGHEST, preferred_element_type=F32)
        b_last = b[last:last + 1]
        b_mid = b[mid:mid + 1]
        qi = q * jnp.exp(jnp.clip(b - b_mid, -80.0, 80.0))
        ki = k * jnp.exp(jnp.clip(b_mid - b, -80.0, 80.0))
        krows = jnp.concatenate([ki, ki, ki, ki], axis=0) * kmask
        a = jnp.where(cm, _nt(qi.astype(BF16), krows.astype(BF16)), 0.0)
        vblk = jnp.concatenate([v, v, v, v], axis=0) * vmask
        o = jnp.dot(a.astype(BF16), vblk.astype(BF16), preferred_element_type=F32)
        o = o + _nt((q * jnp.exp(b)).astype(BF16), st.astype(BF16))
        dst[rows, :] = o
        kd = k * jnp.exp(b_last - b)
        upd = jnp.dot(v.T.astype(BF16), kd.astype(BF16), preferred_element_type=F32)
        s_ref[...] = st * jnp.exp(b_last) + upd * smask

    def body(t, carry):
        cb = jnp.where(t < nctx, nctx - 1 - t, nch + nctx - 1 - t)
        chunk(pl.multiple_of(t * C, C), gf_ref, sf_s, of_s, tri_f, cm_f, C - 1, C // 2 - 1)
        chunk(pl.multiple_of(cb * C, C), gb_ref, sb_s, ob_s, tri_b, cm_b, 0, C // 2)
        return carry

    lax.fori_loop(0, nch, body, 0)

    hones = ((lax.broadcasted_iota(jnp.int32, (256, 256), 0) >> 6)
             == (lax.broadcasted_iota(jnp.int32, (256, 256), 1) >> 6)).astype(F32) * (1.0 / GLA_DV)
    ng = ng_ref[...]
    for t in range(NT_TILES):
        rows = pl.ds(t * TM, TM)
        o = of_s[rows, :] + ob_s[rows, :]
        ms = jnp.dot(o * o, hones, precision=HIGHEST, preferred_element_type=F32)
        r = r_ref[0, rows, :]
        o_ref[0, rows, :] = (o * lax.rsqrt(ms + EPS) * ng * (r * _sigmoid(r))).astype(BF16)


def _gla(gq, gk, ggf, ggb, gv, gr, ng):
    nb = gq.shape[0]
    seq = lambda w: pl.BlockSpec((1, T, w), lambda b: (b, 0, 0))
    return pl.pallas_call(
        _gla_kernel,
        grid=(nb,),
        in_specs=[seq(128), seq(128), seq(128), seq(128), seq(256), seq(256),
                  pl.BlockSpec((1, 256), lambda b: (0, 0))],
        out_specs=seq(256),
        out_shape=jax.ShapeDtypeStruct((nb, T, 256), BF16),
        scratch_shapes=[pltpu.VMEM((T, 256), F32), pltpu.VMEM((T, 256), F32),
                        pltpu.VMEM((256, 128), F32), pltpu.VMEM((256, 128), F32)],
        compiler_params=_params("arbitrary"),
        name="gla",
    )(gq, gk, ggf, ggb, gv, gr, ng)


def _nat_kernel(q_ref, k_ref, v_ref, bias_ref, o_ref):
    lane = lax.broadcasted_iota(jnp.int32, (1, 256), 1) >> 6
    hm_b = [(lane == h).astype(BF16) for h in range(NAT_HEADS)]
    hm_f = [(lane == h).astype(F32) for h in range(NAT_HEADS)]

    qc = q_ref[0, 0:CTX, :]
    kc = k_ref[0, 0:CTX, :]
    vc = v_ref[0, 0:CTX, :]
    acc = jnp.zeros((CTX, 256), F32)
    for h in range(NAT_HEADS):
        s = _nt(qc * hm_b[h], kc)
        e = jnp.exp(s - jnp.max(s, axis=-1, keepdims=True))
        p = (e / jnp.sum(e, axis=-1, keepdims=True)).astype(BF16)
        acc = acc + jnp.dot(p, vc, preferred_element_type=F32) * hm_f[h]
    o_ref[0, 0:CTX, :] = acc.astype(BF16)

    nwin = NAT_WIN_R * GRID_W

    def row(r, carry):
        rs = jnp.clip(r - NAT_WIN_R // 2, 0, GRID_ROWS - NAT_WIN_R)
        case = rs - r + NAT_WIN_R - 1
        qrows = pl.ds(pl.multiple_of(CTX + r * GRID_W, GRID_W), GRID_W)
        wrows = pl.ds(pl.multiple_of(CTX + rs * GRID_W, GRID_W), nwin)
        q = q_ref[0, qrows, :]
        kw = k_ref[0, wrows, :]
        vw = v_ref[0, wrows, :]
        kcx = k_ref[0, 0:CTX, :]
        vcx = v_ref[0, 0:CTX, :]
        acc = jnp.zeros((GRID_W, 256), F32)
        for h in range(NAT_HEADS):
            qh = q * hm_b[h]
            sw = _nt(qh, kw) + bias_ref[h, case]
            sc = _nt(qh, kcx)
            m = jnp.maximum(jnp.max(sw, axis=-1, keepdims=True), jnp.max(sc, axis=-1, keepdims=True))
            ew = jnp.exp(sw - m)
            ec = jnp.exp(sc - m)
            l = jnp.sum(ew, axis=-1, keepdims=True) + jnp.sum(ec, axis=-1, keepdims=True)
            o = jnp.dot((ew / l).astype(BF16), vw, preferred_element_type=F32)
            o = o + jnp.dot((ec / l).astype(BF16), vcx, preferred_element_type=F32)
            acc = acc + o * hm_f[h]
        o_ref[0, qrows, :] = acc.astype(BF16)
        return carry

    lax.fori_loop(0, GRID_ROWS, row, 0)


def _nat(nq, nk, nv, bias):
    nb = nq.shape[0]
    seq = pl.BlockSpec((1, T, 256), lambda b: (b, 0, 0))
    return pl.pallas_call(
        _nat_kernel,
        grid=(nb,),
        in_specs=[seq, seq, seq, pl.BlockSpec(bias.shape, lambda b: (0, 0, 0, 0))],
        out_specs=seq,
        out_shape=jax.ShapeDtypeStruct((nb, T, 256), BF16),
        compiler_params=_params("arbitrary"),
        name="nat",
    )(nq, nk, nv, bias)


def _nat_bias_table(rpb):
    cidx = np.arange(GRID_W)
    col_start = np.clip(cidx - NAT_WIN_C // 2, 0, GRID_W - NAT_WIN_C)
    col_mask = (cidx[None, :] >= col_start[:, None]) & (cidx[None, :] < col_start[:, None] + NAT_WIN_C)
    dc = np.clip(cidx[None, :] - cidx[:, None] + NAT_WIN_C - 1, 0, 2 * NAT_WIN_C - 2)
    dr = np.arange(NAT_WIN_R)[:, None] + np.arange(NAT_WIN_R)[None, :]
    t = rpb.astype(F32)[:, dr][:, :, :, dc]
    t = jnp.where(col_mask[None, None, None], t, -jnp.inf)
    return t.transpose(0, 1, 3, 2, 4).reshape(NAT_HEADS, NAT_WIN_R, GRID_W, NAT_WIN_R * GRID_W)


def _diff_kernel(lam_ref, q_ref, k_ref, v_ref, g_ref, o_ref, *, lam_init):
    lv = lam_ref[...]
    lam = (jnp.exp(jnp.sum(lv[0:1] * lv[1:2], axis=-1, keepdims=True))
           - jnp.exp(jnp.sum(lv[2:3] * lv[3:4], axis=-1, keepdims=True)) + lam_init)
    q = q_ref[0]
    first = lax.broadcasted_iota(jnp.int32, (1, 2 * DIFF_DH), 1) < DIFF_DH
    q1 = jnp.where(first, q, jnp.zeros_like(q))
    q2 = jnp.where(first, jnp.zeros_like(q), q)

    def attend(k, v):
        s1 = _nt(q1, k)
        s2 = _nt(q2, k)
        e1 = jnp.exp(s1 - jnp.max(s1, axis=-1, keepdims=True))
        e2 = jnp.exp(s2 - jnp.max(s2, axis=-1, keepdims=True))
        w1 = 1.0 / jnp.sum(e1, axis=-1, keepdims=True)
        w2 = lam / jnp.sum(e2, axis=-1, keepdims=True)
        p = (e1 * w1 - e2 * w2).astype(BF16)
        o = jnp.dot(p, v, preferred_element_type=F32)
        o_ref[0] = (_rms(o) * g_ref[...] * (1.0 - lam_init)).astype(BF16)

    @pl.when(pl.program_id(2) == 0)
    def _():
        attend(k_ref[0, 0:CTX, :], v_ref[0, 0:CTX, :])

    @pl.when(pl.program_id(2) > 0)
    def _():
        attend(k_ref[0], v_ref[0])


def _diff(dq, dk, dv, lam_vec, dg, lam_init):
    nb = dq.shape[0]
    w = 2 * DIFF_DH
    return pl.pallas_call(
        functools.partial(_diff_kernel, lam_init=lam_init),
        grid=(nb, DIFF_HEADS, NT_TILES),
        in_specs=[pl.BlockSpec((4, DIFF_DH), lambda b, h, t: (0, 0)),
                  pl.BlockSpec((1, TM, w), lambda b, h, t: (b, t, h)),
                  pl.BlockSpec((1, T, w), lambda b, h, t: (b, 0, h)),
                  pl.BlockSpec((1, T, w), lambda b, h, t: (b, 0, h)),
                  pl.BlockSpec((1, w), lambda b, h, t: (0, h))],
        out_specs=pl.BlockSpec((1, TM, w), lambda b, h, t: (b, t, h)),
        out_shape=jax.ShapeDtypeStruct((nb, T, DIFF_HEADS * w), BF16),
        compiler_params=_params("arbitrary", "arbitrary", "arbitrary"),
        name="diff_attn",
    )(lam_vec, dq, dk, dv, dg)


def _out_kernel(x_ref, yg_ref, yn_ref, yd_ref, mod_ref, w_ref, g_ref, xo_ref, ht_ref):
    attn = jnp.dot(yg_ref[0], w_ref[0:256, :], preferred_element_type=F32)
    attn = attn + jnp.dot(yn_ref[0], w_ref[256:512, :], preferred_element_type=F32)
    attn = attn + jnp.dot(yd_ref[0], w_ref[512:1024, :], preferred_element_type=F32)
    m = mod_ref[0]
    x = x_ref[0] + m[:, 2 * D:3 * D] * attn
    xo_ref[0] = x
    h2 = _rms(x) * g_ref[...] * (1.0 + m[:, 4 * D:5 * D]) + m[:, 3 * D:4 * D]
    ht_ref[...] = h2.T.astype(BF16)


def _out_proj(x, yg, yn, yd, mod, w_out, g2):
    nb = x.shape[0]
    tile = lambda w: pl.BlockSpec((1, TM, w), lambda b, t: (b, t, 0))
    return pl.pallas_call(
        _out_kernel,
        grid=(nb, NT_TILES),
        in_specs=[tile(D), tile(256), tile(256), tile(512), _mod_spec(nb),
                  pl.BlockSpec((D, D), lambda b, t: (0, 0)), pl.BlockSpec((1, D), lambda b, t: (0, 0))],
        out_specs=[tile(D), pl.BlockSpec((D, TM), lambda b, t: (0, b * NT_TILES + t))],
        out_shape=[jax.ShapeDtypeStruct((nb, T, D), F32), jax.ShapeDtypeStruct((D, nb * T), BF16)],
        compiler_params=_params("arbitrary", "arbitrary"),
        name="out_proj",
    )(x, yg, yn, yd, mod, w_out, g2)


def _oddeven_pairs(n):
    pairs = []

    def merge(lo, hi, r):
        step = r * 2
        if step < hi - lo:
            merge(lo, hi, step)
            merge(lo + r, hi, step)
            pairs.extend((i, i + r) for i in range(lo + r, hi - r, step))
        else:
            pairs.append((lo, lo + r))

    def sort(lo, hi):
        if hi - lo >= 1:
            mid = lo + (hi - lo) // 2
            sort(lo, mid)
            sort(mid + 1, hi)
            merge(lo, hi, 1)

    sort(0, n - 1)
    return pairs


def _bitonic_pairs(n):
    pairs = []
    d = n // 2
    while d >= 1:
        pairs.extend((i, i + d) for i in range(n) if (i // d) % 2 == 0)
        d //= 2
    return pairs


_SORT16 = _oddeven_pairs(16)
_BITONIC16 = _bitonic_pairs(16)


def _compare_exchange(xs, pairs):
    for i, j in pairs:
        a, b = xs[i], xs[j]
        if b is None:
            continue
        if a is None:
            xs[i], xs[j] = b, None
        else:
            xs[i], xs[j] = jnp.maximum(a, b), jnp.minimum(a, b)


def _merge_top16(xs, ys):
    zs = []
    for k in range(16):
        a, b = xs[k], ys[15 - k]
        zs.append(a if b is None else (b if a is None else jnp.maximum(a, b)))
    _compare_exchange(zs, _BITONIC16)
    return zs


def _top16_of_128(x3):
    xs = [x3[r] for r in range(16)]
    _compare_exchange(xs, _SORT16)
    for shift in (4, 2, 1):
        xs = _merge_top16(xs, [pltpu.roll(x, shift, 0) for x in xs])
    return xs


def _peer_select(s1, s2):
    n = s1.shape[-1]
    a3 = s1.reshape(16, 8, n)
    b3 = s2.reshape(16, 8, n)
    t1 = _top16_of_128(a3)
    t2 = _top16_of_128(b3)
    cand = [[t1[a] + t2[b] for b in range(PEER_TOPK // (a + 1))] for a in range(PEER_TOPK)]
    pad = lambda xs: xs + [None] * (16 - len(xs))
    top = cand[0]
    for a in range(1, 8):
        top = _merge_top16(top, pad(cand[a]))
    top = _merge_top16(top, pad([cand[a][0] for a in range(8, 16)]))
    tau = top[15]
    cmax = cand[0][0]
    inf = jnp.full_like(tau, jnp.inf)
    z = jnp.zeros_like(tau)
    thr = []
    for a in range(PEER_TOPK):
        th = inf
        for b, cv in enumerate(cand[a]):
            sel = cv >= tau
            th = jnp.where(sel, jnp.minimum(th, t2[b]), th)
            z = z + jnp.where(sel, jnp.exp(cv - cmax), 0.0)
        thr.append(th)
    theta = jnp.full_like(a3, jnp.inf)
    for a in range(PEER_TOPK):
        theta = jnp.where(a3 == t1[a][None], thr[a][None], theta)
    c = jnp.exp(a3 - t1[0][None]) / z[None]
    e2 = jnp.exp(b3 - t2[0][None])
    return theta.reshape(PEER_NKEYS, n), c.reshape(PEER_NKEYS, n), e2.reshape(PEER_NKEYS, n)


def _peer_kernel(ht_ref, wq_ref, sk_ref, u_ref, vt_ref, o_ref, th_s, c_s, s2_s, e2_s, a_s, w_s, acc_s):
    e = pl.program_id(1)
    tb = ht_ref.shape[1]

    @pl.when(e == 0)
    def _():
        qt = jnp.dot(wq_ref[...], ht_ref[...], preferred_element_type=F32)
        sk = sk_ref[...]
        for h in range(PEER_HEADS):
            s1 = jnp.dot(sk[0], qt[h * 128:h * 128 + 64], precision=HIGHEST, preferred_element_type=F32)
            s2 = jnp.dot(sk[1], qt[h * 128 + 64:h * 128 + 128], precision=HIGHEST, preferred_element_type=F32)
            theta, c, e2 = _peer_select(s1, s2)
            th_s[h] = theta.reshape(PEER_NKEYS // 8, 8, tb)
            c_s[h] = c.reshape(PEER_NKEYS // 8, 8, tb)
            s2_s[h] = s2
            e2_s[h] = e2
        acc_s[...] = jnp.zeros_like(acc_s)

    a_s[...] = jnp.dot(u_ref[...], ht_ref[...], preferred_element_type=F32)
    for lc in range(tb // 128):
        lanes = slice(lc * 128, (lc + 1) * 128)
        for i in range(PEER_ROWS):
            g = jnp.zeros((PEER_NKEYS, 128), F32)
            for h in range(PEER_HEADS):
                th = th_s[h, e, i:i + 1, lanes]
                cw = c_s[h, e, i:i + 1, lanes]
                g = g + jnp.where(s2_s[h, :, lanes] >= th, e2_s[h, :, lanes] * cw, 0.0)
            rows = slice(i * PEER_NKEYS, (i + 1) * PEER_NKEYS)
            w_s[rows, lanes] = (g * jax.nn.gelu(a_s[rows, lanes], approximate=True)).astype(BF16)
    acc_s[...] += jnp.dot(vt_ref[...], w_s[...], preferred_element_type=F32)

    @pl.when(e == pl.num_programs(1) - 1)
    def _():
        o_ref[...] = acc_s[...].T


def _peer(ht, wq_t, sk, u, v_t):
    n = ht.shape[1]
    tb = PEER_TB
    ne = PEER_EXPERTS // PEER_EB
    sel = pltpu.VMEM((PEER_HEADS, PEER_NKEYS, tb), F32)
    sel8 = pltpu.VMEM((PEER_HEADS, PEER_NKEYS // 8, 8, tb), F32)
    return pl.pallas_call(
        _peer_kernel,
        grid=(n // tb, ne),
        in_specs=[pl.BlockSpec((D, tb), lambda t, e: (0, t)),
                  pl.BlockSpec((D, D), lambda t, e: (0, 0)),
                  pl.BlockSpec((2, PEER_NKEYS, PEER_HALF), lambda t, e: (0, 0, 0)),
                  pl.BlockSpec((PEER_EB, D), lambda t, e: (e, 0)),
                  pl.BlockSpec((D, PEER_EB), lambda t, e: (0, e))],
        out_specs=pl.BlockSpec((tb, D), lambda t, e: (t, 0)),
        out_shape=jax.ShapeDtypeStruct((n, D), F32),
        scratch_shapes=[sel8, sel8, sel, sel,
                        pltpu.VMEM((PEER_EB, tb), F32), pltpu.VMEM((PEER_EB, tb), BF16),
                        pltpu.VMEM((D, tb), F32)],
        compiler_params=_params("arbitrary", "arbitrary"),
        name="peer",
    )(ht, wq_t, sk, u, v_t)


def _final_kernel(x_ref, pr_ref, mod_ref, g_ref, o_ref):
    x = x_ref[0] + mod_ref[0][:, 5 * D:6 * D] * pr_ref[0]
    o_ref[0] = _rms(x) * g_ref[...]


def _final(x, peer, mod, g):
    nb = x.shape[0]
    lat = pl.BlockSpec((1, TM, D), lambda b, t: (b, t + 1, 0))
    return pl.pallas_call(
        _final_kernel,
        grid=(nb, SEQ // TM),
        in_specs=[lat, lat, pl.BlockSpec((1, 1, 6 * D), lambda b, t: (b, 0, 0)),
                  pl.BlockSpec((1, D), lambda b, t: (0, 0))],
        out_specs=pl.BlockSpec((1, TM, D), lambda b, t: (b, t, 0)),
        out_shape=jax.ShapeDtypeStruct((nb, SEQ, D), F32),
        compiler_params=_params("arbitrary", "arbitrary"),
        name="final_norm",
    )(x, peer, mod, g)


def _rope_tables():
    col = np.arange(DIFF_HEADS * 2 * DIFF_DH)
    d = col % DIFF_DH
    half = DIFF_DH // 4
    inv = ROPE_BASE ** (-jnp.arange(half, dtype=F32) / half)
    inv_c = inv[d % half]
    by_col = jnp.asarray((d // (DIFF_DH // 2)) == 1)
    t = jnp.arange(SEQ)
    pos = jnp.where(by_col[None, :], (t % GRID_W)[:, None], (t // GRID_W)[:, None]).astype(F32)
    ang = pos * inv_c[None, :]
    cos = jnp.concatenate([jnp.ones((CTX, col.size), F32), jnp.cos(ang)], axis=0)
    sin = jnp.concatenate([jnp.zeros((CTX, col.size), F32), jnp.sin(ang)], axis=0)
    return cos, sin


def _rot_columns(w):
    col = np.arange(w.shape[1])
    lo = (col % (DIFF_DH // 2)) < DIFF_DH // 4
    src = np.where(lo, col + DIFF_DH // 4, col - DIFF_DH // 4)
    sign = np.where(lo, -1.0, 1.0).astype(np.float32)
    return w[:, src] * sign[None, :]


def _regroup_w_in(w):
    sizes = (128, 128, 256, 256, 16, 16, 256, 256, 256, 512, 512, 512)
    pts = np.cumsum((0,) + sizes)
    p = [w[:, pts[i]:pts[i + 1]] for i in range(12)]
    downs = jnp.concatenate([p[4], p[5], jnp.zeros((D, 128 - 2 * GLA_RANK), w.dtype)], axis=1)
    cols = [p[0], p[1], p[2], p[3], downs, p[6], p[7], p[8], p[9], p[10], p[11], _rot_columns(p[9]), _rot_columns(p[10])]
    return jnp.concatenate(cols, axis=1).astype(BF16)


def _gate_up_padded(gate_up):
    z = jnp.zeros((128, 128), F32)
    gf = z.at[0:GLA_RANK].set(gate_up[0]).astype(BF16)
    gb = z.at[GLA_RANK:2 * GLA_RANK].set(gate_up[1]).astype(BF16)
    return gf, gb


def kernel(x, c, ctx, c_ctx, ada_w, ada_b, norm1_g, norm2_g, w_in, gla_gate_up, gla_gate_b, gla_norm_g, nat_rpb,
           diff_lambda, diff_norm_g, w_out, peer_wq, peer_subkeys, peer_u, peer_v, final_g):
    nb = x.shape[0]
    depth = ada_w.shape[0]
    assert x.shape[1:] == (SEQ, D) and ctx.shape[1:] == (CTX, D) and (nb * T) % PEER_TB == 0

    rows = -(-(nb + 1) // 8) * 8
    cc = jnp.concatenate([c, c_ctx[None], jnp.zeros((rows - nb - 1, D), F32)], axis=0)
    mod_all = _modulation(cc, ada_w, ada_b)
    cos, sin = _rope_tables()

    xa = jnp.concatenate([ctx, x], axis=1)
    peer = None
    mod_prev = None
    for l in range(depth):
        lam_init = 0.8 - 0.6 * math.exp(-0.3 * l)
        mod = mod_all[l].reshape(rows, 1, 6 * D)
        gf, gb = _gate_up_padded(gla_gate_up[l])
        xa, (gq, gk, ggf, ggb, gv, gr, nq, nk, nv, dq, dk, dv) = _norm_proj(
            xa, peer, mod_prev, mod, norm1_g[l][None], _regroup_w_in(w_in[l]), gf, gb, gla_gate_b[l], cos, sin)
        yg = _gla(gq, gk, ggf, ggb, gv, gr, gla_norm_g[l][None])
        yn = _nat(nq, nk, nv, _nat_bias_table(nat_rpb[l]))
        yd = _diff(dq, dk, dv, diff_lambda[l], diff_norm_g[l][None], lam_init)
        xa, ht = _out_proj(xa, yg, yn, yd, mod, w_out[l].astype(BF16), norm2_g[l][None])
        peer = _peer(ht, peer_wq[l].T.astype(BF16), peer_subkeys[l],
                     peer_u[l].astype(BF16), peer_v[l].T.astype(BF16)).reshape(nb, T, D)
        mod_prev = mod
    return _final(xa, peer, mod_prev, final_g[None])
```

```python
import functools
import math

import numpy as np
import jax
import jax.numpy as jnp
from jax import lax
from jax.experimental import pallas as pl
from jax.experimental.pallas import tpu as pltpu

F32 = jnp.float32
BF16 = jnp.bfloat16
HIGHEST = lax.Precision.HIGHEST

D = 1024
SEQ = 2048
CTX = 256
T = SEQ + CTX
GRID_W = 64
GRID_ROWS = SEQ // GRID_W
EPS = 1e-6
TM = 256
NT_TILES = T // TM

GLA_HEADS, GLA_DK, GLA_DV, GLA_RANK, GLA_TAU, GLA_CHUNK = 4, 32, 64, 16, 16.0, 64
NAT_HEADS, NAT_DH, NAT_WIN_R, NAT_WIN_C = 4, 64, 8, 16
DIFF_HEADS, DIFF_DH = 4, 64
ROPE_BASE = 10000.0
PEER_HEADS, PEER_NKEYS, PEER_HALF, PEER_TOPK = 8, 128, 64, 16
PEER_EXPERTS = PEER_NKEYS * PEER_NKEYS
PEER_TB = 512
PEER_ROWS = 8
PEER_EB = PEER_ROWS * PEER_NKEYS

C_GQ, C_GK, C_GV, C_GR, C_GD = 0, 128, 256, 512, 768
C_NQ, C_NK, C_NV = 896, 1152, 1408
C_DQ, C_DK, C_DV, C_DQR, C_DKR = 1664, 2176, 2688, 3200, 3712
PROJ_COLS = 4224

_NT_DIMS = (((1,), (1,)), ((), ()))


def _nt(a, b):
    return lax.dot_general(a, b, _NT_DIMS, preferred_element_type=F32)


def _params(*sem):
    return pltpu.CompilerParams(dimension_semantics=sem, vmem_limit_bytes=56 * 1024 * 1024)


def _rms(x):
    return x * lax.rsqrt(jnp.mean(x * x, axis=-1, keepdims=True) + EPS)


def _sigmoid(x):
    return 1.0 / (1.0 + jnp.exp(-x))


def _log_sigmoid(x):
    return jnp.minimum(x, 0.0) - jnp.log1p(jnp.exp(-jnp.abs(x)))


def _mod_kernel(c_ref, w_ref, b_ref, o_ref):
    c = c_ref[...]
    o_ref[0] = jnp.dot(c * _sigmoid(c), w_ref[0], precision=HIGHEST, preferred_element_type=F32) + b_ref[0]


def _modulation(cc, ada_w, ada_b):
    depth = ada_w.shape[0]
    rows = cc.shape[0]
    return pl.pallas_call(
        _mod_kernel,
        grid=(depth, 6),
        in_specs=[pl.BlockSpec((rows, D), lambda l, j: (0, 0)),
                  pl.BlockSpec((1, D, D), lambda l, j: (l, 0, j)),
                  pl.BlockSpec((1, 1, D), lambda l, j: (l, 0, j))],
        out_specs=pl.BlockSpec((1, rows, D), lambda l, j: (l, 0, j)),
        out_shape=jax.ShapeDtypeStruct((depth, rows, 6 * D), F32),
        compiler_params=_params("arbitrary", "arbitrary"),
        name="adaln_mod",
    )(cc, ada_w, ada_b.reshape(depth, 1, 6 * D))


def _proj_kernel(*refs, fuse_res):
    if fuse_res:
        x_ref, pr_ref, modp_ref = refs[:3]
        refs = refs[3:]
    else:
        x_ref = refs[0]
        refs = refs[1:]
    (mod_ref, g_ref, w_ref, gf_ref, gb_ref, gbias_ref, cos_ref, sin_ref) = refs[:8]
    outs = refs[8:]
    if fuse_res:
        xo_ref = outs[0]
        outs = outs[1:]
    (gq_ref, gk_ref, ggf_ref, ggb_ref, gv_ref, gr_ref, nq_ref, nk_ref, nv_ref, dq_ref, dk_ref, dv_ref) = outs

    x = x_ref[0]
    if fuse_res:
        x = x + modp_ref[0][:, 5 * D:6 * D] * pr_ref[0]
        xo_ref[0] = x
    m = mod_ref[0]
    h = _rms(x) * g_ref[...] * (1.0 + m[:, D:2 * D]) + m[:, 0:D]
    p = jnp.dot(h.astype(BF16), w_ref[...], preferred_element_type=F32)

    gq_ref[0] = p[:, C_GQ:C_GQ + 128] * (GLA_DK ** -0.5)
    gk_ref[0] = p[:, C_GK:C_GK + 128]
    gv_ref[0] = p[:, C_GV:C_GV + 256]
    gr_ref[0] = p[:, C_GR:C_GR + 256]
    pd = p[:, C_GD:C_GD + 128].astype(BF16)
    gbias = gbias_ref[...]
    xf = jnp.dot(pd, gf_ref[...], preferred_element_type=F32) + gbias[0:1]
    xb = jnp.dot(pd, gb_ref[...], preferred_element_type=F32) + gbias[1:2]
    ggf_ref[0] = _log_sigmoid(xf) / GLA_TAU
    ggb_ref[0] = _log_sigmoid(xb) / GLA_TAU

    nq_ref[0] = (p[:, C_NQ:C_NQ + 256] * (NAT_DH ** -0.5)).astype(BF16)
    nk_ref[0] = p[:, C_NK:C_NK + 256].astype(BF16)
    nv_ref[0] = p[:, C_NV:C_NV + 256].astype(BF16)

    cos = cos_ref[...]
    sin = sin_ref[...]
    dq = p[:, C_DQ:C_DQ + 512] * cos + p[:, C_DQR:C_DQR + 512] * sin
    dq_ref[0] = (dq * (DIFF_DH ** -0.5)).astype(BF16)
    dk_ref[0] = (p[:, C_DK:C_DK + 512] * cos + p[:, C_DKR:C_DKR + 512] * sin).astype(BF16)
    dv_ref[0] = p[:, C_DV:C_DV + 512].astype(BF16)


def _mod_spec(nb):
    return pl.BlockSpec((1, 1, 6 * D), lambda b, t: (jnp.where(t == 0, nb, b), 0, 0))


def _norm_proj(x, peer, mod_prev, mod, g1, w_all, gf, gb, gbias, cos, sin):
    nb = x.shape[0]
    fuse = peer is not None
    tile = lambda w: pl.BlockSpec((1, TM, w), lambda b, t: (b, t, 0))
    full = lambda a: pl.BlockSpec(a.shape, lambda b, t: (0,) * a.ndim)
    ins, specs = [x], [tile(D)]
    if fuse:
        ins += [peer, mod_prev]
        specs += [tile(D), _mod_spec(nb)]
    ins += [mod, g1, w_all, gf, gb, gbias, cos, sin]
    specs += [_mod_spec(nb), full(g1), full(w_all), full(gf), full(gb), full(gbias),
              pl.BlockSpec((TM, 512), lambda b, t: (t, 0)), pl.BlockSpec((TM, 512), lambda b, t: (t, 0))]
    widths = [(128, F32)] * 4 + [(256, F32)] * 2 + [(256, BF16)] * 3 + [(512, BF16)] * 3
    out_shape = [jax.ShapeDtypeStruct((nb, T, w), dt) for w, dt in widths]
    out_specs = [tile(w) for w, _ in widths]
    if fuse:
        out_shape = [jax.ShapeDtypeStruct((nb, T, D), F32)] + out_shape
        out_specs = [tile(D)] + out_specs
    res = pl.pallas_call(
        functools.partial(_proj_kernel, fuse_res=fuse),
        grid=(nb, NT_TILES),
        in_specs=specs, out_specs=out_specs, out_shape=out_shape,
        compiler_params=_params("arbitrary", "arbitrary"),
        name="norm_proj",
    )(*ins)
    if fuse:
        return res[0], res[1:]
    return x, res


def _gla_kernel(q_ref, k_ref, gf_ref, gb_ref, v_ref, r_ref, ng_ref, o_ref, of_s, ob_s, sf_s, sb_s):
    C = GLA_CHUNK
    nch = T // C
    nctx = CTX // C
    ii = lax.broadcasted_iota(jnp.int32, (C, C), 0)
    jj = lax.broadcasted_iota(jnp.int32, (C, C), 1)
    tri_f = (jj <= ii).astype(F32)
    tri_b = (jj >= ii).astype(F32)
    it = lax.broadcasted_iota(jnp.int32, (C, 4 * C), 0)
    jt = lax.broadcasted_iota(jnp.int32, (C, 4 * C), 1) & (C - 1)
    cm_f = jt <= it
    cm_b = jt >= it
    kmask = ((lax.broadcasted_iota(jnp.int32, (4 * C, 128), 0) >> 6)
             == (lax.broadcasted_iota(jnp.int32, (4 * C, 128), 1) >> 5)).astype(F32)
    vmask = ((lax.broadcasted_iota(jnp.int32, (4 * C, 256), 0) >> 6)
             == (lax.broadcasted_iota(jnp.int32, (4 * C, 256), 1) >> 6)).astype(F32)
    smask = ((lax.broadcasted_iota(jnp.int32, (256, 128), 0) >> 6)
             == (lax.broadcasted_iota(jnp.int32, (256, 128), 1) >> 5)).astype(F32)

    sf_s[...] = jnp.zeros_like(sf_s)
    sb_s[...] = jnp.zeros_like(sb_s)

    def chunk(off, g_ref, s_ref, dst, tri, cm, last, mid):
        rows = pl.ds(off, C)
        q = q_ref[0, rows, :]
        k = k_ref[0, rows, :]
        v = v_ref[0, rows, :]
        g = g_ref[0, rows, :]
        st = s_ref[...]
        b = jnp.dot(tri, g, precision=HIGHEST, preferred_element_type=F32)
        b_last = b[last:last + 1]
        b_mid = b[mid:mid + 1]
        qi = q * jnp.exp(jnp.clip(b - b_mid, -80.0, 80.0))
        ki = k * jnp.exp(jnp.clip(b_mid - b, -80.0, 80.0))
        krows = jnp.concatenate([ki, ki, ki, ki], axis=0) * kmask
        a = jnp.where(cm, _nt(qi.astype(BF16), krows.astype(BF16)), 0.0)
        vblk = jnp.concatenate([v, v, v, v], axis=0) * vmask
        o = jnp.dot(a.astype(BF16), vblk.astype(BF16), preferred_element_type=F32)
        o = o + _nt((q * jnp.exp(b)).astype(BF16), st.astype(BF16))
        dst[rows, :] = o
        kd = k * jnp.exp(b_last - b)
        upd = jnp.dot(v.T.astype(BF16), kd.astype(BF16), preferred_element_type=F32)
        s_ref[...] = st * jnp.exp(b_last) + upd * smask

    def body(t, carry):
        cb = jnp.where(t < nctx, nctx - 1 - t, nch + nctx - 1 - t)
        chunk(pl.multiple_of(t * C, C), gf_ref, sf_s, of_s, tri_f, cm_f, C - 1, C // 2 - 1)
        chunk(pl.multiple_of(cb * C, C), gb_ref, sb_s, ob_s, tri_b, cm_b, 0, C // 2)
        return carry

    lax.fori_loop(0, nch, body, 0)

    hones = ((lax.broadcasted_iota(jnp.int32, (256, 256), 0) >> 6)
             == (lax.broadcasted_iota(jnp.int32, (256, 256), 1) >> 6)).astype(F32) * (1.0 / GLA_DV)
    ng = ng_ref[...]
    for t in range(NT_TILES):
        rows = pl.ds(t * TM, TM)
        o = of_s[rows, :] + ob_s[rows, :]
        ms = jnp.dot(o * o, hones, precision=HIGHEST, preferred_element_type=F32)
        r = r_ref[0, rows, :]
        o_ref[0, rows, :] = (o * lax.rsqrt(ms + EPS) * ng * (r * _sigmoid(r))).astype(BF16)


def _gla(gq, gk, ggf, ggb, gv, gr, ng):
    nb = gq.shape[0]
    seq = lambda w: pl.BlockSpec((1, T, w), lambda b: (b, 0, 0))
    return pl.pallas_call(
        _gla_kernel,
        grid=(nb,),
        in_specs=[seq(128), seq(128), seq(128), seq(128), seq(256), seq(256),
                  pl.BlockSpec((1, 256), lambda b: (0, 0))],
        out_specs=seq(256),
        out_shape=jax.ShapeDtypeStruct((nb, T, 256), BF16),
        scratch_shapes=[pltpu.VMEM((T, 256), F32), pltpu.VMEM((T, 256), F32),
                        pltpu.VMEM((256, 128), F32), pltpu.VMEM((256, 128), F32)],
        compiler_params=_params("arbitrary"),
        name="gla",
    )(gq, gk, ggf, ggb, gv, gr, ng)


def _nat_kernel(q_ref, k_ref, v_ref, bias_ref, o_ref):
    lane = lax.broadcasted_iota(jnp.int32, (1, 256), 1) >> 6
    hm_b = [(lane == h).astype(BF16) for h in range(NAT_HEADS)]
    hm_f = [(lane == h).astype(F32) for h in range(NAT_HEADS)]

    qc = q_ref[0, 0:CTX, :]
    kc = k_ref[0, 0:CTX, :]
    vc = v_ref[0, 0:CTX, :]
    acc = jnp.zeros((CTX, 256), F32)
    for h in range(NAT_HEADS):
        s = _nt(qc * hm_b[h], kc)
        e = jnp.exp(s - jnp.max(s, axis=-1, keepdims=True))
        p = (e / jnp.sum(e, axis=-1, keepdims=True)).astype(BF16)
        acc = acc + jnp.dot(p, vc, preferred_element_type=F32) * hm_f[h]
    o_ref[0, 0:CTX, :] = acc.astype(BF16)

    nwin = NAT_WIN_R * GRID_W

    def row(r, carry):
        rs = jnp.clip(r - NAT_WIN_R // 2, 0, GRID_ROWS - NAT_WIN_R)
        case = rs - r + NAT_WIN_R - 1
        qrows = pl.ds(pl.multiple_of(CTX + r * GRID_W, GRID_W), GRID_W)
        wrows = pl.ds(pl.multiple_of(CTX + rs * GRID_W, GRID_W), nwin)
        q = q_ref[0, qrows, :]
        kw = k_ref[0, wrows, :]
        vw = v_ref[0, wrows, :]
        kcx = k_ref[0, 0:CTX, :]
        vcx = v_ref[0, 0:CTX, :]
        acc = jnp.zeros((GRID_W, 256), F32)
        for h in range(NAT_HEADS):
            qh = q * hm_b[h]
            sw = _nt(qh, kw) + bias_ref[h, case]
            sc = _nt(qh, kcx)
            m = jnp.maximum(jnp.max(sw, axis=-1, keepdims=True), jnp.max(sc, axis=-1, keepdims=True))
            ew = jnp.exp(sw - m)
            ec = jnp.exp(sc - m)
            l = jnp.sum(ew, axis=-1, keepdims=True) + jnp.sum(ec, axis=-1, keepdims=True)
            o = jnp.dot((ew / l).astype(BF16), vw, preferred_element_type=F32)
            o = o + jnp.dot((ec / l).astype(BF16), vcx, preferred_element_type=F32)
            acc = acc + o * hm_f[h]
        o_ref[0, qrows, :] = acc.astype(BF16)
        return carry

    lax.fori_loop(0, GRID_ROWS, row, 0)


def _nat(nq, nk, nv, bias):
    nb = nq.shape[0]
    seq = pl.BlockSpec((1, T, 256), lambda b: (b, 0, 0))
    return pl.pallas_call(
        _nat_kernel,
        grid=(nb,),
        in_specs=[seq, seq, seq, pl.BlockSpec(bias.shape, lambda b: (0, 0, 0, 0))],
        out_specs=seq,
        out_shape=jax.ShapeDtypeStruct((nb, T, 256), BF16),
        compiler_params=_params("arbitrary"),
        name="nat",
    )(nq, nk, nv, bias)


def _nat_bias_table(rpb):
    cidx = np.arange(GRID_W)
    col_start = np.clip(cidx - NAT_WIN_C // 2, 0, GRID_W - NAT_WIN_C)
    col_mask = (cidx[None, :] >= col_start[:, None]) & (cidx[None, :] < col_start[:, None] + NAT_WIN_C)
    dc = np.clip(cidx[None, :] - cidx[:, None] + NAT_WIN_C - 1, 0, 2 * NAT_WIN_C - 2)
    dr = np.arange(NAT_WIN_R)[:, None] + np.arange(NAT_WIN_R)[None, :]
    t = rpb.astype(F32)[:, dr][:, :, :, dc]
    t = jnp.where(col_mask[None, None, None], t, -jnp.inf)
    return t.transpose(0, 1, 3, 2, 4).reshape(NAT_HEADS, NAT_WIN_R, GRID_W, NAT_WIN_R * GRID_W)


def _diff_kernel(lam_ref, q_ref, k_ref, v_ref, g_ref, o_ref, *, lam_init):
    lv = lam_ref[...]
    lam = (jnp.exp(jnp.sum(lv[0:1] * lv[1:2], axis=-1, keepdims=True))
           - jnp.exp(jnp.sum(lv[2:3] * lv[3:4], axis=-1, keepdims=True)) + lam_init)
    q = q_ref[0]
    first = lax.broadcasted_iota(jnp.int32, (1, 2 * DIFF_DH), 1) < DIFF_DH
    q1 = jnp.where(first, q, jnp.zeros_like(q))
    q2 = jnp.where(first, jnp.zeros_like(q), q)

    def attend(k, v):
        s1 = _nt(q1, k)
        s2 = _nt(q2, k)
        e1 = jnp.exp(s1 - jnp.max(s1, axis=-1, keepdims=True))
        e2 = jnp.exp(s2 - jnp.max(s2, axis=-1, keepdims=True))
        w1 = 1.0 / jnp.sum(e1, axis=-1, keepdims=True)
        w2 = lam / jnp.sum(e2, axis=-1, keepdims=True)
        p = (e1 * w1 - e2 * w2).astype(BF16)
        o = jnp.dot(p, v, preferred_element_type=F32)
        o_ref[0] = (_rms(o) * g_ref[...] * (1.0 - lam_init)).astype(BF16)

    @pl.when(pl.program_id(2) == 0)
    def _():
        attend(k_ref[0, 0:CTX, :], v_ref[0, 0:CTX, :])

    @pl.when(pl.program_id(2) > 0)
    def _():
        attend(k_ref[0], v_ref[0])


def _diff(dq, dk, dv, lam_vec, dg, lam_init):
    nb = dq.shape[0]
    w = 2 * DIFF_DH
    return pl.pallas_call(
        functools.partial(_diff_kernel, lam_init=lam_init),
        grid=(nb, DIFF_HEADS, NT_TILES),
        in_specs=[pl.BlockSpec((4, DIFF_DH), lambda b, h, t: (0, 0)),
                  pl.BlockSpec((1, TM, w), lambda b, h, t: (b, t, h)),
                  pl.BlockSpec((1, T, w), lambda b, h, t: (b, 0, h)),
                  pl.BlockSpec((1, T, w), lambda b, h, t: (b, 0, h)),
                  pl.BlockSpec((1, w), lambda b, h, t: (0, h))],
        out_specs=pl.BlockSpec((1, TM, w), lambda b, h, t: (b, t, h)),
        out_shape=jax.ShapeDtypeStruct((nb, T, DIFF_HEADS * w), BF16),
        compiler_params=_params("arbitrary", "arbitrary", "arbitrary"),
        name="diff_attn",
    )(lam_vec, dq, dk, dv, dg)


def _out_kernel(x_ref, yg_ref, yn_ref, yd_ref, mod_ref, w_ref, g_ref, xo_ref, ht_ref):
    attn = jnp.dot(yg_ref[0], w_ref[0:256, :], preferred_element_type=F32)
    attn = attn + jnp.dot(yn_ref[0], w_ref[256:512, :], preferred_element_type=F32)
    attn = attn + jnp.dot(yd_ref[0], w_ref[512:1024, :], preferred_element_type=F32)
    m = mod_ref[0]
    x = x_ref[0] + m[:, 2 * D:3 * D] * attn
    xo_ref[0] = x
    h2 = _rms(x) * g_ref[...] * (1.0 + m[:, 4 * D:5 * D]) + m[:, 3 * D:4 * D]
    ht_ref[...] = h2.T.astype(BF16)


def _out_proj(x, yg, yn, yd, mod, w_out, g2):
    nb = x.shape[0]
    tile = lambda w: pl.BlockSpec((1, TM, w), lambda b, t: (b, t, 0))
    return pl.pallas_call(
        _out_kernel,
        grid=(nb, NT_TILES),
        in_specs=[tile(D), tile(256), tile(256), tile(512), _mod_spec(nb),
                  pl.BlockSpec((D, D), lambda b, t: (0, 0)), pl.BlockSpec((1, D), lambda b, t: (0, 0))],
        out_specs=[tile(D), pl.BlockSpec((D, TM), lambda b, t: (0, b * NT_TILES + t))],
        out_shape=[jax.ShapeDtypeStruct((nb, T, D), F32), jax.ShapeDtypeStruct((D, nb * T), BF16)],
        compiler_params=_params("arbitrary", "arbitrary"),
        name="out_proj",
    )(x, yg, yn, yd, mod, w_out, g2)


def _oddeven_pairs(n):
    pairs = []

    def merge(lo, hi, r):
        step = r * 2
        if step < hi - lo:
            merge(lo, hi, step)
            merge(lo + r, hi, step)
            pairs.extend((i, i + r) for i in range(lo + r, hi - r, step))
        else:
            pairs.append((lo, lo + r))

    def sort(lo, hi):
        if hi - lo >= 1:
            mid = lo + (hi - lo) // 2
            sort(lo, mid)
            sort(mid + 1, hi)
            merge(lo, hi, 1)

    sort(0, n - 1)
    return pairs


def _bitonic_pairs(n):
    pairs = []
    d = n // 2
    while d >= 1:
        pairs.extend((i, i + d) for i in range(n) if (i // d) % 2 == 0)
        d //= 2
    return pairs


_SORT16 = _oddeven_pairs(16)
_BITONIC16 = _bitonic_pairs(16)


def _compare_exchange(xs, pairs):
    for i, j in pairs:
        a, b = xs[i], xs[j]
        if b is None:
            continue
        if a is None:
            xs[i], xs[j] = b, None
        else:
            xs[i], xs[j] = jnp.maximum(a, b), jnp.minimum(a, b)


def _merge_top16(xs, ys):
    zs = []
    for k in range(16):
        a, b = xs[k], ys[15 - k]
        zs.append(a if b is None else (b if a is None else jnp.maximum(a, b)))
    _compare_exchange(zs, _BITONIC16)
    return zs


def _top16_of_128(x3):
    xs = [x3[r] for r in range(16)]
    _compare_exchange(xs, _SORT16)
    for shift in (4, 2, 1):
        xs = _merge_top16(xs, [pltpu.roll(x, shift, 0) for x in xs])
    return xs


def _peer_select(s1, s2):
    n = s1.shape[-1]
    a3 = s1.reshape(16, 8, n)
    b3 = s2.reshape(16, 8, n)
    t1 = _top16_of_128(a3)
    t2 = _top16_of_128(b3)
    cand = [[t1[a] + t2[b] for b in range(PEER_TOPK // (a + 1))] for a in range(PEER_TOPK)]
    pad = lambda xs: xs + [None] * (16 - len(xs))
    top = cand[0]
    for a in range(1, 8):
        top = _merge_top16(top, pad(cand[a]))
    top = _merge_top16(top, pad([cand[a][0] for a in range(8, 16)]))
    tau = top[15]
    cmax = cand[0][0]
    z = jnp.zeros_like(tau)
    count = []
    for a in range(PEER_TOPK):
        n_a = jnp.zeros_like(tau)
        for cv in cand[a]:
            sel = cv >= tau
            n_a = n_a + jnp.where(sel, 1.0, 0.0)
            z = z + jnp.where(sel, jnp.exp(cv - cmax), 0.0)
        count.append(n_a)
    lrow = jnp.zeros_like(a3)
    for a in range(PEER_TOPK):
        lrow = jnp.where(a3 == t1[a][None], count[a][None], lrow)
    rank2 = jnp.zeros_like(b3)
    for b in range(PEER_TOPK):
        rank2 = rank2 + jnp.where(b3 < t2[b][None], 1.0, 0.0)
    c = jnp.exp(a3 - t1[0][None]) / z[None]
    e2 = jnp.exp(b3 - t2[0][None])
    return lrow, c, rank2.reshape(PEER_NKEYS, n), e2.reshape(PEER_NKEYS, n)


def _peer_kernel(ht_ref, wq_ref, sk_ref, u_ref, vt_ref, o_ref, l_s, c_s, r2_s, e2_s, w0_s, w1_s, acc_s):
    e = pl.program_id(1)
    ne = pl.num_programs(1) - 1
    tb = ht_ref.shape[1]
    w_bufs = (w0_s, w1_s)

    @pl.when(e == 0)
    def _():
        qt = jnp.dot(wq_ref[...], ht_ref[...], preferred_element_type=F32)
        sk = sk_ref[...]
        for h in range(PEER_HEADS):
            s1 = jnp.dot(sk[0], qt[h * 128:h * 128 + 64], precision=HIGHEST, preferred_element_type=F32)
            s2 = jnp.dot(sk[1], qt[h * 128 + 64:h * 128 + 128], precision=HIGHEST, preferred_element_type=F32)
            lrow, c, rank2, e2 = _peer_select(s1, s2)
            l_s[h] = lrow
            c_s[h] = c
            r2_s[h] = rank2.astype(BF16)
            e2_s[h] = e2.astype(BF16)
        acc_s[...] = jnp.zeros_like(acc_s)
        w1_s[...] = jnp.zeros_like(w1_s)

    nk = PEER_ROWS // 2
    nn = tb // 256

    def a_piece(k, n):
        return jnp.dot(u_ref[k * 256:(k + 1) * 256, :], ht_ref[:, n * 256:(n + 1) * 256], preferred_element_type=F32)

    def drain_piece(w_prev, r, n):
        rows, cols = slice(r * 256, (r + 1) * 256), slice(n * 256, (n + 1) * 256)
        acc_s[rows, cols] += jnp.dot(vt_ref[rows, :], w_prev[:, cols], preferred_element_type=F32)

    def build(w_cur, w_prev):
        a_cur = [a_piece(0, n) for n in range(nn)]
        for k in range(nk):
            a_next = []
            pieces = [functools.partial(drain_piece, w_prev, k, n) for n in range(nn)]
            if k + 1 < nk:
                pieces += [(lambda n=n: a_next.append(a_piece(k + 1, n))) for n in range(nn)]
            for lc in range(tb // 128):
                lanes = slice(lc * 128, (lc + 1) * 128)
                a = a_cur[lc // 2][:, (lc % 2) * 128:(lc % 2 + 1) * 128]
                for i2 in range(2):
                    i = 2 * k + i2
                    g = jnp.zeros((PEER_NKEYS // 16, 16, 128), BF16)
                    for h in range(PEER_HEADS):
                        lb = jnp.broadcast_to(l_s[h, e, i:i + 1, lanes], (16, 128)).astype(BF16)
                        cb = jnp.broadcast_to(c_s[h, e, i:i + 1, lanes], (16, 128)).astype(BF16)
                        r2 = r2_s[h, :, lanes].reshape(PEER_NKEYS // 16, 16, 128)
                        e2 = e2_s[h, :, lanes].reshape(PEER_NKEYS // 16, 16, 128)
                        g = g + jnp.where(r2 < lb[None], e2 * cb[None], jnp.zeros_like(e2))
                    act = jax.nn.gelu(a[i2 * PEER_NKEYS:(i2 + 1) * PEER_NKEYS], approximate=True).astype(BF16)
                    w_cur[i * PEER_NKEYS:(i + 1) * PEER_NKEYS, lanes] = (
                        g * act.reshape(PEER_NKEYS // 16, 16, 128)).reshape(PEER_NKEYS, 128)
                if lc < len(pieces):
                    pieces[lc]()
            a_cur = a_next

    for parity in range(2):
        pl.when((e < ne) & (e % 2 == parity))(functools.partial(build, w_bufs[parity], w_bufs[1 - parity]))

    @pl.when(e == ne)
    def _():
        for r in range(D // 256):
            for n in range(nn):
                drain_piece(w1_s, r, n)
        o_ref[...] = acc_s[...].T


def _peer(ht, wq_t, sk, u, v_t):
    n = ht.shape[1]
    tb = PEER_TB
    ne = PEER_EXPERTS // PEER_EB
    sel = pltpu.VMEM((PEER_HEADS, PEER_NKEYS, tb), BF16)
    sel8 = pltpu.VMEM((PEER_HEADS, PEER_NKEYS // 8, 8, tb), F32)
    return pl.pallas_call(
        _peer_kernel,
        grid=(n // tb, ne + 1),
        in_specs=[pl.BlockSpec((D, tb), lambda t, e: (0, t)),
                  pl.BlockSpec((D, D), lambda t, e: (0, 0)),
                  pl.BlockSpec((2, PEER_NKEYS, PEER_HALF), lambda t, e: (0, 0, 0)),
                  pl.BlockSpec((PEER_EB, D), lambda t, e: (jnp.minimum(e, ne - 1), 0)),
                  pl.BlockSpec((D, PEER_EB), lambda t, e: (0, jnp.maximum(e - 1, 0)))],
        out_specs=pl.BlockSpec((tb, D), lambda t, e: (t, 0)),
        out_shape=jax.ShapeDtypeStruct((n, D), F32),
        scratch_shapes=[sel8, sel8, sel, sel,
                        pltpu.VMEM((PEER_EB, tb), BF16), pltpu.VMEM((PEER_EB, tb), BF16), pltpu.VMEM((D, tb), F32)],
        compiler_params=_params("arbitrary", "arbitrary"),
        name="peer",
    )(ht, wq_t, sk, u, v_t)


def _final_kernel(x_ref, pr_ref, mod_ref, g_ref, o_ref):
    x = x_ref[0] + mod_ref[0][:, 5 * D:6 * D] * pr_ref[0]
    o_ref[0] = _rms(x) * g_ref[...]


def _final(x, peer, mod, g):
    nb = x.shape[0]
    lat = pl.BlockSpec((1, TM, D), lambda b, t: (b, t + 1, 0))
    return pl.pallas_call(
        _final_kernel,
        grid=(nb, SEQ // TM),
        in_specs=[lat, lat, pl.BlockSpec((1, 1, 6 * D), lambda b, t: (b, 0, 0)),
                  pl.BlockSpec((1, D), lambda b, t: (0, 0))],
        out_specs=pl.BlockSpec((1, TM, D), lambda b, t: (b, t, 0)),
        out_shape=jax.ShapeDtypeStruct((nb, SEQ, D), F32),
        compiler_params=_params("arbitrary", "arbitrary"),
        name="final_norm",
    )(x, peer, mod, g)


def _rope_tables():
    col = np.arange(DIFF_HEADS * 2 * DIFF_DH)
    d = col % DIFF_DH
    half = DIFF_DH // 4
    inv = ROPE_BASE ** (-jnp.arange(half, dtype=F32) / half)
    inv_c = inv[d % half]
    by_col = jnp.asarray((d // (DIFF_DH // 2)) == 1)
    t = jnp.arange(SEQ)
    pos = jnp.where(by_col[None, :], (t % GRID_W)[:, None], (t // GRID_W)[:, None]).astype(F32)
    ang = pos * inv_c[None, :]
    cos = jnp.concatenate([jnp.ones((CTX, col.size), F32), jnp.cos(ang)], axis=0)
    sin = jnp.concatenate([jnp.zeros((CTX, col.size), F32), jnp.sin(ang)], axis=0)
    return cos, sin


def _rot_columns(w):
    col = np.arange(w.shape[1])
    lo = (col % (DIFF_DH // 2)) < DIFF_DH // 4
    src = np.where(lo, col + DIFF_DH // 4, col - DIFF_DH // 4)
    sign = np.where(lo, -1.0, 1.0).astype(np.float32)
    return w[:, src] * sign[None, :]


def _regroup_w_in(w):
    sizes = (128, 128, 256, 256, 16, 16, 256, 256, 256, 512, 512, 512)
    pts = np.cumsum((0,) + sizes)
    p = [w[:, pts[i]:pts[i + 1]] for i in range(12)]
    downs = jnp.concatenate([p[4], p[5], jnp.zeros((D, 128 - 2 * GLA_RANK), w.dtype)], axis=1)
    cols = [p[0], p[1], p[2], p[3], downs, p[6], p[7], p[8], p[9], p[10], p[11], _rot_columns(p[9]), _rot_columns(p[10])]
    return jnp.concatenate(cols, axis=1).astype(BF16)


def _gate_up_padded(gate_up):
    z = jnp.zeros((128, 128), F32)
    gf = z.at[0:GLA_RANK].set(gate_up[0]).astype(BF16)
    gb = z.at[GLA_RANK:2 * GLA_RANK].set(gate_up[1]).astype(BF16)
    return gf, gb


def kernel(x, c, ctx, c_ctx, ada_w, ada_b, norm1_g, norm2_g, w_in, gla_gate_up, gla_gate_b, gla_norm_g, nat_rpb,
           diff_lambda, diff_norm_g, w_out, peer_wq, peer_subkeys, peer_u, peer_v, final_g):
    nb = x.shape[0]
    depth = ada_w.shape[0]
    assert x.shape[1:] == (SEQ, D) and ctx.shape[1:] == (CTX, D) and (nb * T) % PEER_TB == 0

    rows = -(-(nb + 1) // 8) * 8
    cc = jnp.concatenate([c, c_ctx[None], jnp.zeros((rows - nb - 1, D), F32)], axis=0)
    mod_all = _modulation(cc, ada_w, ada_b)
    cos, sin = _rope_tables()

    xa = jnp.concatenate([ctx, x], axis=1)
    peer = None
    mod_prev = None
    for l in range(depth):
        lam_init = 0.8 - 0.6 * math.exp(-0.3 * l)
        mod = mod_all[l].reshape(rows, 1, 6 * D)
        gf, gb = _gate_up_padded(gla_gate_up[l])
        xa, (gq, gk, ggf, ggb, gv, gr, nq, nk, nv, dq, dk, dv) = _norm_proj(
            xa, peer, mod_prev, mod, norm1_g[l][None], _regroup_w_in(w_in[l]), gf, gb, gla_gate_b[l], cos, sin)
        yg = _gla(gq, gk, ggf, ggb, gv, gr, gla_norm_g[l][None])
        yn = _nat(nq, nk, nv, _nat_bias_table(nat_rpb[l]))
        yd = _diff(dq, dk, dv, diff_lambda[l], diff_norm_g[l][None], lam_init)
        xa, ht = _out_proj(xa, yg, yn, yd, mod, w_out[l].astype(BF16), norm2_g[l][None])
        peer = _peer(ht, peer_wq[l].T.astype(BF16), peer_subkeys[l],
                     peer_u[l].astype(BF16), peer_v[l].T.astype(BF16)).reshape(nb, T, D)
        mod_prev = mod
    return _final(xa, peer, mod_prev, final_g[None])
```

```python
import functools
import math

import numpy as np
import jax
import jax.numpy as jnp
from jax import lax
from jax.experimental import pallas as pl
from jax.experimental.pallas import tpu as pltpu

F32 = jnp.float32
BF16 = jnp.bfloat16
F8 = jnp.float8_e4m3fn
F8_MAX = 448.0
HIGHEST = lax.Precision.HIGHEST

D = 1024
SEQ = 2048
CTX = 256
T = SEQ + CTX
GRID_W = 64
GRID_ROWS = SEQ // GRID_W
EPS = 1e-6
TM = 256
NT_TILES = T // TM

GLA_HEADS, GLA_DK, GLA_DV, GLA_RANK, GLA_TAU, GLA_CHUNK = 4, 32, 64, 16, 16.0, 64
NAT_HEADS, NAT_DH, NAT_WIN_R, NAT_WIN_C = 4, 64, 8, 16
DIFF_HEADS, DIFF_DH = 4, 64
ROPE_BASE = 10000.0
PEER_HEADS, PEER_NKEYS, PEER_HALF, PEER_TOPK = 8, 128, 64, 16
PEER_EXPERTS = PEER_NKEYS * PEER_NKEYS
PEER_TB = 512
PEER_ROWS = 8
PEER_EB = PEER_ROWS * PEER_NKEYS
PEER_U_SCALE = 32.0
PEER_W_SCALE = 4.0

C_GQ, C_GK, C_GV, C_GR, C_GD = 0, 128, 256, 512, 768
C_NQ, C_NK, C_NV = 896, 1152, 1408
C_DQ, C_DK, C_DV, C_DQR, C_DKR = 1664, 2176, 2688, 3200, 3712
PROJ_COLS = 4224

_NT_DIMS = (((1,), (1,)), ((), ()))


def _nt(a, b):
    return lax.dot_general(a, b, _NT_DIMS, preferred_element_type=F32)


def _params(*sem):
    return pltpu.CompilerParams(dimension_semantics=sem, vmem_limit_bytes=56 * 1024 * 1024)


def _to_f8(x):
    return jnp.clip(x, -F8_MAX, F8_MAX).astype(F8)


def _rms(x):
    return x * lax.rsqrt(jnp.mean(x * x, axis=-1, keepdims=True) + EPS)


def _sigmoid(x):
    return 1.0 / (1.0 + jnp.exp(-x))


def _log_sigmoid(x):
    return jnp.minimum(x, 0.0) - jnp.log1p(jnp.exp(-jnp.abs(x)))


def _mod_kernel(c_ref, w_ref, b_ref, o_ref):
    c = c_ref[...]
    o_ref[0] = jnp.dot(c * _sigmoid(c), w_ref[0], precision=HIGHEST, preferred_element_type=F32) + b_ref[0]


def _modulation(cc, ada_w, ada_b):
    depth = ada_w.shape[0]
    rows = cc.shape[0]
    return pl.pallas_call(
        _mod_kernel,
        grid=(depth, 6),
        in_specs=[pl.BlockSpec((rows, D), lambda l, j: (0, 0)),
                  pl.BlockSpec((1, D, D), lambda l, j: (l, 0, j)),
                  pl.BlockSpec((1, 1, D), lambda l, j: (l, 0, j))],
        out_specs=pl.BlockSpec((1, rows, D), lambda l, j: (l, 0, j)),
        out_shape=jax.ShapeDtypeStruct((depth, rows, 6 * D), F32),
        compiler_params=_params("arbitrary", "arbitrary"),
        name="adaln_mod",
    )(cc, ada_w, ada_b.reshape(depth, 1, 6 * D))


def _proj_kernel(*refs, fuse_res):
    if fuse_res:
        x_ref, pr_ref, modp_ref = refs[:3]
        refs = refs[3:]
    else:
        x_ref = refs[0]
        refs = refs[1:]
    (mod_ref, g_ref, w_ref, gf_ref, gb_ref, gbias_ref, cos_ref, sin_ref) = refs[:8]
    outs = refs[8:]
    if fuse_res:
        xo_ref = outs[0]
        outs = outs[1:]
    (gq_ref, gk_ref, ggf_ref, ggb_ref, gv_ref, gr_ref, nq_ref, nk_ref, nv_ref, dq_ref, dk_ref, dv_ref) = outs

    x = x_ref[0]
    if fuse_res:
        x = x + modp_ref[0][:, 5 * D:6 * D] * pr_ref[0]
        xo_ref[0] = x
    m = mod_ref[0]
    h = _rms(x) * g_ref[...] * (1.0 + m[:, D:2 * D]) + m[:, 0:D]
    p = jnp.dot(h.astype(BF16), w_ref[...], preferred_element_type=F32)

    gq_ref[0] = p[:, C_GQ:C_GQ + 128] * (GLA_DK ** -0.5)
    gk_ref[0] = p[:, C_GK:C_GK + 128]
    gv_ref[0] = p[:, C_GV:C_GV + 256]
    gr_ref[0] = p[:, C_GR:C_GR + 256]
    pd = p[:, C_GD:C_GD + 128].astype(BF16)
    gbias = gbias_ref[...]
    xf = jnp.dot(pd, gf_ref[...], preferred_element_type=F32) + gbias[0:1]
    xb = jnp.dot(pd, gb_ref[...], preferred_element_type=F32) + gbias[1:2]
    ggf_ref[0] = _log_sigmoid(xf) / GLA_TAU
    ggb_ref[0] = _log_sigmoid(xb) / GLA_TAU

    nq_ref[0] = (p[:, C_NQ:C_NQ + 256] * (NAT_DH ** -0.5)).astype(BF16)
    nk_ref[0] = p[:, C_NK:C_NK + 256].astype(BF16)
    nv_ref[0] = p[:, C_NV:C_NV + 256].astype(BF16)

    cos = cos_ref[...]
    sin = sin_ref[...]
    dq = p[:, C_DQ:C_DQ + 512] * cos + p[:, C_DQR:C_DQR + 512] * sin
    dq_ref[0] = (dq * (DIFF_DH ** -0.5)).astype(BF16)
    dk_ref[0] = (p[:, C_DK:C_DK + 512] * cos + p[:, C_DKR:C_DKR + 512] * sin).astype(BF16)
    dv_ref[0] = p[:, C_DV:C_DV + 512].astype(BF16)


def _mod_spec(nb):
    return pl.BlockSpec((1, 1, 6 * D), lambda b, t: (jnp.where(t == 0, nb, b), 0, 0))


def _norm_proj(x, peer, mod_prev, mod, g1, w_all, gf, gb, gbias, cos, sin):
    nb = x.shape[0]
    fuse = peer is not None
    tile = lambda w: pl.BlockSpec((1, TM, w), lambda b, t: (b, t, 0))
    full = lambda a: pl.BlockSpec(a.shape, lambda b, t: (0,) * a.ndim)
    ins, specs = [x], [tile(D)]
    if fuse:
        ins += [peer, mod_prev]
        specs += [tile(D), _mod_spec(nb)]
    ins += [mod, g1, w_all, gf, gb, gbias, cos, sin]
    specs += [_mod_spec(nb), full(g1), full(w_all), full(gf), full(gb), full(gbias),
              pl.BlockSpec((TM, 512), lambda b, t: (t, 0)), pl.BlockSpec((TM, 512), lambda b, t: (t, 0))]
    widths = [(128, F32)] * 4 + [(256, F32)] * 2 + [(256, BF16)] * 3 + [(512, BF16)] * 3
    out_shape = [jax.ShapeDtypeStruct((nb, T, w), dt) for w, dt in widths]
    out_specs = [tile(w) for w, _ in widths]
    if fuse:
        out_shape = [jax.ShapeDtypeStruct((nb, T, D), F32)] + out_shape
        out_specs = [tile(D)] + out_specs
    res = pl.pallas_call(
        functools.partial(_proj_kernel, fuse_res=fuse),
        grid=(nb, NT_TILES),
        in_specs=specs, out_specs=out_specs, out_shape=out_shape,
        compiler_params=_params("arbitrary", "arbitrary"),
        name="norm_proj",
    )(*ins)
    if fuse:
        return res[0], res[1:]
    return x, res


def _gla_kernel(q_ref, k_ref, gf_ref, gb_ref, v_ref, r_ref, ng_ref, o_ref, of_s, ob_s, sf_s, sb_s):
    C = GLA_CHUNK
    nch = T // C
    nctx = CTX // C
    ii = lax.broadcasted_iota(jnp.int32, (C, C), 0)
    jj = lax.broadcasted_iota(jnp.int32, (C, C), 1)
    tri_f = (jj <= ii).astype(F32)
    tri_b = (jj >= ii).astype(F32)
    it = lax.broadcasted_iota(jnp.int32, (C, 4 * C), 0)
    jt = lax.broadcasted_iota(jnp.int32, (C, 4 * C), 1) & (C - 1)
    cm_f = jt <= it
    cm_b = jt >= it
    kmask = ((lax.broadcasted_iota(jnp.int32, (4 * C, 128), 0) >> 6)
             == (lax.broadcasted_iota(jnp.int32, (4 * C, 128), 1) >> 5)).astype(F32)
    vmask = ((lax.broadcasted_iota(jnp.int32, (4 * C, 256), 0) >> 6)
             == (lax.broadcasted_iota(jnp.int32, (4 * C, 256), 1) >> 6)).astype(F32)
    smask = ((lax.broadcasted_iota(jnp.int32, (256, 128), 0) >> 6)
             == (lax.broadcasted_iota(jnp.int32, (256, 128), 1) >> 5)).astype(F32)

    sf_s[...] = jnp.zeros_like(sf_s)
    sb_s[...] = jnp.zeros_like(sb_s)

    def chunk(off, g_ref, s_ref, dst, tri, cm, last, mid):
        rows = pl.ds(off, C)
        q = q_ref[0, rows, :]
        k = k_ref[0, rows, :]
        v = v_ref[0, rows, :]
        g = g_ref[0, rows, :]
        st = s_ref[...]
        b = jnp.dot(tri, g, precision=HIGHEST, preferred_element_type=F32)
        b_last = b[last:last + 1]
        b_mid = b[mid:mid + 1]
        qi = q * jnp.exp(jnp.clip(b - b_mid, -80.0, 80.0))
        ki = k * jnp.exp(jnp.clip(b_mid - b, -80.0, 80.0))
        krows = jnp.concatenate([ki, ki, ki, ki], axis=0) * kmask
        a = jnp.where(cm, _nt(qi.astype(BF16), krows.astype(BF16)), 0.0)
        vblk = jnp.concatenate([v, v, v, v], axis=0) * vmask
        o = jnp.dot(a.astype(BF16), vblk.astype(BF16), preferred_element_type=F32)
        o = o + _nt((q * jnp.exp(b)).astype(BF16), st.astype(BF16))
        dst[rows, :] = o
        kd = k * jnp.exp(b_last - b)
        upd = jnp.dot(v.T.astype(BF16), kd.astype(BF16), preferred_element_type=F32)
        s_ref[...] = st * jnp.exp(b_last) + upd * smask

    def body(t, carry):
        cb = jnp.where(t < nctx, nctx - 1 - t, nch + nctx - 1 - t)
        chunk(pl.multiple_of(t * C, C), gf_ref, sf_s, of_s, tri_f, cm_f, C - 1, C // 2 - 1)
        chunk(pl.multiple_of(cb * C, C), gb_ref, sb_s, ob_s, tri_b, cm_b, 0, C // 2)
        return carry

    lax.fori_loop(0, nch, body, 0)

    hones = ((lax.broadcasted_iota(jnp.int32, (256, 256), 0) >> 6)
             == (lax.broadcasted_iota(jnp.int32, (256, 256), 1) >> 6)).astype(F32) * (1.0 / GLA_DV)
    ng = ng_ref[...]
    for t in range(NT_TILES):
        rows = pl.ds(t * TM, TM)
        o = of_s[rows, :] + ob_s[rows, :]
        ms = jnp.dot(o * o, hones, precision=HIGHEST, preferred_element_type=F32)
        r = r_ref[0, rows, :]
        o_ref[0, rows, :] = (o * lax.rsqrt(ms + EPS) * ng * (r * _sigmoid(r))).astype(BF16)


def _gla(gq, gk, ggf, ggb, gv, gr, ng):
    nb = gq.shape[0]
    seq = lambda w: pl.BlockSpec((1, T, w), lambda b: (b, 0, 0))
    return pl.pallas_call(
        _gla_kernel,
        grid=(nb,),
        in_specs=[seq(128), seq(128), seq(128), seq(128), seq(256), seq(256),
                  pl.BlockSpec((1, 256), lambda b: (0, 0))],
        out_specs=seq(256),
        out_shape=jax.ShapeDtypeStruct((nb, T, 256), BF16),
        scratch_shapes=[pltpu.VMEM((T, 256), F32), pltpu.VMEM((T, 256), F32),
                        pltpu.VMEM((256, 128), F32), pltpu.VMEM((256, 128), F32)],
        compiler_params=_params("arbitrary"),
        name="gla",
    )(gq, gk, ggf, ggb, gv, gr, ng)


def _nat_kernel(q_ref, k_ref, v_ref, bias_ref, o_ref):
    lane = lax.broadcasted_iota(jnp.int32, (1, 256), 1) >> 6
    hm_b = [(lane == h).astype(BF16) for h in range(NAT_HEADS)]
    hm_f = [(lane == h).astype(F32) for h in range(NAT_HEADS)]

    qc = q_ref[0, 0:CTX, :]
    kc = k_ref[0, 0:CTX, :]
    vc = v_ref[0, 0:CTX, :]
    acc = jnp.zeros((CTX, 256), F32)
    for h in range(NAT_HEADS):
        s = _nt(qc * hm_b[h], kc)
        e = jnp.exp(s - jnp.max(s, axis=-1, keepdims=True))
        p = (e / jnp.sum(e, axis=-1, keepdims=True)).astype(BF16)
        acc = acc + jnp.dot(p, vc, preferred_element_type=F32) * hm_f[h]
    o_ref[0, 0:CTX, :] = acc.astype(BF16)

    nwin = NAT_WIN_R * GRID_W

    def row(r, carry):
        rs = jnp.clip(r - NAT_WIN_R // 2, 0, GRID_ROWS - NAT_WIN_R)
        case = rs - r + NAT_WIN_R - 1
        qrows = pl.ds(pl.multiple_of(CTX + r * GRID_W, GRID_W), GRID_W)
        wrows = pl.ds(pl.multiple_of(CTX + rs * GRID_W, GRID_W), nwin)
        q = q_ref[0, qrows, :]
        kw = k_ref[0, wrows, :]
        vw = v_ref[0, wrows, :]
        kcx = k_ref[0, 0:CTX, :]
        vcx = v_ref[0, 0:CTX, :]
        acc = jnp.zeros((GRID_W, 256), F32)
        for h in range(NAT_HEADS):
            qh = q * hm_b[h]
            sw = _nt(qh, kw) + bias_ref[h, case]
            sc = _nt(qh, kcx)
            m = jnp.maximum(jnp.max(sw, axis=-1, keepdims=True), jnp.max(sc, axis=-1, keepdims=True))
            ew = jnp.exp(sw - m)
            ec = jnp.exp(sc - m)
            l = jnp.sum(ew, axis=-1, keepdims=True) + jnp.sum(ec, axis=-1, keepdims=True)
            o = jnp.dot((ew / l).astype(BF16), vw, preferred_element_type=F32)
            o = o + jnp.dot((ec / l).astype(BF16), vcx, preferred_element_type=F32)
            acc = acc + o * hm_f[h]
        o_ref[0, qrows, :] = acc.astype(BF16)
        return carry

    lax.fori_loop(0, GRID_ROWS, row, 0)


def _nat(nq, nk, nv, bias):
    nb = nq.shape[0]
    seq = pl.BlockSpec((1, T, 256), lambda b: (b, 0, 0))
    return pl.pallas_call(
        _nat_kernel,
        grid=(nb,),
        in_specs=[seq, seq, seq, pl.BlockSpec(bias.shape, lambda b: (0, 0, 0, 0))],
        out_specs=seq,
        out_shape=jax.ShapeDtypeStruct((nb, T, 256), BF16),
        compiler_params=_params("arbitrary"),
        name="nat",
    )(nq, nk, nv, bias)


def _nat_bias_table(rpb):
    cidx = np.arange(GRID_W)
    col_start = np.clip(cidx - NAT_WIN_C // 2, 0, GRID_W - NAT_WIN_C)
    col_mask = (cidx[None, :] >= col_start[:, None]) & (cidx[None, :] < col_start[:, None] + NAT_WIN_C)
    dc = np.clip(cidx[None, :] - cidx[:, None] + NAT_WIN_C - 1, 0, 2 * NAT_WIN_C - 2)
    dr = np.arange(NAT_WIN_R)[:, None] + np.arange(NAT_WIN_R)[None, :]
    t = rpb.astype(F32)[:, dr][:, :, :, dc]
    t = jnp.where(col_mask[None, None, None], t, -jnp.inf)
    return t.transpose(0, 1, 3, 2, 4).reshape(NAT_HEADS, NAT_WIN_R, GRID_W, NAT_WIN_R * GRID_W)


def _diff_kernel(lam_ref, q_ref, k_ref, v_ref, g_ref, o_ref, *, lam_init):
    lv = lam_ref[...]
    lam = (jnp.exp(jnp.sum(lv[0:1] * lv[1:2], axis=-1, keepdims=True))
           - jnp.exp(jnp.sum(lv[2:3] * lv[3:4], axis=-1, keepdims=True)) + lam_init)
    q = q_ref[0]
    first = lax.broadcasted_iota(jnp.int32, (1, 2 * DIFF_DH), 1) < DIFF_DH
    q1 = jnp.where(first, q, jnp.zeros_like(q))
    q2 = jnp.where(first, jnp.zeros_like(q), q)

    def attend(k, v):
        s1 = _nt(q1, k)
        s2 = _nt(q2, k)
        e1 = jnp.exp(s1 - jnp.max(s1, axis=-1, keepdims=True))
        e2 = jnp.exp(s2 - jnp.max(s2, axis=-1, keepdims=True))
        w1 = 1.0 / jnp.sum(e1, axis=-1, keepdims=True)
        w2 = lam / jnp.sum(e2, axis=-1, keepdims=True)
        p = (e1 * w1 - e2 * w2).astype(BF16)
        o = jnp.dot(p, v, preferred_element_type=F32)
        o_ref[0] = (_rms(o) * g_ref[...] * (1.0 - lam_init)).astype(BF16)

    @pl.when(pl.program_id(2) == 0)
    def _():
        attend(k_ref[0, 0:CTX, :], v_ref[0, 0:CTX, :])

    @pl.when(pl.program_id(2) > 0)
    def _():
        attend(k_ref[0], v_ref[0])


def _diff(dq, dk, dv, lam_vec, dg, lam_init):
    nb = dq.shape[0]
    w = 2 * DIFF_DH
    return pl.pallas_call(
        functools.partial(_diff_kernel, lam_init=lam_init),
        grid=(nb, DIFF_HEADS, NT_TILES),
        in_specs=[pl.BlockSpec((4, DIFF_DH), lambda b, h, t: (0, 0)),
                  pl.BlockSpec((1, TM, w), lambda b, h, t: (b, t, h)),
                  pl.BlockSpec((1, T, w), lambda b, h, t: (b, 0, h)),
                  pl.BlockSpec((1, T, w), lambda b, h, t: (b, 0, h)),
                  pl.BlockSpec((1, w), lambda b, h, t: (0, h))],
        out_specs=pl.BlockSpec((1, TM, w), lambda b, h, t: (b, t, h)),
        out_shape=jax.ShapeDtypeStruct((nb, T, DIFF_HEADS * w), BF16),
        compiler_params=_params("arbitrary", "arbitrary", "arbitrary"),
        name="diff_attn",
    )(lam_vec, dq, dk, dv, dg)


def _out_kernel(x_ref, yg_ref, yn_ref, yd_ref, mod_ref, w_ref, g_ref, xo_ref, ht_ref):
    attn = jnp.dot(yg_ref[0], w_ref[0:256, :], preferred_element_type=F32)
    attn = attn + jnp.dot(yn_ref[0], w_ref[256:512, :], preferred_element_type=F32)
    attn = attn + jnp.dot(yd_ref[0], w_ref[512:1024, :], preferred_element_type=F32)
    m = mod_ref[0]
    x = x_ref[0] + m[:, 2 * D:3 * D] * attn
    xo_ref[0] = x
    h2 = _rms(x) * g_ref[...] * (1.0 + m[:, 4 * D:5 * D]) + m[:, 3 * D:4 * D]
    ht_ref[...] = h2.T.astype(BF16)


def _out_proj(x, yg, yn, yd, mod, w_out, g2):
    nb = x.shape[0]
    tile = lambda w: pl.BlockSpec((1, TM, w), lambda b, t: (b, t, 0))
    return pl.pallas_call(
        _out_kernel,
        grid=(nb, NT_TILES),
        in_specs=[tile(D), tile(256), tile(256), tile(512), _mod_spec(nb),
                  pl.BlockSpec((D, D), lambda b, t: (0, 0)), pl.BlockSpec((1, D), lambda b, t: (0, 0))],
        out_specs=[tile(D), pl.BlockSpec((D, TM), lambda b, t: (0, b * NT_TILES + t))],
        out_shape=[jax.ShapeDtypeStruct((nb, T, D), F32), jax.ShapeDtypeStruct((D, nb * T), BF16)],
        compiler_params=_params("arbitrary", "arbitrary"),
        name="out_proj",
    )(x, yg, yn, yd, mod, w_out, g2)


def _oddeven_pairs(n):
    pairs = []

    def merge(lo, hi, r):
        step = r * 2
        if step < hi - lo:
            merge(lo, hi, step)
            merge(lo + r, hi, step)
            pairs.extend((i, i + r) for i in range(lo + r, hi - r, step))
        else:
            pairs.append((lo, lo + r))

    def sort(lo, hi):
        if hi - lo >= 1:
            mid = lo + (hi - lo) // 2
            sort(lo, mid)
            sort(mid + 1, hi)
            merge(lo, hi, 1)

    sort(0, n - 1)
    return pairs


def _bitonic_pairs(n):
    pairs = []
    d = n // 2
    while d >= 1:
        pairs.extend((i, i + d) for i in range(n) if (i // d) % 2 == 0)
        d //= 2
    return pairs


_SORT16 = _oddeven_pairs(16)
_BITONIC16 = _bitonic_pairs(16)


def _compare_exchange(xs, pairs):
    for i, j in pairs:
        a, b = xs[i], xs[j]
        if b is None:
            continue
        if a is None:
            xs[i], xs[j] = b, None
        else:
            xs[i], xs[j] = jnp.maximum(a, b), jnp.minimum(a, b)


def _merge_top16(xs, ys):
    zs = []
    for k in range(16):
        a, b = xs[k], ys[15 - k]
        zs.append(a if b is None else (b if a is None else jnp.maximum(a, b)))
    _compare_exchange(zs, _BITONIC16)
    return zs


def _top16_of_128(x3):
    xs = [x3[r] for r in range(16)]
    _compare_exchange(xs, _SORT16)
    for shift in (4, 2, 1):
        xs = _merge_top16(xs, [pltpu.roll(x, shift, 0) for x in xs])
    return xs


def _peer_select(s1, s2):
    n = s1.shape[-1]
    a3 = s1.reshape(16, 8, n)
    b3 = s2.reshape(16, 8, n)
    t1 = _top16_of_128(a3)
    t2 = _top16_of_128(b3)
    cand = [[t1[a] + t2[b] for b in range(PEER_TOPK // (a + 1))] for a in range(PEER_TOPK)]
    pad = lambda xs: xs + [None] * (16 - len(xs))
    top = cand[0]
    for a in range(1, 8):
        top = _merge_top16(top, pad(cand[a]))
    top = _merge_top16(top, pad([cand[a][0] for a in range(8, 16)]))
    tau = top[15]
    cmax = cand[0][0]
    inf = jnp.full_like(tau, jnp.inf)
    z = jnp.zeros_like(tau)
    thr = []
    for a in range(PEER_TOPK):
        th = inf
        for b, cv in enumerate(cand[a]):
            sel = cv >= tau
            th = jnp.where(sel, jnp.minimum(th, t2[b]), th)
            z = z + jnp.where(sel, jnp.exp(cv - cmax), 0.0)
        thr.append(th)
    theta = jnp.full_like(a3, jnp.inf)
    for a in range(PEER_TOPK):
        theta = jnp.where(a3 == t1[a][None], thr[a][None], theta)
    c = jnp.exp(a3 - t1[0][None]) / z[None]
    e2 = jnp.exp(b3 - t2[0][None])
    return theta, c, e2.reshape(PEER_NKEYS, n)


def _peer_kernel(ht_ref, wq_ref, sk_ref, u_ref, vt_ref, o_ref, th_s, c_s, s2_s, e2_s, h8_s, a_s, w_s, acc_s):
    e = pl.program_id(1)
    tb = ht_ref.shape[1]

    @pl.when(e == 0)
    def _():
        qt = jnp.dot(wq_ref[...], ht_ref[...], preferred_element_type=F32)
        sk = sk_ref[...]
        for h in range(PEER_HEADS):
            s1 = jnp.dot(sk[0], qt[h * 128:h * 128 + 64], precision=HIGHEST, preferred_element_type=F32)
            s2 = jnp.dot(sk[1], qt[h * 128 + 64:h * 128 + 128], precision=HIGHEST, preferred_element_type=F32)
            theta, c, e2 = _peer_select(s1, s2)
            th_s[h] = theta
            c_s[h] = c * PEER_W_SCALE
            for lc in range(tb // 128):
                s2_s[h, lc] = s2[:, lc * 128:(lc + 1) * 128]
                e2_s[h, lc] = e2[:, lc * 128:(lc + 1) * 128]
        acc_s[...] = jnp.zeros_like(acc_s)
        h8_s[...] = jnp.clip(ht_ref[...], -F8_MAX, F8_MAX).astype(F8)

    a = jnp.dot(u_ref[...], h8_s[...], preferred_element_type=F32) * (1.0 / PEER_U_SCALE)
    for lc in range(tb // 128):
        a_s[lc] = a[:, lc * 128:(lc + 1) * 128]
    for lc in range(tb // 128):
        lanes = slice(lc * 128, (lc + 1) * 128)
        for i in range(PEER_ROWS):
            g = jnp.zeros((PEER_NKEYS, 128), F32)
            for h in range(PEER_HEADS):
                th = th_s[h, e, i:i + 1, lanes]
                cw = c_s[h, e, i:i + 1, lanes]
                g = g + jnp.where(s2_s[h, lc] >= th, e2_s[h, lc] * cw, 0.0)
            rows = slice(i * PEER_NKEYS, (i + 1) * PEER_NKEYS)
            w = g * jax.nn.gelu(a_s[lc, rows, :], approximate=True)
            w_s[rows, lanes] = jnp.minimum(w, F8_MAX).astype(F8)
    acc_s[...] += jnp.dot(vt_ref[...], w_s[...], preferred_element_type=F32)

    @pl.when(e == pl.num_programs(1) - 1)
    def _():
        o_ref[...] = acc_s[...].T * (1.0 / PEER_W_SCALE)


def _peer(ht, wq_t, sk, u, v_t):
    n = ht.shape[1]
    tb = PEER_TB
    ne = PEER_EXPERTS // PEER_EB
    sel = pltpu.VMEM((PEER_HEADS, tb // 128, PEER_NKEYS, 128), F32)
    sel8 = pltpu.VMEM((PEER_HEADS, PEER_NKEYS // 8, 8, tb), F32)
    return pl.pallas_call(
        _peer_kernel,
        grid=(n // tb, ne),
        in_specs=[pl.BlockSpec((D, tb), lambda t, e: (0, t)),
                  pl.BlockSpec((D, D), lambda t, e: (0, 0)),
                  pl.BlockSpec((2, PEER_NKEYS, PEER_HALF), lambda t, e: (0, 0, 0)),
                  pl.BlockSpec((PEER_EB, D), lambda t, e: (e, 0)),
                  pl.BlockSpec((D, PEER_EB), lambda t, e: (0, e))],
        out_specs=pl.BlockSpec((tb, D), lambda t, e: (t, 0)),
        out_shape=jax.ShapeDtypeStruct((n, D), F32),
        scratch_shapes=[sel8, sel8, sel, sel, pltpu.VMEM((D, tb), F8),
                        pltpu.VMEM((tb // 128, PEER_EB, 128), F32), pltpu.VMEM((PEER_EB, tb), F8),
                        pltpu.VMEM((D, tb), F32)],
        compiler_params=_params("arbitrary", "arbitrary"),
        name="peer",
    )(ht, wq_t, sk, u, v_t)


def _final_kernel(x_ref, pr_ref, mod_ref, g_ref, o_ref):
    x = x_ref[0] + mod_ref[0][:, 5 * D:6 * D] * pr_ref[0]
    o_ref[0] = _rms(x) * g_ref[...]


def _final(x, peer, mod, g):
    nb = x.shape[0]
    lat = pl.BlockSpec((1, TM, D), lambda b, t: (b, t + 1, 0))
    return pl.pallas_call(
        _final_kernel,
        grid=(nb, SEQ // TM),
        in_specs=[lat, lat, pl.BlockSpec((1, 1, 6 * D), lambda b, t: (b, 0, 0)),
                  pl.BlockSpec((1, D), lambda b, t: (0, 0))],
        out_specs=pl.BlockSpec((1, TM, D), lambda b, t: (b, t, 0)),
        out_shape=jax.ShapeDtypeStruct((nb, SEQ, D), F32),
        compiler_params=_params("arbitrary", "arbitrary"),
        name="final_norm",
    )(x, peer, mod, g)


def _rope_tables():
    col = np.arange(DIFF_HEADS * 2 * DIFF_DH)
    d = col % DIFF_DH
    half = DIFF_DH // 4
    inv = ROPE_BASE ** (-jnp.arange(half, dtype=F32) / half)
    inv_c = inv[d % half]
    by_col = jnp.asarray((d // (DIFF_DH // 2)) == 1)
    t = jnp.arange(SEQ)
    pos = jnp.where(by_col[None, :], (t % GRID_W)[:, None], (t // GRID_W)[:, None]).astype(F32)
    ang = pos * inv_c[None, :]
    cos = jnp.concatenate([jnp.ones((CTX, col.size), F32), jnp.cos(ang)], axis=0)
    sin = jnp.concatenate([jnp.zeros((CTX, col.size), F32), jnp.sin(ang)], axis=0)
    return cos, sin


def _rot_columns(w):
    col = np.arange(w.shape[1])
    lo = (col % (DIFF_DH // 2)) < DIFF_DH // 4
    src = np.where(lo, col + DIFF_DH // 4, col - DIFF_DH // 4)
    sign = np.where(lo, -1.0, 1.0).astype(np.float32)
    return w[:, src] * sign[None, :]


def _regroup_w_in(w):
    sizes = (128, 128, 256, 256, 16, 16, 256, 256, 256, 512, 512, 512)
    pts = np.cumsum((0,) + sizes)
    p = [w[:, pts[i]:pts[i + 1]] for i in range(12)]
    downs = jnp.concatenate([p[4], p[5], jnp.zeros((D, 128 - 2 * GLA_RANK), w.dtype)], axis=1)
    cols = [p[0], p[1], p[2], p[3], downs, p[6], p[7], p[8], p[9], p[10], p[11], _rot_columns(p[9]), _rot_columns(p[10])]
    return jnp.concatenate(cols, axis=1).astype(BF16)


def _gate_up_padded(gate_up):
    z = jnp.zeros((128, 128), F32)
    gf = z.at[0:GLA_RANK].set(gate_up[0]).astype(BF16)
    gb = z.at[GLA_RANK:2 * GLA_RANK].set(gate_up[1]).astype(BF16)
    return gf, gb


def kernel(x, c, ctx, c_ctx, ada_w, ada_b, norm1_g, norm2_g, w_in, gla_gate_up, gla_gate_b, gla_norm_g, nat_rpb,
           diff_lambda, diff_norm_g, w_out, peer_wq, peer_subkeys, peer_u, peer_v, final_g):
    nb = x.shape[0]
    depth = ada_w.shape[0]
    assert x.shape[1:] == (SEQ, D) and ctx.shape[1:] == (CTX, D) and (nb * T) % PEER_TB == 0

    rows = -(-(nb + 1) // 8) * 8
    cc = jnp.concatenate([c, c_ctx[None], jnp.zeros((rows - nb - 1, D), F32)], axis=0)
    mod_all = _modulation(cc, ada_w, ada_b)
    cos, sin = _rope_tables()

    xa = jnp.concatenate([ctx, x], axis=1)
    peer = None
    mod_prev = None
    for l in range(depth):
        lam_init = 0.8 - 0.6 * math.exp(-0.3 * l)
        mod = mod_all[l].reshape(rows, 1, 6 * D)
        gf, gb = _gate_up_padded(gla_gate_up[l])
        xa, (gq, gk, ggf, ggb, gv, gr, nq, nk, nv, dq, dk, dv) = _norm_proj(
            xa, peer, mod_prev, mod, norm1_g[l][None], _regroup_w_in(w_in[l]), gf, gb, gla_gate_b[l], cos, sin)
        yg = _gla(gq, gk, ggf, ggb, gv, gr, gla_norm_g[l][None])
        yn = _nat(nq, nk, nv, _nat_bias_table(nat_rpb[l]))
        yd = _diff(dq, dk, dv, diff_lambda[l], diff_norm_g[l][None], lam_init)
        xa, ht = _out_proj(xa, yg, yn, yd, mod, w_out[l].astype(BF16), norm2_g[l][None])
        peer = _peer(ht, peer_wq[l].T.astype(BF16), peer_subkeys[l],
                     _to_f8(peer_u[l] * PEER_U_SCALE), _to_f8(peer_v[l].T)).reshape(nb, T, D)
        mod_prev = mod
    return _final(xa, peer, mod_prev, final_g[None])
```

```python
import functools
import math

import numpy as np
import jax
import jax.numpy as jnp
from jax import lax
from jax.experimental import pallas as pl
from jax.experimental.pallas import tpu as pltpu

F32 = jnp.float32
BF16 = jnp.bfloat16
F8 = jnp.float8_e4m3fn
F8_MAX = 448.0
HIGHEST = lax.Precision.HIGHEST

D = 1024
SEQ = 2048
CTX = 256
T = SEQ + CTX
GRID_W = 64
GRID_ROWS = SEQ // GRID_W
EPS = 1e-6
TM = 256
NT_TILES = T // TM

GLA_HEADS, GLA_DK, GLA_DV, GLA_RANK, GLA_TAU, GLA_CHUNK = 4, 32, 64, 16, 16.0, 64
NAT_HEADS, NAT_DH, NAT_WIN_R, NAT_WIN_C = 4, 64, 8, 16
DIFF_HEADS, DIFF_DH = 4, 64
ROPE_BASE = 10000.0
PEER_HEADS, PEER_NKEYS, PEER_HALF, PEER_TOPK = 8, 128, 64, 16
PEER_EXPERTS = PEER_NKEYS * PEER_NKEYS
PEER_TB = 512
PEER_ROWS = 8
PEER_EB = PEER_ROWS * PEER_NKEYS
PEER_U_SCALE = 32.0
PEER_W_SCALE = 4.0

C_GQ, C_GK, C_GV, C_GR, C_GD = 0, 128, 256, 512, 768
C_NQ, C_NK, C_NV = 896, 1152, 1408
C_DQ, C_DK, C_DV, C_DQR, C_DKR = 1664, 2176, 2688, 3200, 3712
PROJ_COLS = 4224

_NT_DIMS = (((1,), (1,)), ((), ()))


def _nt(a, b):
    return lax.dot_general(a, b, _NT_DIMS, preferred_element_type=F32)


def _params(*sem):
    return pltpu.CompilerParams(dimension_semantics=sem, vmem_limit_bytes=56 * 1024 * 1024)


def _to_f8(x):
    return jnp.clip(x, -F8_MAX, F8_MAX).astype(F8)


def _rms(x):
    return x * lax.rsqrt(jnp.mean(x * x, axis=-1, keepdims=True) + EPS)


def _sigmoid(x):
    return 1.0 / (1.0 + jnp.exp(-x))


def _log_sigmoid(x):
    return jnp.minimum(x, 0.0) - jnp.log1p(jnp.exp(-jnp.abs(x)))


def _mod_kernel(c_ref, w_ref, b_ref, o_ref):
    c = c_ref[...]
    o_ref[0] = jnp.dot(c * _sigmoid(c), w_ref[0], precision=HIGHEST, preferred_element_type=F32) + b_ref[0]


def _modulation(cc, ada_w, ada_b):
    depth = ada_w.shape[0]
    rows = cc.shape[0]
    return pl.pallas_call(
        _mod_kernel,
        grid=(depth, 6),
        in_specs=[pl.BlockSpec((rows, D), lambda l, j: (0, 0)),
                  pl.BlockSpec((1, D, D), lambda l, j: (l, 0, j)),
                  pl.BlockSpec((1, 1, D), lambda l, j: (l, 0, j))],
        out_specs=pl.BlockSpec((1, rows, D), lambda l, j: (l, 0, j)),
        out_shape=jax.ShapeDtypeStruct((depth, rows, 6 * D), F32),
        compiler_params=_params("arbitrary", "arbitrary"),
        name="adaln_mod",
    )(cc, ada_w, ada_b.reshape(depth, 1, 6 * D))


def _proj_kernel(*refs, fuse_res):
    if fuse_res:
        x_ref, pr_ref, modp_ref = refs[:3]
        refs = refs[3:]
    else:
        x_ref = refs[0]
        refs = refs[1:]
    (mod_ref, g_ref, w_ref, gf_ref, gb_ref, gbias_ref, cos_ref, sin_ref) = refs[:8]
    outs = refs[8:]
    if fuse_res:
        xo_ref = outs[0]
        outs = outs[1:]
    (gq_ref, gk_ref, ggf_ref, ggb_ref, gv_ref, gr_ref, nq_ref, nk_ref, nv_ref, dq_ref, dk_ref, dv_ref) = outs

    x = x_ref[0]
    if fuse_res:
        x = x + modp_ref[0][:, 5 * D:6 * D] * pr_ref[0]
        xo_ref[0] = x
    m = mod_ref[0]
    h = _rms(x) * g_ref[...] * (1.0 + m[:, D:2 * D]) + m[:, 0:D]
    p = jnp.dot(h.astype(BF16), w_ref[...], preferred_element_type=F32)

    gq_ref[0] = p[:, C_GQ:C_GQ + 128] * (GLA_DK ** -0.5)
    gk_ref[0] = p[:, C_GK:C_GK + 128]
    gv_ref[0] = p[:, C_GV:C_GV + 256]
    gr_ref[0] = p[:, C_GR:C_GR + 256]
    pd = p[:, C_GD:C_GD + 128].astype(BF16)
    gbias = gbias_ref[...]
    xf = jnp.dot(pd, gf_ref[...], preferred_element_type=F32) + gbias[0:1]
    xb = jnp.dot(pd, gb_ref[...], preferred_element_type=F32) + gbias[1:2]
    ggf_ref[0] = _log_sigmoid(xf) / GLA_TAU
    ggb_ref[0] = _log_sigmoid(xb) / GLA_TAU

    nq_ref[0] = (p[:, C_NQ:C_NQ + 256] * (NAT_DH ** -0.5)).astype(BF16)
    nk_ref[0] = p[:, C_NK:C_NK + 256].astype(BF16)
    nv_ref[0] = p[:, C_NV:C_NV + 256].astype(BF16)

    cos = cos_ref[...]
    sin = sin_ref[...]
    dq = p[:, C_DQ:C_DQ + 512] * cos + p[:, C_DQR:C_DQR + 512] * sin
    dq_ref[0] = (dq * (DIFF_DH ** -0.5)).astype(BF16)
    dk_ref[0] = (p[:, C_DK:C_DK + 512] * cos + p[:, C_DKR:C_DKR + 512] * sin).astype(BF16)
    dv_ref[0] = p[:, C_DV:C_DV + 512].astype(BF16)


def _mod_spec(nb):
    return pl.BlockSpec((1, 1, 6 * D), lambda b, t: (jnp.where(t == 0, nb, b), 0, 0))


def _norm_proj(x, peer, mod_prev, mod, g1, w_all, gf, gb, gbias, cos, sin):
    nb = x.shape[0]
    fuse = peer is not None
    tile = lambda w: pl.BlockSpec((1, TM, w), lambda b, t: (b, t, 0))
    full = lambda a: pl.BlockSpec(a.shape, lambda b, t: (0,) * a.ndim)
    ins, specs = [x], [tile(D)]
    if fuse:
        ins += [peer, mod_prev]
        specs += [tile(D), _mod_spec(nb)]
    ins += [mod, g1, w_all, gf, gb, gbias, cos, sin]
    specs += [_mod_spec(nb), full(g1), full(w_all), full(gf), full(gb), full(gbias),
              pl.BlockSpec((TM, 512), lambda b, t: (t, 0)), pl.BlockSpec((TM, 512), lambda b, t: (t, 0))]
    widths = [(128, F32)] * 4 + [(256, F32)] * 2 + [(256, BF16)] * 3 + [(512, BF16)] * 3
    out_shape = [jax.ShapeDtypeStruct((nb, T, w), dt) for w, dt in widths]
    out_specs = [tile(w) for w, _ in widths]
    if fuse:
        out_shape = [jax.ShapeDtypeStruct((nb, T, D), F32)] + out_shape
        out_specs = [tile(D)] + out_specs
    res = pl.pallas_call(
        functools.partial(_proj_kernel, fuse_res=fuse),
        grid=(nb, NT_TILES),
        in_specs=specs, out_specs=out_specs, out_shape=out_shape,
        compiler_params=_params("arbitrary", "arbitrary"),
        name="norm_proj",
    )(*ins)
    if fuse:
        return res[0], res[1:]
    return x, res


def _gla_kernel(q_ref, k_ref, gf_ref, gb_ref, v_ref, r_ref, ng_ref, o_ref, of_s, ob_s, sf_s, sb_s):
    C = GLA_CHUNK
    nch = T // C
    nctx = CTX // C
    ii = lax.broadcasted_iota(jnp.int32, (C, C), 0)
    jj = lax.broadcasted_iota(jnp.int32, (C, C), 1)
    tri_f = (jj <= ii).astype(F32)
    tri_b = (jj >= ii).astype(F32)
    it = lax.broadcasted_iota(jnp.int32, (C, 4 * C), 0)
    jt = lax.broadcasted_iota(jnp.int32, (C, 4 * C), 1) & (C - 1)
    cm_f = jt <= it
    cm_b = jt >= it
    kmask = ((lax.broadcasted_iota(jnp.int32, (4 * C, 128), 0) >> 6)
             == (lax.broadcasted_iota(jnp.int32, (4 * C, 128), 1) >> 5)).astype(F32)
    vmask = ((lax.broadcasted_iota(jnp.int32, (4 * C, 256), 0) >> 6)
             == (lax.broadcasted_iota(jnp.int32, (4 * C, 256), 1) >> 6)).astype(F32)
    smask = ((lax.broadcasted_iota(jnp.int32, (256, 128), 0) >> 6)
             == (lax.broadcasted_iota(jnp.int32, (256, 128), 1) >> 5)).astype(F32)

    sf_s[...] = jnp.zeros_like(sf_s)
    sb_s[...] = jnp.zeros_like(sb_s)

    def chunk(off, g_ref, s_ref, dst, tri, cm, last, mid):
        rows = pl.ds(off, C)
        q = q_ref[0, rows, :]
        k = k_ref[0, rows, :]
        v = v_ref[0, rows, :]
        g = g_ref[0, rows, :]
        st = s_ref[...]
        b = jnp.dot(tri, g, precision=HIGHEST, preferred_element_type=F32)
        b_last = b[last:last + 1]
        b_mid = b[mid:mid + 1]
        qi = q * jnp.exp(jnp.clip(b - b_mid, -80.0, 80.0))
        ki = k * jnp.exp(jnp.clip(b_mid - b, -80.0, 80.0))
        krows = jnp.concatenate([ki, ki, ki, ki], axis=0) * kmask
        a = jnp.where(cm, _nt(qi.astype(BF16), krows.astype(BF16)), 0.0)
        vblk = jnp.concatenate([v, v, v, v], axis=0) * vmask
        o = jnp.dot(a.astype(BF16), vblk.astype(BF16), preferred_element_type=F32)
        o = o + _nt((q * jnp.exp(b)).astype(BF16), st.astype(BF16))
        dst[rows, :] = o
        kd = k * jnp.exp(b_last - b)
        upd = jnp.dot(v.T.astype(BF16), kd.astype(BF16), preferred_element_type=F32)
        s_ref[...] = st * jnp.exp(b_last) + upd * smask

    def body(t, carry):
        cb = jnp.where(t < nctx, nctx - 1 - t, nch + nctx - 1 - t)
        chunk(pl.multiple_of(t * C, C), gf_ref, sf_s, of_s, tri_f, cm_f, C - 1, C // 2 - 1)
        chunk(pl.multiple_of(cb * C, C), gb_ref, sb_s, ob_s, tri_b, cm_b, 0, C // 2)
        return carry

    lax.fori_loop(0, nch, body, 0)

    hones = ((lax.broadcasted_iota(jnp.int32, (256, 256), 0) >> 6)
             == (lax.broadcasted_iota(jnp.int32, (256, 256), 1) >> 6)).astype(F32) * (1.0 / GLA_DV)
    ng = ng_ref[...]
    for t in range(NT_TILES):
        rows = pl.ds(t * TM, TM)
        o = of_s[rows, :] + ob_s[rows, :]
        ms = jnp.dot(o * o, hones, precision=HIGHEST, preferred_element_type=F32)
        r = r_ref[0, rows, :]
        o_ref[0, rows, :] = (o * lax.rsqrt(ms + EPS) * ng * (r * _sigmoid(r))).astype(BF16)


def _gla(gq, gk, ggf, ggb, gv, gr, ng):
    nb = gq.shape[0]
    seq = lambda w: pl.BlockSpec((1, T, w), lambda b: (b, 0, 0))
    return pl.pallas_call(
        _gla_kernel,
        grid=(nb,),
        in_specs=[seq(128), seq(128), seq(128), seq(128), seq(256), seq(256),
                  pl.BlockSpec((1, 256), lambda b: (0, 0))],
        out_specs=seq(256),
        out_shape=jax.ShapeDtypeStruct((nb, T, 256), BF16),
        scratch_shapes=[pltpu.VMEM((T, 256), F32), pltpu.VMEM((T, 256), F32),
                        pltpu.VMEM((256, 128), F32), pltpu.VMEM((256, 128), F32)],
        compiler_params=_params("arbitrary"),
        name="gla",
    )(gq, gk, ggf, ggb, gv, gr, ng)


def _stacked_heads(q, n):
    rows = lax.broadcasted_iota(jnp.int32, (NAT_HEADS * n, 256), 0)
    lanes = lax.broadcasted_iota(jnp.int32, (NAT_HEADS * n, 256), 1) >> 6
    head = jnp.zeros_like(rows)
    for h in range(1, NAT_HEADS):
        head = head + (rows >= h * n).astype(jnp.int32)
    own = head == lanes
    q4 = jnp.concatenate([q] * NAT_HEADS, axis=0)
    return jnp.where(own, q4, jnp.zeros_like(q4)), own


def _unstack_heads(pv, own, n):
    pv = jnp.where(own, pv, 0.0)
    out = pv[0:n]
    for h in range(1, NAT_HEADS):
        out = out + pv[h * n:(h + 1) * n]
    return out


def _nat_kernel(q_ref, k_ref, v_ref, bias_ref, o_ref):
    kc = k_ref[0, 0:CTX, :]
    vc = v_ref[0, 0:CTX, :]
    q4, own = _stacked_heads(q_ref[0, 0:CTX, :], CTX)
    s = _nt(q4, kc)
    e = jnp.exp(s - jnp.max(s, axis=-1, keepdims=True))
    p = (e * (1.0 / jnp.sum(e, axis=-1, keepdims=True))).astype(BF16)
    o_ref[0, 0:CTX, :] = _unstack_heads(jnp.dot(p, vc, preferred_element_type=F32), own, CTX).astype(BF16)

    nwin = NAT_WIN_R * GRID_W

    def row(r, carry):
        rs = jnp.clip(r - NAT_WIN_R // 2, 0, GRID_ROWS - NAT_WIN_R)
        case = rs - r + NAT_WIN_R - 1
        qrows = pl.ds(pl.multiple_of(CTX + r * GRID_W, GRID_W), GRID_W)
        wrows = pl.ds(pl.multiple_of(CTX + rs * GRID_W, GRID_W), nwin)
        q4, own = _stacked_heads(q_ref[0, qrows, :], GRID_W)
        sw = _nt(q4, k_ref[0, wrows, :]) + bias_ref[case]
        sc = _nt(q4, k_ref[0, 0:CTX, :])
        m = jnp.maximum(jnp.max(sw, axis=-1, keepdims=True), jnp.max(sc, axis=-1, keepdims=True))
        ew = jnp.exp(sw - m)
        ec = jnp.exp(sc - m)
        inv = 1.0 / (jnp.sum(ew, axis=-1, keepdims=True) + jnp.sum(ec, axis=-1, keepdims=True))
        pv = jnp.dot((ew * inv).astype(BF16), v_ref[0, wrows, :], preferred_element_type=F32)
        pv = pv + jnp.dot((ec * inv).astype(BF16), v_ref[0, 0:CTX, :], preferred_element_type=F32)
        o_ref[0, qrows, :] = _unstack_heads(pv, own, GRID_W).astype(BF16)
        return carry

    lax.fori_loop(0, GRID_ROWS, row, 0)


def _nat(nq, nk, nv, bias):
    nb = nq.shape[0]
    seq = pl.BlockSpec((1, T, 256), lambda b: (b, 0, 0))
    return pl.pallas_call(
        _nat_kernel,
        grid=(nb,),
        in_specs=[seq, seq, seq, pl.BlockSpec(bias.shape, lambda b: (0, 0, 0))],
        out_specs=seq,
        out_shape=jax.ShapeDtypeStruct((nb, T, 256), BF16),
        compiler_params=_params("arbitrary"),
        name="nat",
    )(nq, nk, nv, bias)


def _nat_bias_table(rpb):
    cidx = np.arange(GRID_W)
    col_start = np.clip(cidx - NAT_WIN_C // 2, 0, GRID_W - NAT_WIN_C)
    col_mask = (cidx[None, :] >= col_start[:, None]) & (cidx[None, :] < col_start[:, None] + NAT_WIN_C)
    dc = np.clip(cidx[None, :] - cidx[:, None] + NAT_WIN_C - 1, 0, 2 * NAT_WIN_C - 2)
    dr = np.arange(NAT_WIN_R)[:, None] + np.arange(NAT_WIN_R)[None, :]
    t = rpb.astype(F32)[:, dr][:, :, :, dc]
    t = jnp.where(col_mask[None, None, None], t, -jnp.inf)
    return t.transpose(1, 0, 3, 2, 4).reshape(NAT_WIN_R, NAT_HEADS * GRID_W, NAT_WIN_R * GRID_W)


def _diff_kernel(lam_ref, q_ref, k_ref, v_ref, g_ref, o_ref, *, lam_init):
    lv = lam_ref[...]
    lam = (jnp.exp(jnp.sum(lv[0:1] * lv[1:2], axis=-1, keepdims=True))
           - jnp.exp(jnp.sum(lv[2:3] * lv[3:4], axis=-1, keepdims=True)) + lam_init)
    q = q_ref[0]
    first = lax.broadcasted_iota(jnp.int32, (1, 2 * DIFF_DH), 1) < DIFF_DH
    q1 = jnp.where(first, q, jnp.zeros_like(q))
    q2 = jnp.where(first, jnp.zeros_like(q), q)

    def attend(k, v):
        s1 = _nt(q1, k)
        s2 = _nt(q2, k)
        e1 = jnp.exp(s1 - jnp.max(s1, axis=-1, keepdims=True))
        e2 = jnp.exp(s2 - jnp.max(s2, axis=-1, keepdims=True))
        w1 = 1.0 / jnp.sum(e1, axis=-1, keepdims=True)
        w2 = lam / jnp.sum(e2, axis=-1, keepdims=True)
        p = (e1 * w1 - e2 * w2).astype(BF16)
        o = jnp.dot(p, v, preferred_element_type=F32)
        o_ref[0] = (_rms(o) * g_ref[...] * (1.0 - lam_init)).astype(BF16)

    @pl.when(pl.program_id(2) == 0)
    def _():
        attend(k_ref[0, 0:CTX, :], v_ref[0, 0:CTX, :])

    @pl.when(pl.program_id(2) > 0)
    def _():
        attend(k_ref[0], v_ref[0])


def _diff(dq, dk, dv, lam_vec, dg, lam_init):
    nb = dq.shape[0]
    w = 2 * DIFF_DH
    return pl.pallas_call(
        functools.partial(_diff_kernel, lam_init=lam_init),
        grid=(nb, DIFF_HEADS, NT_TILES),
        in_specs=[pl.BlockSpec((4, DIFF_DH), lambda b, h, t: (0, 0)),
                  pl.BlockSpec((1, TM, w), lambda b, h, t: (b, t, h)),
                  pl.BlockSpec((1, T, w), lambda b, h, t: (b, 0, h)),
                  pl.BlockSpec((1, T, w), lambda b, h, t: (b, 0, h)),
                  pl.BlockSpec((1, w), lambda b, h, t: (0, h))],
        out_specs=pl.BlockSpec((1, TM, w), lambda b, h, t: (b, t, h)),
        out_shape=jax.ShapeDtypeStruct((nb, T, DIFF_HEADS * w), BF16),
        compiler_params=_params("arbitrary", "arbitrary", "arbitrary"),
        name="diff_attn",
    )(lam_vec, dq, dk, dv, dg)


def _out_kernel(x_ref, yg_ref, yn_ref, yd_ref, mod_ref, w_ref, g_ref, xo_ref, ht_ref):
    attn = jnp.dot(yg_ref[0], w_ref[0:256, :], preferred_element_type=F32)
    attn = attn + jnp.dot(yn_ref[0], w_ref[256:512, :], preferred_element_type=F32)
    attn = attn + jnp.dot(yd_ref[0], w_ref[512:1024, :], preferred_element_type=F32)
    m = mod_ref[0]
    x = x_ref[0] + m[:, 2 * D:3 * D] * attn
    xo_ref[0] = x
    h2 = _rms(x) * g_ref[...] * (1.0 + m[:, 4 * D:5 * D]) + m[:, 3 * D:4 * D]
    ht_ref[...] = h2.T.astype(BF16)


def _out_proj(x, yg, yn, yd, mod, w_out, g2):
    nb = x.shape[0]
    tile = lambda w: pl.BlockSpec((1, TM, w), lambda b, t: (b, t, 0))
    return pl.pallas_call(
        _out_kernel,
        grid=(nb, NT_TILES),
        in_specs=[tile(D), tile(256), tile(256), tile(512), _mod_spec(nb),
                  pl.BlockSpec((D, D), lambda b, t: (0, 0)), pl.BlockSpec((1, D), lambda b, t: (0, 0))],
        out_specs=[tile(D), pl.BlockSpec((D, TM), lambda b, t: (0, b * NT_TILES + t))],
        out_shape=[jax.ShapeDtypeStruct((nb, T, D), F32), jax.ShapeDtypeStruct((D, nb * T), BF16)],
        compiler_params=_params("arbitrary", "arbitrary"),
        name="out_proj",
    )(x, yg, yn, yd, mod, w_out, g2)


def _oddeven_pairs(n):
    pairs = []

    def merge(lo, hi, r):
        step = r * 2
        if step < hi - lo:
            merge(lo, hi, step)
            merge(lo + r, hi, step)
            pairs.extend((i, i + r) for i in range(lo + r, hi - r, step))
        else:
            pairs.append((lo, lo + r))

    def sort(lo, hi):
        if hi - lo >= 1:
            mid = lo + (hi - lo) // 2
            sort(lo, mid)
            sort(mid + 1, hi)
            merge(lo, hi, 1)

    sort(0, n - 1)
    return pairs


def _bitonic_pairs(n):
    pairs = []
    d = n // 2
    while d >= 1:
        pairs.extend((i, i + d) for i in range(n) if (i // d) % 2 == 0)
        d //= 2
    return pairs


_SORT16 = _oddeven_pairs(16)
_BITONIC16 = _bitonic_pairs(16)


def _compare_exchange(xs, pairs):
    for i, j in pairs:
        a, b = xs[i], xs[j]
        if b is None:
            continue
        if a is None:
            xs[i], xs[j] = b, None
        else:
            xs[i], xs[j] = jnp.maximum(a, b), jnp.minimum(a, b)


def _merge_top16(xs, ys):
    zs = []
    for k in range(16):
        a, b = xs[k], ys[15 - k]
        zs.append(a if b is None else (b if a is None else jnp.maximum(a, b)))
    _compare_exchange(zs, _BITONIC16)
    return zs


def _top16_of_128(x3):
    xs = [x3[r] for r in range(16)]
    _compare_exchange(xs, _SORT16)
    for shift in (4, 2, 1):
        xs = _merge_top16(xs, [pltpu.roll(x, shift, 0) for x in xs])
    return xs


def _peer_select(s1, s2):
    n = s1.shape[-1]
    a3 = s1.reshape(16, 8, n)
    b3 = s2.reshape(16, 8, n)
    t1 = _top16_of_128(a3)
    t2 = _top16_of_128(b3)
    cand = [[t1[a] + t2[b] for b in range(PEER_TOPK // (a + 1))] for a in range(PEER_TOPK)]
    pad = lambda xs: xs + [None] * (16 - len(xs))
    top = cand[0]
    for a in range(1, 8):
        top = _merge_top16(top, pad(cand[a]))
    top = _merge_top16(top, pad([cand[a][0] for a in range(8, 16)]))
    tau = top[15]
    cmax = cand[0][0]
    inf = jnp.full_like(tau, jnp.inf)
    z = jnp.zeros_like(tau)
    thr = []
    for a in range(PEER_TOPK):
        th = inf
        for b, cv in enumerate(cand[a]):
            sel = cv >= tau
            th = jnp.where(sel, jnp.minimum(th, t2[b]), th)
            z = z + jnp.where(sel, jnp.exp(cv - cmax), 0.0)
        thr.append(th)
    theta = jnp.full_like(a3, jnp.inf)
    for a in range(PEER_TOPK):
        theta = jnp.where(a3 == t1[a][None], thr[a][None], theta)
    c = jnp.exp(a3 - t1[0][None]) / z[None]
    e2 = jnp.exp(b3 - t2[0][None])
    return theta, c, e2.reshape(PEER_NKEYS, n)


def _peer_kernel(ht_ref, wq_ref, sk_ref, u_ref, vt_ref, o_ref, th_s, c_s, s2_s, e2_s, h8_s, a_s, w_s, acc_s):
    e = pl.program_id(1)
    tb = ht_ref.shape[1]

    @pl.when(e == 0)
    def _():
        qt = jnp.dot(wq_ref[...], ht_ref[...], preferred_element_type=F32)
        sk = sk_ref[...]
        for h in range(PEER_HEADS):
            s1 = jnp.dot(sk[0], qt[h * 128:h * 128 + 64], precision=HIGHEST, preferred_element_type=F32)
            s2 = jnp.dot(sk[1], qt[h * 128 + 64:h * 128 + 128], precision=HIGHEST, preferred_element_type=F32)
            theta, c, e2 = _peer_select(s1, s2)
            th_s[h] = theta
            c_s[h] = c * PEER_W_SCALE
            for lc in range(tb // 128):
                s2_s[h, lc] = s2[:, lc * 128:(lc + 1) * 128]
                e2_s[h, lc] = e2[:, lc * 128:(lc + 1) * 128]
        acc_s[...] = jnp.zeros_like(acc_s)
        h8_s[...] = jnp.clip(ht_ref[...], -F8_MAX, F8_MAX).astype(F8)

    a = jnp.dot(u_ref[...], h8_s[...], preferred_element_type=F32) * (1.0 / PEER_U_SCALE)
    for lc in range(tb // 128):
        a_s[lc] = a[:, lc * 128:(lc + 1) * 128]
    for lc in range(tb // 128):
        lanes = slice(lc * 128, (lc + 1) * 128)
        for i in range(PEER_ROWS):
            g = jnp.zeros((PEER_NKEYS, 128), F32)
            for h in range(PEER_HEADS):
                th = th_s[h, e, i:i + 1, lanes]
                cw = c_s[h, e, i:i + 1, lanes]
                g = g + jnp.where(s2_s[h, lc] >= th, e2_s[h, lc] * cw, 0.0)
            rows = slice(i * PEER_NKEYS, (i + 1) * PEER_NKEYS)
            w = g * jax.nn.gelu(a_s[lc, rows, :], approximate=True)
            w_s[rows, lanes] = jnp.minimum(w, F8_MAX).astype(F8)
    acc_s[...] += jnp.dot(vt_ref[...], w_s[...], preferred_element_type=F32)

    @pl.when(e == pl.num_programs(1) - 1)
    def _():
        o_ref[...] = acc_s[...].T * (1.0 / PEER_W_SCALE)


def _peer(ht, wq_t, sk, u, v_t):
    n = ht.shape[1]
    tb = PEER_TB
    ne = PEER_EXPERTS // PEER_EB
    sel = pltpu.VMEM((PEER_HEADS, tb // 128, PEER_NKEYS, 128), F32)
    sel8 = pltpu.VMEM((PEER_HEADS, PEER_NKEYS // 8, 8, tb), F32)
    return pl.pallas_call(
        _peer_kernel,
        grid=(n // tb, ne),
        in_specs=[pl.BlockSpec((D, tb), lambda t, e: (0, t)),
                  pl.BlockSpec((D, D), lambda t, e: (0, 0)),
                  pl.BlockSpec((2, PEER_NKEYS, PEER_HALF), lambda t, e: (0, 0, 0)),
                  pl.BlockSpec((PEER_EB, D), lambda t, e: (e, 0)),
                  pl.BlockSpec((D, PEER_EB), lambda t, e: (0, e))],
        out_specs=pl.BlockSpec((tb, D), lambda t, e: (t, 0)),
        out_shape=jax.ShapeDtypeStruct((n, D), F32),
        scratch_shapes=[sel8, sel8, sel, sel, pltpu.VMEM((D, tb), F8),
                        pltpu.VMEM((tb // 128, PEER_EB, 128), F32), pltpu.VMEM((PEER_EB, tb), F8),
                        pltpu.VMEM((D, tb), F32)],
        compiler_params=_params("arbitrary", "arbitrary"),
        name="peer",
    )(ht, wq_t, sk, u, v_t)


def _final_kernel(x_ref, pr_ref, mod_ref, g_ref, o_ref):
    x = x_ref[0] + mod_ref[0][:, 5 * D:6 * D] * pr_ref[0]
    o_ref[0] = _rms(x) * g_ref[...]


def _final(x, peer, mod, g):
    nb = x.shape[0]
    lat = pl.BlockSpec((1, TM, D), lambda b, t: (b, t + 1, 0))
    return pl.pallas_call(
        _final_kernel,
        grid=(nb, SEQ // TM),
        in_specs=[lat, lat, pl.BlockSpec((1, 1, 6 * D), lambda b, t: (b, 0, 0)),
                  pl.BlockSpec((1, D), lambda b, t: (0, 0))],
        out_specs=pl.BlockSpec((1, TM, D), lambda b, t: (b, t, 0)),
        out_shape=jax.ShapeDtypeStruct((nb, SEQ, D), F32),
        compiler_params=_params("arbitrary", "arbitrary"),
        name="final_norm",
    )(x, peer, mod, g)


def _rope_tables():
    col = np.arange(DIFF_HEADS * 2 * DIFF_DH)
    d = col % DIFF_DH
    half = DIFF_DH // 4
    inv = ROPE_BASE ** (-jnp.arange(half, dtype=F32) / half)
    inv_c = inv[d % half]
    by_col = jnp.asarray((d // (DIFF_DH // 2)) == 1)
    t = jnp.arange(SEQ)
    pos = jnp.where(by_col[None, :], (t % GRID_W)[:, None], (t // GRID_W)[:, None]).astype(F32)
    ang = pos * inv_c[None, :]
    cos = jnp.concatenate([jnp.ones((CTX, col.size), F32), jnp.cos(ang)], axis=0)
    sin = jnp.concatenate([jnp.zeros((CTX, col.size), F32), jnp.sin(ang)], axis=0)
    return cos, sin


def _rot_columns(w):
    col = np.arange(w.shape[1])
    lo = (col % (DIFF_DH // 2)) < DIFF_DH // 4
    src = np.where(lo, col + DIFF_DH // 4, col - DIFF_DH // 4)
    sign = np.where(lo, -1.0, 1.0).astype(np.float32)
    return w[:, src] * sign[None, :]


def _regroup_w_in(w):
    sizes = (128, 128, 256, 256, 16, 16, 256, 256, 256, 512, 512, 512)
    pts = np.cumsum((0,) + sizes)
    p = [w[:, pts[i]:pts[i + 1]] for i in range(12)]
    downs = jnp.concatenate([p[4], p[5], jnp.zeros((D, 128 - 2 * GLA_RANK), w.dtype)], axis=1)
    cols = [p[0], p[1], p[2], p[3], downs, p[6], p[7], p[8], p[9], p[10], p[11], _rot_columns(p[9]), _rot_columns(p[10])]
    return jnp.concatenate(cols, axis=1).astype(BF16)


def _gate_up_padded(gate_up):
    z = jnp.zeros((128, 128), F32)
    gf = z.at[0:GLA_RANK].set(gate_up[0]).astype(BF16)
    gb = z.at[GLA_RANK:2 * GLA_RANK].set(gate_up[1]).astype(BF16)
    return gf, gb


def kernel(x, c, ctx, c_ctx, ada_w, ada_b, norm1_g, norm2_g, w_in, gla_gate_up, gla_gate_b, gla_norm_g, nat_rpb,
           diff_lambda, diff_norm_g, w_out, peer_wq, peer_subkeys, peer_u, peer_v, final_g):
    nb = x.shape[0]
    depth = ada_w.shape[0]
    assert x.shape[1:] == (SEQ, D) and ctx.shape[1:] == (CTX, D) and (nb * T) % PEER_TB == 0

    rows = -(-(nb + 1) // 8) * 8
    cc = jnp.concatenate([c, c_ctx[None], jnp.zeros((rows - nb - 1, D), F32)], axis=0)
    mod_all = _modulation(cc, ada_w, ada_b)
    cos, sin = _rope_tables()

    xa = jnp.concatenate([ctx, x], axis=1)
    peer = None
    mod_prev = None
    for l in range(depth):
        lam_init = 0.8 - 0.6 * math.exp(-0.3 * l)
        mod = mod_all[l].reshape(rows, 1, 6 * D)
        gf, gb = _gate_up_padded(gla_gate_up[l])
        xa, (gq, gk, ggf, ggb, gv, gr, nq, nk, nv, dq, dk, dv) = _norm_proj(
            xa, peer, mod_prev, mod, norm1_g[l][None], _regroup_w_in(w_in[l]), gf, gb, gla_gate_b[l], cos, sin)
        yg = _gla(gq, gk, ggf, ggb, gv, gr, gla_norm_g[l][None])
        yn = _nat(nq, nk, nv, _nat_bias_table(nat_rpb[l]))
        yd = _diff(dq, dk, dv, diff_lambda[l], diff_norm_g[l][None], lam_init)
        xa, ht = _out_proj(xa, yg, yn, yd, mod, w_out[l].astype(BF16), norm2_g[l][None])
        peer = _peer(ht, peer_wq[l].T.astype(BF16), peer_subkeys[l],
                     _to_f8(peer_u[l] * PEER_U_SCALE), _to_f8(peer_v[l].T)).reshape(nb, T, D)
        mod_prev = mod
    return _final(xa, peer, mod_prev, final_g[None])
```

```python
import functools
import math

import numpy as np
import jax
import jax.numpy as jnp
from jax import lax
from jax.experimental import pallas as pl
from jax.experimental.pallas import tpu as pltpu

F32 = jnp.float32
BF16 = jnp.bfloat16
F8 = jnp.float8_e4m3fn
F8_MAX = 448.0
HIGHEST = lax.Precision.HIGHEST

D = 1024
SEQ = 2048
CTX = 256
T = SEQ + CTX
GRID_W = 64
GRID_ROWS = SEQ // GRID_W
EPS = 1e-6
TM = 256
NT_TILES = T // TM

GLA_HEADS, GLA_DK, GLA_DV, GLA_RANK, GLA_TAU, GLA_CHUNK = 4, 32, 64, 16, 16.0, 64
NAT_HEADS, NAT_DH, NAT_WIN_R, NAT_WIN_C = 4, 64, 8, 16
DIFF_HEADS, DIFF_DH = 4, 64
ROPE_BASE = 10000.0
PEER_HEADS, PEER_NKEYS, PEER_HALF, PEER_TOPK = 8, 128, 64, 16
PEER_EXPERTS = PEER_NKEYS * PEER_NKEYS
PEER_TB = 512
PEER_ROWS = 16
PEER_EB = PEER_ROWS * PEER_NKEYS
PEER_U_SCALE = 32.0
PEER_W_SCALE = 4.0

C_GQ, C_GK, C_GV, C_GR, C_GD = 0, 128, 256, 512, 768
C_NQ, C_NK, C_NV = 896, 1152, 1408
C_DQ, C_DK, C_DV, C_DQR, C_DKR = 1664, 2176, 2688, 3200, 3712
PROJ_COLS = 4224

_NT_DIMS = (((1,), (1,)), ((), ()))


def _nt(a, b):
    return lax.dot_general(a, b, _NT_DIMS, preferred_element_type=F32)


def _params(*sem):
    return pltpu.CompilerParams(dimension_semantics=sem, vmem_limit_bytes=56 * 1024 * 1024)


def _to_f8(x):
    return jnp.clip(x, -F8_MAX, F8_MAX).astype(F8)


def _rms(x):
    return x * lax.rsqrt(jnp.mean(x * x, axis=-1, keepdims=True) + EPS)


def _sigmoid(x):
    return 1.0 / (1.0 + jnp.exp(-x))


def _log_sigmoid(x):
    return jnp.minimum(x, 0.0) - jnp.log1p(jnp.exp(-jnp.abs(x)))


def _mod_kernel(c_ref, w_ref, b_ref, o_ref):
    c = c_ref[...]
    o_ref[0] = jnp.dot(c * _sigmoid(c), w_ref[0], precision=HIGHEST, preferred_element_type=F32) + b_ref[0]


def _modulation(cc, ada_w, ada_b):
    depth = ada_w.shape[0]
    rows = cc.shape[0]
    return pl.pallas_call(
        _mod_kernel,
        grid=(depth, 6),
        in_specs=[pl.BlockSpec((rows, D), lambda l, j: (0, 0)),
                  pl.BlockSpec((1, D, D), lambda l, j: (l, 0, j)),
                  pl.BlockSpec((1, 1, D), lambda l, j: (l, 0, j))],
        out_specs=pl.BlockSpec((1, rows, D), lambda l, j: (l, 0, j)),
        out_shape=jax.ShapeDtypeStruct((depth, rows, 6 * D), F32),
        compiler_params=_params("arbitrary", "arbitrary"),
        name="adaln_mod",
    )(cc, ada_w, ada_b.reshape(depth, 1, 6 * D))


def _proj_kernel(*refs, fuse_res):
    if fuse_res:
        x_ref, pr_ref, modp_ref = refs[:3]
        refs = refs[3:]
    else:
        x_ref = refs[0]
        refs = refs[1:]
    (mod_ref, g_ref, w_ref, gf_ref, gb_ref, gbias_ref, cos_ref, sin_ref) = refs[:8]
    outs = refs[8:]
    if fuse_res:
        xo_ref = outs[0]
        outs = outs[1:]
    (gq_ref, gk_ref, ggf_ref, ggb_ref, gv_ref, gr_ref, nq_ref, nk_ref, nv_ref, dq_ref, dk_ref, dv_ref) = outs

    x = x_ref[0]
    if fuse_res:
        x = x + modp_ref[0][:, 5 * D:6 * D] * pr_ref[0]
        xo_ref[0] = x
    m = mod_ref[0]
    h = _rms(x) * g_ref[...] * (1.0 + m[:, D:2 * D]) + m[:, 0:D]
    p = jnp.dot(h.astype(BF16), w_ref[...], preferred_element_type=F32)

    gq_ref[0] = p[:, C_GQ:C_GQ + 128] * (GLA_DK ** -0.5)
    gk_ref[0] = p[:, C_GK:C_GK + 128]
    gv_ref[0] = p[:, C_GV:C_GV + 256]
    gr_ref[0] = p[:, C_GR:C_GR + 256]
    pd = p[:, C_GD:C_GD + 128].astype(BF16)
    gbias = gbias_ref[...]
    xf = jnp.dot(pd, gf_ref[...], preferred_element_type=F32) + gbias[0:1]
    xb = jnp.dot(pd, gb_ref[...], preferred_element_type=F32) + gbias[1:2]
    ggf_ref[0] = _log_sigmoid(xf) / GLA_TAU
    ggb_ref[0] = _log_sigmoid(xb) / GLA_TAU

    nq_ref[0] = (p[:, C_NQ:C_NQ + 256] * (NAT_DH ** -0.5)).astype(BF16)
    nk_ref[0] = p[:, C_NK:C_NK + 256].astype(BF16)
    nv_ref[0] = p[:, C_NV:C_NV + 256].astype(BF16)

    cos = cos_ref[...]
    sin = sin_ref[...]
    dq = p[:, C_DQ:C_DQ + 512] * cos + p[:, C_DQR:C_DQR + 512] * sin
    dq_ref[0] = (dq * (DIFF_DH ** -0.5)).astype(BF16)
    dk_ref[0] = (p[:, C_DK:C_DK + 512] * cos + p[:, C_DKR:C_DKR + 512] * sin).astype(BF16)
    dv_ref[0] = p[:, C_DV:C_DV + 512].astype(BF16)


def _mod_spec(nb, t0=0):
    return pl.BlockSpec((1, 1, 6 * D), lambda b, t: (jnp.where(t + t0 == 0, nb, b), 0, 0))


def _norm_proj(x, peer, mod_prev, mod, g1, w_all, gf, gb, gbias, cos, sin):
    nb = x.shape[0]
    fuse = peer is not None
    tile = lambda w: pl.BlockSpec((1, TM, w), lambda b, t: (b, t, 0))
    full = lambda a: pl.BlockSpec(a.shape, lambda b, t: (0,) * a.ndim)
    ins, specs = [x], [tile(D)]
    if fuse:
        ins += [peer, mod_prev]
        specs += [tile(D), _mod_spec(nb)]
    ins += [mod, g1, w_all, gf, gb, gbias, cos, sin]
    specs += [_mod_spec(nb), full(g1), full(w_all), full(gf), full(gb), full(gbias),
              pl.BlockSpec((TM, 512), lambda b, t: (t, 0)), pl.BlockSpec((TM, 512), lambda b, t: (t, 0))]
    widths = [(128, F32)] * 4 + [(256, F32)] * 2 + [(256, BF16)] * 3 + [(512, BF16)] * 3
    out_shape = [jax.ShapeDtypeStruct((nb, T, w), dt) for w, dt in widths]
    out_specs = [tile(w) for w, _ in widths]
    if fuse:
        out_shape = [jax.ShapeDtypeStruct((nb, T, D), F32)] + out_shape
        out_specs = [tile(D)] + out_specs
    res = pl.pallas_call(
        functools.partial(_proj_kernel, fuse_res=fuse),
        grid=(nb, NT_TILES),
        in_specs=specs, out_specs=out_specs, out_shape=out_shape,
        compiler_params=_params("arbitrary", "arbitrary"),
        name="norm_proj",
    )(*ins)
    if fuse:
        return res[0], res[1:]
    return x, res


def _gla_kernel(q_ref, k_ref, gf_ref, gb_ref, v_ref, r_ref, ng_ref, o_ref, of_s, ob_s, sf_s, sb_s):
    C = GLA_CHUNK
    nch = T // C
    nctx = CTX // C
    ii = lax.broadcasted_iota(jnp.int32, (C, C), 0)
    jj = lax.broadcasted_iota(jnp.int32, (C, C), 1)
    tri_f = (jj <= ii).astype(F32)
    tri_b = (jj >= ii).astype(F32)
    it = lax.broadcasted_iota(jnp.int32, (C, 4 * C), 0)
    jt = lax.broadcasted_iota(jnp.int32, (C, 4 * C), 1) & (C - 1)
    cm_f = jt <= it
    cm_b = jt >= it
    kmask = ((lax.broadcasted_iota(jnp.int32, (4 * C, 128), 0) >> 6)
             == (lax.broadcasted_iota(jnp.int32, (4 * C, 128), 1) >> 5)).astype(F32)
    vmask = ((lax.broadcasted_iota(jnp.int32, (4 * C, 256), 0) >> 6)
             == (lax.broadcasted_iota(jnp.int32, (4 * C, 256), 1) >> 6)).astype(F32)
    smask = ((lax.broadcasted_iota(jnp.int32, (256, 128), 0) >> 6)
             == (lax.broadcasted_iota(jnp.int32, (256, 128), 1) >> 5)).astype(F32)

    sf_s[...] = jnp.zeros_like(sf_s)
    sb_s[...] = jnp.zeros_like(sb_s)

    def chunk(off, g_ref, s_ref, dst, tri, cm, last, mid):
        rows = pl.ds(off, C)
        q = q_ref[0, rows, :]
        k = k_ref[0, rows, :]
        v = v_ref[0, rows, :]
        g = g_ref[0, rows, :]
        st = s_ref[...]
        b = jnp.dot(tri, g, precision=HIGHEST, preferred_element_type=F32)
        b_last = b[last:last + 1]
        b_mid = b[mid:mid + 1]
        qi = q * jnp.exp(jnp.clip(b - b_mid, -80.0, 80.0))
        ki = k * jnp.exp(jnp.clip(b_mid - b, -80.0, 80.0))
        krows = jnp.concatenate([ki, ki, ki, ki], axis=0) * kmask
        a = jnp.where(cm, _nt(qi.astype(BF16), krows.astype(BF16)), 0.0)
        vblk = jnp.concatenate([v, v, v, v], axis=0) * vmask
        o = jnp.dot(a.astype(BF16), vblk.astype(BF16), preferred_element_type=F32)
        o = o + _nt((q * jnp.exp(b)).astype(BF16), st.astype(BF16))
        dst[rows, :] = o
        kd = k * jnp.exp(b_last - b)
        upd = jnp.dot(v.T.astype(BF16), kd.astype(BF16), preferred_element_type=F32)
        s_ref[...] = st * jnp.exp(b_last) + upd * smask

    def body(t, carry):
        cb = jnp.where(t < nctx, nctx - 1 - t, nch + nctx - 1 - t)
        chunk(pl.multiple_of(t * C, C), gf_ref, sf_s, of_s, tri_f, cm_f, C - 1, C // 2 - 1)
        chunk(pl.multiple_of(cb * C, C), gb_ref, sb_s, ob_s, tri_b, cm_b, 0, C // 2)
        return carry

    lax.fori_loop(0, nch, body, 0)

    hones = ((lax.broadcasted_iota(jnp.int32, (256, 256), 0) >> 6)
             == (lax.broadcasted_iota(jnp.int32, (256, 256), 1) >> 6)).astype(F32) * (1.0 / GLA_DV)
    ng = ng_ref[...]
    for t in range(NT_TILES):
        rows = pl.ds(t * TM, TM)
        o = of_s[rows, :] + ob_s[rows, :]
        ms = jnp.dot(o * o, hones, precision=HIGHEST, preferred_element_type=F32)
        r = r_ref[0, rows, :]
        o_ref[0, rows, :] = (o * lax.rsqrt(ms + EPS) * ng * (r * _sigmoid(r))).astype(BF16)


def _gla(gq, gk, ggf, ggb, gv, gr, ng):
    nb = gq.shape[0]
    seq = lambda w: pl.BlockSpec((1, T, w), lambda b: (b, 0, 0))
    return pl.pallas_call(
        _gla_kernel,
        grid=(nb,),
        in_specs=[seq(128), seq(128), seq(128), seq(128), seq(256), seq(256),
                  pl.BlockSpec((1, 256), lambda b: (0, 0))],
        out_specs=seq(256),
        out_shape=jax.ShapeDtypeStruct((nb, T, 256), BF16),
        scratch_shapes=[pltpu.VMEM((T, 256), F32), pltpu.VMEM((T, 256), F32),
                        pltpu.VMEM((256, 128), F32), pltpu.VMEM((256, 128), F32)],
        compiler_params=_params("arbitrary"),
        name="gla",
    )(gq, gk, ggf, ggb, gv, gr, ng)


def _stacked_heads(q, n):
    rows = lax.broadcasted_iota(jnp.int32, (NAT_HEADS * n, 256), 0)
    lanes = lax.broadcasted_iota(jnp.int32, (NAT_HEADS * n, 256), 1) >> 6
    head = jnp.zeros_like(rows)
    for h in range(1, NAT_HEADS):
        head = head + (rows >= h * n).astype(jnp.int32)
    own = head == lanes
    q4 = jnp.concatenate([q] * NAT_HEADS, axis=0)
    return jnp.where(own, q4, jnp.zeros_like(q4)), own


def _unstack_heads(pv, own, n):
    pv = jnp.where(own, pv, 0.0)
    out = pv[0:n]
    for h in range(1, NAT_HEADS):
        out = out + pv[h * n:(h + 1) * n]
    return out


def _nat_kernel(q_ref, k_ref, v_ref, bias_ref, o_ref, *, need_ctx):
    out0 = CTX if need_ctx else 0
    if need_ctx:
        kc = k_ref[0, 0:CTX, :]
        vc = v_ref[0, 0:CTX, :]
        q4, own = _stacked_heads(q_ref[0, 0:CTX, :], CTX)
        s = _nt(q4, kc)
        e = jnp.exp(s - jnp.max(s, axis=-1, keepdims=True))
        p = (e * (1.0 / jnp.sum(e, axis=-1, keepdims=True))).astype(BF16)
        o_ref[0, 0:CTX, :] = _unstack_heads(jnp.dot(p, vc, preferred_element_type=F32), own, CTX).astype(BF16)

    nwin = NAT_WIN_R * GRID_W

    def row(r, carry):
        rs = jnp.clip(r - NAT_WIN_R // 2, 0, GRID_ROWS - NAT_WIN_R)
        case = rs - r + NAT_WIN_R - 1
        qrows = pl.ds(pl.multiple_of(CTX + r * GRID_W, GRID_W), GRID_W)
        wrows = pl.ds(pl.multiple_of(CTX + rs * GRID_W, GRID_W), nwin)
        q4, own = _stacked_heads(q_ref[0, qrows, :], GRID_W)
        sw = _nt(q4, k_ref[0, wrows, :]) + bias_ref[case]
        sc = _nt(q4, k_ref[0, 0:CTX, :])
        m = jnp.maximum(jnp.max(sw, axis=-1, keepdims=True), jnp.max(sc, axis=-1, keepdims=True))
        ew = jnp.exp(sw - m)
        ec = jnp.exp(sc - m)
        inv = 1.0 / (jnp.sum(ew, axis=-1, keepdims=True) + jnp.sum(ec, axis=-1, keepdims=True))
        pv = jnp.dot((ew * inv).astype(BF16), v_ref[0, wrows, :], preferred_element_type=F32)
        pv = pv + jnp.dot((ec * inv).astype(BF16), v_ref[0, 0:CTX, :], preferred_element_type=F32)
        orows = pl.ds(pl.multiple_of(out0 + r * GRID_W, GRID_W), GRID_W)
        o_ref[0, orows, :] = _unstack_heads(pv, own, GRID_W).astype(BF16)
        return carry

    lax.fori_loop(0, GRID_ROWS, row, 0)


def _nat(nq, nk, nv, bias, need_ctx):
    nb = nq.shape[0]
    rows_out = T if need_ctx else SEQ
    seq = pl.BlockSpec((1, T, 256), lambda b: (b, 0, 0))
    return pl.pallas_call(
        functools.partial(_nat_kernel, need_ctx=need_ctx),
        grid=(nb,),
        in_specs=[seq, seq, seq, pl.BlockSpec(bias.shape, lambda b: (0, 0, 0))],
        out_specs=pl.BlockSpec((1, rows_out, 256), lambda b: (b, 0, 0)),
        out_shape=jax.ShapeDtypeStruct((nb, rows_out, 256), BF16),
        compiler_params=_params("arbitrary"),
        name="nat",
    )(nq, nk, nv, bias)


def _nat_bias_table(rpb):
    cidx = np.arange(GRID_W)
    col_start = np.clip(cidx - NAT_WIN_C // 2, 0, GRID_W - NAT_WIN_C)
    col_mask = (cidx[None, :] >= col_start[:, None]) & (cidx[None, :] < col_start[:, None] + NAT_WIN_C)
    dc = np.clip(cidx[None, :] - cidx[:, None] + NAT_WIN_C - 1, 0, 2 * NAT_WIN_C - 2)
    dr = np.arange(NAT_WIN_R)[:, None] + np.arange(NAT_WIN_R)[None, :]
    t = rpb.astype(F32)[:, dr][:, :, :, dc]
    t = jnp.where(col_mask[None, None, None], t, -jnp.inf)
    return t.transpose(1, 0, 3, 2, 4).reshape(NAT_WIN_R, NAT_HEADS * GRID_W, NAT_WIN_R * GRID_W)


def _diff_kernel(lam_ref, q_ref, k_ref, v_ref, g_ref, o_ref, *, lam_init, need_ctx):
    lv = lam_ref[...]
    lam = (jnp.exp(jnp.sum(lv[0:1] * lv[1:2], axis=-1, keepdims=True))
           - jnp.exp(jnp.sum(lv[2:3] * lv[3:4], axis=-1, keepdims=True)) + lam_init)
    q = q_ref[0]
    first = lax.broadcasted_iota(jnp.int32, (1, 2 * DIFF_DH), 1) < DIFF_DH
    q1 = jnp.where(first, q, jnp.zeros_like(q))
    q2 = jnp.where(first, jnp.zeros_like(q), q)

    def attend(k, v):
        s1 = _nt(q1, k)
        s2 = _nt(q2, k)
        e1 = jnp.exp(s1 - jnp.max(s1, axis=-1, keepdims=True))
        e2 = jnp.exp(s2 - jnp.max(s2, axis=-1, keepdims=True))
        w1 = 1.0 / jnp.sum(e1, axis=-1, keepdims=True)
        w2 = lam / jnp.sum(e2, axis=-1, keepdims=True)
        p = (e1 * w1 - e2 * w2).astype(BF16)
        o = jnp.dot(p, v, preferred_element_type=F32)
        o_ref[0] = (_rms(o) * g_ref[...] * (1.0 - lam_init)).astype(BF16)

    if need_ctx:
        @pl.when(pl.program_id(2) == 0)
        def _():
            attend(k_ref[0, 0:CTX, :], v_ref[0, 0:CTX, :])

        @pl.when(pl.program_id(2) > 0)
        def _():
            attend(k_ref[0], v_ref[0])
    else:
        attend(k_ref[0], v_ref[0])


def _diff(dq, dk, dv, lam_vec, dg, lam_init, need_ctx):
    nb = dq.shape[0]
    w = 2 * DIFF_DH
    t0 = 0 if need_ctx else 1
    return pl.pallas_call(
        functools.partial(_diff_kernel, lam_init=lam_init, need_ctx=need_ctx),
        grid=(nb, DIFF_HEADS, NT_TILES - t0),
        in_specs=[pl.BlockSpec((4, DIFF_DH), lambda b, h, t: (0, 0)),
                  pl.BlockSpec((1, TM, w), lambda b, h, t: (b, t + t0, h)),
                  pl.BlockSpec((1, T, w), lambda b, h, t: (b, 0, h)),
                  pl.BlockSpec((1, T, w), lambda b, h, t: (b, 0, h)),
                  pl.BlockSpec((1, w), lambda b, h, t: (0, h))],
        out_specs=pl.BlockSpec((1, TM, w), lambda b, h, t: (b, t, h)),
        out_shape=jax.ShapeDtypeStruct((nb, T - t0 * TM, DIFF_HEADS * w), BF16),
        compiler_params=_params("arbitrary", "arbitrary", "arbitrary"),
        name="diff_attn",
    )(lam_vec, dq, dk, dv, dg)


def _out_kernel(x_ref, yg_ref, yn_ref, yd_ref, mod_ref, w_ref, g_ref, xo_ref, ht_ref):
    attn = jnp.dot(yg_ref[0], w_ref[0:256, :], preferred_element_type=F32)
    attn = attn + jnp.dot(yn_ref[0], w_ref[256:512, :], preferred_element_type=F32)
    attn = attn + jnp.dot(yd_ref[0], w_ref[512:1024, :], preferred_element_type=F32)
    m = mod_ref[0]
    x = x_ref[0] + m[:, 2 * D:3 * D] * attn
    xo_ref[0] = x
    h2 = _rms(x) * g_ref[...] * (1.0 + m[:, 4 * D:5 * D]) + m[:, 3 * D:4 * D]
    ht_ref[...] = h2.T.astype(BF16)


def _out_proj(x, yg, yn, yd, mod, w_out, g2, need_ctx):
    nb = x.shape[0]
    t0 = 0 if need_ctx else 1
    nt = NT_TILES - t0
    full = lambda w: pl.BlockSpec((1, TM, w), lambda b, t: (b, t + t0, 0))
    tile = lambda w: pl.BlockSpec((1, TM, w), lambda b, t: (b, t, 0))
    return pl.pallas_call(
        _out_kernel,
        grid=(nb, nt),
        in_specs=[full(D), full(256), tile(256), tile(512), _mod_spec(nb, t0),
                  pl.BlockSpec((D, D), lambda b, t: (0, 0)), pl.BlockSpec((1, D), lambda b, t: (0, 0))],
        out_specs=[tile(D), pl.BlockSpec((D, TM), lambda b, t: (0, b * nt + t))],
        out_shape=[jax.ShapeDtypeStruct((nb, nt * TM, D), F32), jax.ShapeDtypeStruct((D, nb * nt * TM), BF16)],
        compiler_params=_params("arbitrary", "arbitrary"),
        name="out_proj",
    )(x, yg, yn, yd, mod, w_out, g2)


def _oddeven_pairs(n):
    pairs = []

    def merge(lo, hi, r):
        step = r * 2
        if step < hi - lo:
            merge(lo, hi, step)
            merge(lo + r, hi, step)
            pairs.extend((i, i + r) for i in range(lo + r, hi - r, step))
        else:
            pairs.append((lo, lo + r))

    def sort(lo, hi):
        if hi - lo >= 1:
            mid = lo + (hi - lo) // 2
            sort(lo, mid)
            sort(mid + 1, hi)
            merge(lo, hi, 1)

    sort(0, n - 1)
    return pairs


def _bitonic_pairs(n):
    pairs = []
    d = n // 2
    while d >= 1:
        pairs.extend((i, i + d) for i in range(n) if (i // d) % 2 == 0)
        d //= 2
    return pairs


_SORT16 = _oddeven_pairs(16)
_BITONIC16 = _bitonic_pairs(16)


def _compare_exchange(xs, pairs):
    for i, j in pairs:
        a, b = xs[i], xs[j]
        if b is None:
            continue
        if a is None:
            xs[i], xs[j] = b, None
        else:
            xs[i], xs[j] = jnp.maximum(a, b), jnp.minimum(a, b)


def _merge_top16(xs, ys):
    zs = []
    for k in range(16):
        a, b = xs[k], ys[15 - k]
        zs.append(a if b is None else (b if a is None else jnp.maximum(a, b)))
    _compare_exchange(zs, _BITONIC16)
    return zs


def _top16_of_128(x3):
    xs = [x3[r] for r in range(16)]
    _compare_exchange(xs, _SORT16)
    for shift in (4, 2, 1):
        xs = _merge_top16(xs, [pltpu.roll(x, shift, 0) for x in xs])
    return xs


def _peer_select(s1, s2):
    n = s1.shape[-1]
    a3 = s1.reshape(16, 8, n)
    b3 = s2.reshape(16, 8, n)
    t1 = _top16_of_128(a3)
    t2 = _top16_of_128(b3)
    cand = [[t1[a] + t2[b] for b in range(PEER_TOPK // (a + 1))] for a in range(PEER_TOPK)]
    pad = lambda xs: xs + [None] * (16 - len(xs))
    top = cand[0]
    for a in range(1, 8):
        top = _merge_top16(top, pad(cand[a]))
    top = _merge_top16(top, pad([cand[a][0] for a in range(8, 16)]))
    tau = top[15]
    cmax = cand[0][0]
    inf = jnp.full_like(tau, jnp.inf)
    z = jnp.zeros_like(tau)
    thr = []
    for a in range(PEER_TOPK):
        th = inf
        for b, cv in enumerate(cand[a]):
            sel = cv >= tau
            th = jnp.where(sel, jnp.minimum(th, t2[b]), th)
            z = z + jnp.where(sel, jnp.exp(cv - cmax), 0.0)
        thr.append(th)
    theta = jnp.full_like(a3, jnp.inf)
    for a in range(PEER_TOPK):
        theta = jnp.where(a3 == t1[a][None], thr[a][None], theta)
    c = jnp.exp(a3 - t1[0][None]) / z[None]
    e2 = jnp.exp(b3 - t2[0][None])
    return theta, c, e2.reshape(PEER_NKEYS, n)


def _peer_kernel(ht_ref, wq_ref, sk_ref, u_ref, vt_ref, o_ref, th_s, c_s, s2_s, e2_s, h8_s, a_s, w_s, acc_s):
    e = pl.program_id(1)
    tb = ht_ref.shape[1]

    @pl.when(e == 0)
    def _():
        qt = jnp.dot(wq_ref[...], ht_ref[...], preferred_element_type=F32)
        sk = sk_ref[...]
        for h in range(PEER_HEADS):
            s1 = jnp.dot(sk[0], qt[h * 128:h * 128 + 64], precision=HIGHEST, preferred_element_type=F32)
            s2 = jnp.dot(sk[1], qt[h * 128 + 64:h * 128 + 128], precision=HIGHEST, preferred_element_type=F32)
            theta, c, e2 = _peer_select(s1, s2)
            th_s[h] = theta
            c_s[h] = c * PEER_W_SCALE
            for lc in range(tb // 128):
                s2_s[h, lc] = s2[:, lc * 128:(lc + 1) * 128]
                e2_s[h, lc] = e2[:, lc * 128:(lc + 1) * 128]
        acc_s[...] = jnp.zeros_like(acc_s)
        h8_s[...] = jnp.clip(ht_ref[...], -F8_MAX, F8_MAX).astype(F8)

    a = jnp.dot(u_ref[...], h8_s[...], preferred_element_type=F32) * (1.0 / PEER_U_SCALE)
    for lc in range(tb // 128):
        a_s[lc] = a[:, lc * 128:(lc + 1) * 128]
    for lc in range(tb // 128):
        lanes = slice(lc * 128, (lc + 1) * 128)
        for i in range(PEER_ROWS):
            g = jnp.zeros((PEER_NKEYS, 128), F32)
            for h in range(PEER_HEADS):
                th = th_s[h, e * (PEER_ROWS // 8) + i // 8, i % 8:i % 8 + 1, lanes]
                cw = c_s[h, e * (PEER_ROWS // 8) + i // 8, i % 8:i % 8 + 1, lanes]
                g = g + jnp.where(s2_s[h, lc] >= th, e2_s[h, lc] * cw, 0.0)
            rows = slice(i * PEER_NKEYS, (i + 1) * PEER_NKEYS)
            w = g * jax.nn.gelu(a_s[lc, rows, :], approximate=True)
            w_s[rows, lanes] = jnp.minimum(w, F8_MAX).astype(F8)
    acc_s[...] += jnp.dot(vt_ref[...], w_s[...], preferred_element_type=F32)

    @pl.when(e == pl.num_programs(1) - 1)
    def _():
        o_ref[...] = acc_s[...].T * (1.0 / PEER_W_SCALE)


def _peer(ht, wq_t, sk, u, v_t):
    n = ht.shape[1]
    tb = PEER_TB
    ne = PEER_EXPERTS // PEER_EB
    sel = pltpu.VMEM((PEER_HEADS, tb // 128, PEER_NKEYS, 128), F32)
    sel8 = pltpu.VMEM((PEER_HEADS, PEER_NKEYS // 8, 8, tb), F32)
    return pl.pallas_call(
        _peer_kernel,
        grid=(n // tb, ne),
        in_specs=[pl.BlockSpec((D, tb), lambda t, e: (0, t)),
                  pl.BlockSpec((D, D), lambda t, e: (0, 0)),
                  pl.BlockSpec((2, PEER_NKEYS, PEER_HALF), lambda t, e: (0, 0, 0)),
                  pl.BlockSpec((PEER_EB, D), lambda t, e: (e, 0)),
                  pl.BlockSpec((D, PEER_EB), lambda t, e: (0, e))],
        out_specs=pl.BlockSpec((tb, D), lambda t, e: (t, 0)),
        out_shape=jax.ShapeDtypeStruct((n, D), F32),
        scratch_shapes=[sel8, sel8, sel, sel, pltpu.VMEM((D, tb), F8),
                        pltpu.VMEM((tb // 128, PEER_EB, 128), F32), pltpu.VMEM((PEER_EB, tb), F8),
                        pltpu.VMEM((D, tb), F32)],
        compiler_params=_params("arbitrary", "arbitrary"),
        name="peer",
    )(ht, wq_t, sk, u, v_t)


def _final_kernel(x_ref, pr_ref, mod_ref, g_ref, o_ref):
    x = x_ref[0] + mod_ref[0][:, 5 * D:6 * D] * pr_ref[0]
    o_ref[0] = _rms(x) * g_ref[...]


def _final(x, peer, mod, g):
    nb = x.shape[0]
    lat = pl.BlockSpec((1, TM, D), lambda b, t: (b, t, 0))
    return pl.pallas_call(
        _final_kernel,
        grid=(nb, SEQ // TM),
        in_specs=[lat, lat, pl.BlockSpec((1, 1, 6 * D), lambda b, t: (b, 0, 0)),
                  pl.BlockSpec((1, D), lambda b, t: (0, 0))],
        out_specs=lat,
        out_shape=jax.ShapeDtypeStruct((nb, SEQ, D), F32),
        compiler_params=_params("arbitrary", "arbitrary"),
        name="final_norm",
    )(x, peer, mod, g)


def _rope_tables():
    col = np.arange(DIFF_HEADS * 2 * DIFF_DH)
    d = col % DIFF_DH
    half = DIFF_DH // 4
    inv = ROPE_BASE ** (-jnp.arange(half, dtype=F32) / half)
    inv_c = inv[d % half]
    by_col = jnp.asarray((d // (DIFF_DH // 2)) == 1)
    t = jnp.arange(SEQ)
    pos = jnp.where(by_col[None, :], (t % GRID_W)[:, None], (t // GRID_W)[:, None]).astype(F32)
    ang = pos * inv_c[None, :]
    cos = jnp.concatenate([jnp.ones((CTX, col.size), F32), jnp.cos(ang)], axis=0)
    sin = jnp.concatenate([jnp.zeros((CTX, col.size), F32), jnp.sin(ang)], axis=0)
    return cos, sin


def _rot_columns(w):
    q = DIFF_DH // 4
    w4 = w.reshape(w.shape[0], -1, 2, q)
    return jnp.stack([-w4[:, :, 1], w4[:, :, 0]], axis=2).reshape(w.shape)


def _regroup_w_in(w):
    sizes = (128, 128, 256, 256, 16, 16, 256, 256, 256, 512, 512, 512)
    pts = np.cumsum((0,) + sizes)
    p = [w[:, pts[i]:pts[i + 1]] for i in range(12)]
    downs = jnp.concatenate([p[4], p[5], jnp.zeros((D, 128 - 2 * GLA_RANK), w.dtype)], axis=1)
    cols = [p[0], p[1], p[2], p[3], downs, p[6], p[7], p[8], p[9], p[10], p[11], _rot_columns(p[9]), _rot_columns(p[10])]
    return jnp.concatenate(cols, axis=1).astype(BF16)


def _gate_up_padded(gate_up):
    z = jnp.zeros((128, 128), F32)
    gf = z.at[0:GLA_RANK].set(gate_up[0]).astype(BF16)
    gb = z.at[GLA_RANK:2 * GLA_RANK].set(gate_up[1]).astype(BF16)
    return gf, gb


def kernel(x, c, ctx, c_ctx, ada_w, ada_b, norm1_g, norm2_g, w_in, gla_gate_up, gla_gate_b, gla_norm_g, nat_rpb,
           diff_lambda, diff_norm_g, w_out, peer_wq, peer_subkeys, peer_u, peer_v, final_g):
    nb = x.shape[0]
    depth = ada_w.shape[0]
    assert x.shape[1:] == (SEQ, D) and ctx.shape[1:] == (CTX, D) and (nb * T) % PEER_TB == 0 and (nb * SEQ) % PEER_TB == 0

    rows = -(-(nb + 1) // 8) * 8
    cc = jnp.concatenate([c, c_ctx[None], jnp.zeros((rows - nb - 1, D), F32)], axis=0)
    mod_all = _modulation(cc, ada_w, ada_b)
    cos, sin = _rope_tables()

    xa = jnp.concatenate([ctx, x], axis=1)
    peer = None
    mod_prev = None
    for l in range(depth):
        lam_init = 0.8 - 0.6 * math.exp(-0.3 * l)
        mod = mod_all[l].reshape(rows, 1, 6 * D)
        gf, gb = _gate_up_padded(gla_gate_up[l])
        xa, (gq, gk, ggf, ggb, gv, gr, nq, nk, nv, dq, dk, dv) = _norm_proj(
            xa, peer, mod_prev, mod, norm1_g[l][None], _regroup_w_in(w_in[l]), gf, gb, gla_gate_b[l], cos, sin)
        yg = _gla(gq, gk, ggf, ggb, gv, gr, gla_norm_g[l][None])
        need_ctx = l + 1 < depth
        yn = _nat(nq, nk, nv, _nat_bias_table(nat_rpb[l]), need_ctx)
        yd = _diff(dq, dk, dv, diff_lambda[l], diff_norm_g[l][None], lam_init, need_ctx)
        xa, ht = _out_proj(xa, yg, yn, yd, mod, w_out[l].astype(BF16), norm2_g[l][None], need_ctx)
        peer = _peer(ht, peer_wq[l].T.astype(BF16), peer_subkeys[l],
                     _to_f8(peer_u[l] * PEER_U_SCALE), _to_f8(peer_v[l].T)).reshape(xa.shape)
        mod_prev = mod
    return _final(xa, peer, mod_prev, final_g[None])
```

```python
import functools
import math

import numpy as np
import jax
import jax.numpy as jnp
from jax import lax
from jax.experimental import pallas as pl
from jax.experimental.pallas import tpu as pltpu

F32 = jnp.float32
BF16 = jnp.bfloat16
F8 = jnp.float8_e4m3fn
F8_MAX = 448.0
HIGHEST = lax.Precision.HIGHEST

D = 1024
SEQ = 2048
CTX = 256
T = SEQ + CTX
GRID_W = 64
GRID_ROWS = SEQ // GRID_W
EPS = 1e-6
TM = 256
NT_TILES = T // TM

GLA_HEADS, GLA_DK, GLA_DV, GLA_RANK, GLA_TAU, GLA_CHUNK = 4, 32, 64, 16, 16.0, 64
NAT_HEADS, NAT_DH, NAT_WIN_R, NAT_WIN_C = 4, 64, 8, 16
DIFF_HEADS, DIFF_DH = 4, 64
ROPE_BASE = 10000.0
PEER_HEADS, PEER_NKEYS, PEER_HALF, PEER_TOPK = 8, 128, 64, 16
PEER_EXPERTS = PEER_NKEYS * PEER_NKEYS
PEER_TB = 512
PEER_ROWS = 16
PEER_EB = PEER_ROWS * PEER_NKEYS
PEER_U_SCALE = 32.0
PEER_W_SCALE = 4.0

C_GQ, C_GK, C_GV, C_GR, C_GD = 0, 128, 256, 512, 768
C_NQ, C_NK, C_NV = 896, 1152, 1408
C_DQ, C_DK, C_DV, C_DQR, C_DKR = 1664, 2176, 2688, 3200, 3712
PROJ_COLS = 4224

_NT_DIMS = (((1,), (1,)), ((), ()))


def _nt(a, b):
    return lax.dot_general(a, b, _NT_DIMS, preferred_element_type=F32)


def _params(*sem):
    return pltpu.CompilerParams(dimension_semantics=sem, vmem_limit_bytes=56 * 1024 * 1024)


def _to_f8(x):
    return jnp.clip(x, -F8_MAX, F8_MAX).astype(F8)


def _rms(x):
    return x * lax.rsqrt(jnp.mean(x * x, axis=-1, keepdims=True) + EPS)


def _sigmoid(x):
    return 1.0 / (1.0 + jnp.exp(-x))


def _log_sigmoid(x):
    return jnp.minimum(x, 0.0) - jnp.log1p(jnp.exp(-jnp.abs(x)))


def _mod_kernel(c_ref, w_ref, b_ref, o_ref):
    c = c_ref[...]
    o_ref[0] = jnp.dot(c * _sigmoid(c), w_ref[0], precision=HIGHEST, preferred_element_type=F32) + b_ref[0]


def _modulation(cc, ada_w, ada_b):
    depth = ada_w.shape[0]
    rows = cc.shape[0]
    return pl.pallas_call(
        _mod_kernel,
        grid=(depth, 6),
        in_specs=[pl.BlockSpec((rows, D), lambda l, j: (0, 0)),
                  pl.BlockSpec((1, D, D), lambda l, j: (l, 0, j)),
                  pl.BlockSpec((1, 1, D), lambda l, j: (l, 0, j))],
        out_specs=pl.BlockSpec((1, rows, D), lambda l, j: (l, 0, j)),
        out_shape=jax.ShapeDtypeStruct((depth, rows, 6 * D), F32),
        compiler_params=_params("arbitrary", "arbitrary"),
        name="adaln_mod",
    )(cc, ada_w, ada_b.reshape(depth, 1, 6 * D))


def _proj_kernel(*refs, fuse_res):
    if fuse_res:
        x_ref, pr_ref, modp_ref = refs[:3]
        refs = refs[3:]
    else:
        x_ref = refs[0]
        refs = refs[1:]
    (mod_ref, g_ref, w_ref, gf_ref, gb_ref, gbias_ref, cos_ref, sin_ref) = refs[:8]
    outs = refs[8:]
    if fuse_res:
        xo_ref = outs[0]
        outs = outs[1:]
    (gq_ref, gk_ref, ggf_ref, ggb_ref, gv_ref, gr_ref, nq_ref, nk_ref, nv_ref, dq_ref, dk_ref, dv_ref) = outs

    x = x_ref[0]
    if fuse_res:
        x = x + modp_ref[0][:, 5 * D:6 * D] * pr_ref[0]
        xo_ref[0] = x
    m = mod_ref[0]
    h = _rms(x) * g_ref[...] * (1.0 + m[:, D:2 * D]) + m[:, 0:D]
    p = jnp.dot(h.astype(BF16), w_ref[...], preferred_element_type=F32)

    gq_ref[0] = p[:, C_GQ:C_GQ + 128] * (GLA_DK ** -0.5)
    gk_ref[0] = p[:, C_GK:C_GK + 128]
    gv_ref[0] = p[:, C_GV:C_GV + 256]
    gr_ref[0] = p[:, C_GR:C_GR + 256]
    pd = p[:, C_GD:C_GD + 128].astype(BF16)
    gbias = gbias_ref[...]
    xf = jnp.dot(pd, gf_ref[...], preferred_element_type=F32) + gbias[0:1]
    xb = jnp.dot(pd, gb_ref[...], preferred_element_type=F32) + gbias[1:2]
    ggf_ref[0] = _log_sigmoid(xf) / GLA_TAU
    ggb_ref[0] = _log_sigmoid(xb) / GLA_TAU

    nq_ref[0] = (p[:, C_NQ:C_NQ + 256] * (NAT_DH ** -0.5)).astype(BF16)
    nk_ref[0] = p[:, C_NK:C_NK + 256].astype(BF16)
    nv_ref[0] = p[:, C_NV:C_NV + 256].astype(BF16)

    cos = cos_ref[...]
    sin = sin_ref[...]
    dq = p[:, C_DQ:C_DQ + 512] * cos + p[:, C_DQR:C_DQR + 512] * sin
    dq_ref[0] = (dq * (DIFF_DH ** -0.5)).astype(BF16)
    dk_ref[0] = (p[:, C_DK:C_DK + 512] * cos + p[:, C_DKR:C_DKR + 512] * sin).astype(BF16)
    dv_ref[0] = p[:, C_DV:C_DV + 512].astype(BF16)


def _mod_spec(nb, t0=0):
    return pl.BlockSpec((1, 1, 6 * D), lambda b, t: (jnp.where(t + t0 == 0, nb, b), 0, 0))


def _norm_proj(x, peer, mod_prev, mod, g1, w_all, gf, gb, gbias, cos, sin):
    nb = x.shape[0]
    fuse = peer is not None
    tile = lambda w: pl.BlockSpec((1, TM, w), lambda b, t: (b, t, 0))
    full = lambda a: pl.BlockSpec(a.shape, lambda b, t: (0,) * a.ndim)
    ins, specs = [x], [tile(D)]
    if fuse:
        ins += [peer, mod_prev]
        specs += [tile(D), _mod_spec(nb)]
    ins += [mod, g1, w_all, gf, gb, gbias, cos, sin]
    specs += [_mod_spec(nb), full(g1), full(w_all), full(gf), full(gb), full(gbias),
              pl.BlockSpec((TM, 512), lambda b, t: (t, 0)), pl.BlockSpec((TM, 512), lambda b, t: (t, 0))]
    widths = [(128, F32)] * 4 + [(256, F32)] * 2 + [(256, BF16)] * 3 + [(512, BF16)] * 3
    out_shape = [jax.ShapeDtypeStruct((nb, T, w), dt) for w, dt in widths]
    out_specs = [tile(w) for w, _ in widths]
    if fuse:
        out_shape = [jax.ShapeDtypeStruct((nb, T, D), F32)] + out_shape
        out_specs = [tile(D)] + out_specs
    res = pl.pallas_call(
        functools.partial(_proj_kernel, fuse_res=fuse),
        grid=(nb, NT_TILES),
        in_specs=specs, out_specs=out_specs, out_shape=out_shape,
        compiler_params=_params("arbitrary", "arbitrary"),
        name="norm_proj",
    )(*ins)
    if fuse:
        return res[0], res[1:]
    return x, res


def _gla_kernel(q_ref, k_ref, gf_ref, gb_ref, v_ref, r_ref, ng_ref, o_ref, of_s, ob_s, sf_s, sb_s):
    C = GLA_CHUNK
    nch = T // C
    nctx = CTX // C
    ii = lax.broadcasted_iota(jnp.int32, (C, C), 0)
    jj = lax.broadcasted_iota(jnp.int32, (C, C), 1)
    tri_f = (jj <= ii).astype(F32)
    tri_b = (jj >= ii).astype(F32)
    it = lax.broadcasted_iota(jnp.int32, (C, 4 * C), 0)
    jt = lax.broadcasted_iota(jnp.int32, (C, 4 * C), 1) & (C - 1)
    cm_f = jt <= it
    cm_b = jt >= it
    kmask = ((lax.broadcasted_iota(jnp.int32, (4 * C, 128), 0) >> 6)
             == (lax.broadcasted_iota(jnp.int32, (4 * C, 128), 1) >> 5)).astype(F32)
    vmask = ((lax.broadcasted_iota(jnp.int32, (4 * C, 256), 0) >> 6)
             == (lax.broadcasted_iota(jnp.int32, (4 * C, 256), 1) >> 6)).astype(F32)
    smask = ((lax.broadcasted_iota(jnp.int32, (256, 128), 0) >> 6)
             == (lax.broadcasted_iota(jnp.int32, (256, 128), 1) >> 5)).astype(F32)

    sf_s[...] = jnp.zeros_like(sf_s)
    sb_s[...] = jnp.zeros_like(sb_s)

    def chunk(off, g_ref, s_ref, dst, tri, cm, last, mid):
        rows = pl.ds(off, C)
        q = q_ref[0, rows, :]
        k = k_ref[0, rows, :]
        v = v_ref[0, rows, :]
        g = g_ref[0, rows, :]
        st = s_ref[...]
        b = jnp.dot(tri, g, precision=HIGHEST, preferred_element_type=F32)
        b_last = b[last:last + 1]
        b_mid = b[mid:mid + 1]
        qi = q * jnp.exp(jnp.clip(b - b_mid, -80.0, 80.0))
        ki = k * jnp.exp(jnp.clip(b_mid - b, -80.0, 80.0))
        krows = jnp.concatenate([ki, ki, ki, ki], axis=0) * kmask
        a = jnp.where(cm, _nt(qi.astype(BF16), krows.astype(BF16)), 0.0)
        vblk = jnp.concatenate([v, v, v, v], axis=0) * vmask
        o = jnp.dot(a.astype(BF16), vblk.astype(BF16), preferred_element_type=F32)
        o = o + _nt((q * jnp.exp(b)).astype(BF16), st.astype(BF16))
        dst[rows, :] = o
        kd = k * jnp.exp(b_last - b)
        upd = jnp.dot(v.T.astype(BF16), kd.astype(BF16), preferred_element_type=F32)
        s_ref[...] = st * jnp.exp(b_last) + upd * smask

    def body(t, carry):
        cb = jnp.where(t < nctx, nctx - 1 - t, nch + nctx - 1 - t)
        chunk(pl.multiple_of(t * C, C), gf_ref, sf_s, of_s, tri_f, cm_f, C - 1, C // 2 - 1)
        chunk(pl.multiple_of(cb * C, C), gb_ref, sb_s, ob_s, tri_b, cm_b, 0, C // 2)
        return carry

    lax.fori_loop(0, nch, body, 0)

    hones = ((lax.broadcasted_iota(jnp.int32, (256, 256), 0) >> 6)
             == (lax.broadcasted_iota(jnp.int32, (256, 256), 1) >> 6)).astype(F32) * (1.0 / GLA_DV)
    ng = ng_ref[...]
    for t in range(NT_TILES):
        rows = pl.ds(t * TM, TM)
        o = of_s[rows, :] + ob_s[rows, :]
        ms = jnp.dot(o * o, hones, precision=HIGHEST, preferred_element_type=F32)
        r = r_ref[0, rows, :]
        o_ref[0, rows, :] = (o * lax.rsqrt(ms + EPS) * ng * (r * _sigmoid(r))).astype(BF16)


def _gla(gq, gk, ggf, ggb, gv, gr, ng):
    nb = gq.shape[0]
    seq = lambda w: pl.BlockSpec((1, T, w), lambda b: (b, 0, 0))
    return pl.pallas_call(
        _gla_kernel,
        grid=(nb,),
        in_specs=[seq(128), seq(128), seq(128), seq(128), seq(256), seq(256),
                  pl.BlockSpec((1, 256), lambda b: (0, 0))],
        out_specs=seq(256),
        out_shape=jax.ShapeDtypeStruct((nb, T, 256), BF16),
        scratch_shapes=[pltpu.VMEM((T, 256), F32), pltpu.VMEM((T, 256), F32),
                        pltpu.VMEM((256, 128), F32), pltpu.VMEM((256, 128), F32)],
        compiler_params=_params("arbitrary"),
        name="gla",
    )(gq, gk, ggf, ggb, gv, gr, ng)


def _stacked_heads(q, n):
    rows = lax.broadcasted_iota(jnp.int32, (NAT_HEADS * n, 256), 0)
    lanes = lax.broadcasted_iota(jnp.int32, (NAT_HEADS * n, 256), 1) >> 6
    head = jnp.zeros_like(rows)
    for h in range(1, NAT_HEADS):
        head = head + (rows >= h * n).astype(jnp.int32)
    own = head == lanes
    q4 = jnp.concatenate([q] * NAT_HEADS, axis=0)
    return jnp.where(own, q4, jnp.zeros_like(q4)), own


def _unstack_heads(pv, own, n):
    pv = jnp.where(own, pv, 0.0)
    out = pv[0:n]
    for h in range(1, NAT_HEADS):
        out = out + pv[h * n:(h + 1) * n]
    return out


def _nat_kernel(q_ref, k_ref, v_ref, bias_ref, o_ref, *, need_ctx):
    out0 = CTX if need_ctx else 0
    if need_ctx:
        kc = k_ref[0, 0:CTX, :]
        vc = v_ref[0, 0:CTX, :]
        q4, own = _stacked_heads(q_ref[0, 0:CTX, :], CTX)
        s = _nt(q4, kc)
        e = jnp.exp(s - jnp.max(s, axis=-1, keepdims=True))
        p = (e * (1.0 / jnp.sum(e, axis=-1, keepdims=True))).astype(BF16)
        o_ref[0, 0:CTX, :] = _unstack_heads(jnp.dot(p, vc, preferred_element_type=F32), own, CTX).astype(BF16)

    nwin = NAT_WIN_R * GRID_W

    def row(r, carry):
        rs = jnp.clip(r - NAT_WIN_R // 2, 0, GRID_ROWS - NAT_WIN_R)
        case = rs - r + NAT_WIN_R - 1
        qrows = pl.ds(pl.multiple_of(CTX + r * GRID_W, GRID_W), GRID_W)
        wrows = pl.ds(pl.multiple_of(CTX + rs * GRID_W, GRID_W), nwin)
        q4, own = _stacked_heads(q_ref[0, qrows, :], GRID_W)
        sw = _nt(q4, k_ref[0, wrows, :]) + bias_ref[case]
        sc = _nt(q4, k_ref[0, 0:CTX, :])
        m = jnp.maximum(jnp.max(sw, axis=-1, keepdims=True), jnp.max(sc, axis=-1, keepdims=True))
        ew = jnp.exp(sw - m)
        ec = jnp.exp(sc - m)
        inv = 1.0 / (jnp.sum(ew, axis=-1, keepdims=True) + jnp.sum(ec, axis=-1, keepdims=True))
        pv = jnp.dot((ew * inv).astype(BF16), v_ref[0, wrows, :], preferred_element_type=F32)
        pv = pv + jnp.dot((ec * inv).astype(BF16), v_ref[0, 0:CTX, :], preferred_element_type=F32)
        orows = pl.ds(pl.multiple_of(out0 + r * GRID_W, GRID_W), GRID_W)
        o_ref[0, orows, :] = _unstack_heads(pv, own, GRID_W).astype(BF16)
        return carry

    lax.fori_loop(0, GRID_ROWS, row, 0)


def _nat(nq, nk, nv, bias, need_ctx):
    nb = nq.shape[0]
    rows_out = T if need_ctx else SEQ
    seq = pl.BlockSpec((1, T, 256), lambda b: (b, 0, 0))
    return pl.pallas_call(
        functools.partial(_nat_kernel, need_ctx=need_ctx),
        grid=(nb,),
        in_specs=[seq, seq, seq, pl.BlockSpec(bias.shape, lambda b: (0, 0, 0))],
        out_specs=pl.BlockSpec((1, rows_out, 256), lambda b: (b, 0, 0)),
        out_shape=jax.ShapeDtypeStruct((nb, rows_out, 256), BF16),
        compiler_params=_params("arbitrary"),
        name="nat",
    )(nq, nk, nv, bias)


def _nat_bias_table(rpb):
    cidx = np.arange(GRID_W)
    col_start = np.clip(cidx - NAT_WIN_C // 2, 0, GRID_W - NAT_WIN_C)
    col_mask = (cidx[None, :] >= col_start[:, None]) & (cidx[None, :] < col_start[:, None] + NAT_WIN_C)
    dc = np.clip(cidx[None, :] - cidx[:, None] + NAT_WIN_C - 1, 0, 2 * NAT_WIN_C - 2)
    dr = np.arange(NAT_WIN_R)[:, None] + np.arange(NAT_WIN_R)[None, :]
    pick_r = np.eye(2 * NAT_WIN_R - 1, dtype=np.float32)[dr]
    pick_c = np.eye(2 * NAT_WIN_C - 1, dtype=np.float32)[dc]
    t = jnp.einsum('hab,cja,qkb->hcjqk', rpb.astype(F32), pick_r, pick_c, precision=HIGHEST)
    t = jnp.where(col_mask[None, None, None], t, -jnp.inf)
    return t.transpose(1, 0, 3, 2, 4).reshape(NAT_WIN_R, NAT_HEADS * GRID_W, NAT_WIN_R * GRID_W)


def _diff_kernel(lam_ref, q_ref, k_ref, v_ref, g_ref, o_ref, *, lam_init, need_ctx):
    lv = lam_ref[...]
    lam = (jnp.exp(jnp.sum(lv[0:1] * lv[1:2], axis=-1, keepdims=True))
           - jnp.exp(jnp.sum(lv[2:3] * lv[3:4], axis=-1, keepdims=True)) + lam_init)
    q = q_ref[0]
    first = lax.broadcasted_iota(jnp.int32, (1, 2 * DIFF_DH), 1) < DIFF_DH
    q1 = jnp.where(first, q, jnp.zeros_like(q))
    q2 = jnp.where(first, jnp.zeros_like(q), q)

    def attend(k, v):
        s1 = _nt(q1, k)
        s2 = _nt(q2, k)
        e1 = jnp.exp(s1 - jnp.max(s1, axis=-1, keepdims=True))
        e2 = jnp.exp(s2 - jnp.max(s2, axis=-1, keepdims=True))
        w1 = 1.0 / jnp.sum(e1, axis=-1, keepdims=True)
        w2 = lam / jnp.sum(e2, axis=-1, keepdims=True)
        p = (e1 * w1 - e2 * w2).astype(BF16)
        o = jnp.dot(p, v, preferred_element_type=F32)
        o_ref[0] = (_rms(o) * g_ref[...] * (1.0 - lam_init)).astype(BF16)

    if need_ctx:
        @pl.when(pl.program_id(2) == 0)
        def _():
            attend(k_ref[0, 0:CTX, :], v_ref[0, 0:CTX, :])

        @pl.when(pl.program_id(2) > 0)
        def _():
            attend(k_ref[0], v_ref[0])
    else:
        attend(k_ref[0], v_ref[0])


def _diff(dq, dk, dv, lam_vec, dg, lam_init, need_ctx):
    nb = dq.shape[0]
    w = 2 * DIFF_DH
    t0 = 0 if need_ctx else 1
    return pl.pallas_call(
        functools.partial(_diff_kernel, lam_init=lam_init, need_ctx=need_ctx),
        grid=(nb, DIFF_HEADS, NT_TILES - t0),
        in_specs=[pl.BlockSpec((4, DIFF_DH), lambda b, h, t: (0, 0)),
                  pl.BlockSpec((1, TM, w), lambda b, h, t: (b, t + t0, h)),
                  pl.BlockSpec((1, T, w), lambda b, h, t: (b, 0, h)),
                  pl.BlockSpec((1, T, w), lambda b, h, t: (b, 0, h)),
                  pl.BlockSpec((1, w), lambda b, h, t: (0, h))],
        out_specs=pl.BlockSpec((1, TM, w), lambda b, h, t: (b, t, h)),
        out_shape=jax.ShapeDtypeStruct((nb, T - t0 * TM, DIFF_HEADS * w), BF16),
        compiler_params=_params("arbitrary", "arbitrary", "arbitrary"),
        name="diff_attn",
    )(lam_vec, dq, dk, dv, dg)


def _out_kernel(x_ref, yg_ref, yn_ref, yd_ref, mod_ref, w_ref, g_ref, xo_ref, ht_ref):
    attn = jnp.dot(yg_ref[0], w_ref[0:256, :], preferred_element_type=F32)
    attn = attn + jnp.dot(yn_ref[0], w_ref[256:512, :], preferred_element_type=F32)
    attn = attn + jnp.dot(yd_ref[0], w_ref[512:1024, :], preferred_element_type=F32)
    m = mod_ref[0]
    x = x_ref[0] + m[:, 2 * D:3 * D] * attn
    xo_ref[0] = x
    h2 = _rms(x) * g_ref[...] * (1.0 + m[:, 4 * D:5 * D]) + m[:, 3 * D:4 * D]
    ht_ref[...] = h2.T.astype(BF16)


def _out_proj(x, yg, yn, yd, mod, w_out, g2, need_ctx):
    nb = x.shape[0]
    t0 = 0 if need_ctx else 1
    nt = NT_TILES - t0
    full = lambda w: pl.BlockSpec((1, TM, w), lambda b, t: (b, t + t0, 0))
    tile = lambda w: pl.BlockSpec((1, TM, w), lambda b, t: (b, t, 0))
    return pl.pallas_call(
        _out_kernel,
        grid=(nb, nt),
        in_specs=[full(D), full(256), tile(256), tile(512), _mod_spec(nb, t0),
                  pl.BlockSpec((D, D), lambda b, t: (0, 0)), pl.BlockSpec((1, D), lambda b, t: (0, 0))],
        out_specs=[tile(D), pl.BlockSpec((D, TM), lambda b, t: (0, b * nt + t))],
        out_shape=[jax.ShapeDtypeStruct((nb, nt * TM, D), F32), jax.ShapeDtypeStruct((D, nb * nt * TM), BF16)],
        compiler_params=_params("arbitrary", "arbitrary"),
        name="out_proj",
    )(x, yg, yn, yd, mod, w_out, g2)


def _oddeven_pairs(n):
    pairs = []

    def merge(lo, hi, r):
        step = r * 2
        if step < hi - lo:
            merge(lo, hi, step)
            merge(lo + r, hi, step)
            pairs.extend((i, i + r) for i in range(lo + r, hi - r, step))
        else:
            pairs.append((lo, lo + r))

    def sort(lo, hi):
        if hi - lo >= 1:
            mid = lo + (hi - lo) // 2
            sort(lo, mid)
            sort(mid + 1, hi)
            merge(lo, hi, 1)

    sort(0, n - 1)
    return pairs


def _bitonic_pairs(n):
    pairs = []
    d = n // 2
    while d >= 1:
        pairs.extend((i, i + d) for i in range(n) if (i // d) % 2 == 0)
        d //= 2
    return pairs


_SORT16 = _oddeven_pairs(16)
_BITONIC16 = _bitonic_pairs(16)


def _compare_exchange(xs, pairs):
    for i, j in pairs:
        a, b = xs[i], xs[j]
        if b is None:
            continue
        if a is None:
            xs[i], xs[j] = b, None
        else:
            xs[i], xs[j] = jnp.maximum(a, b), jnp.minimum(a, b)


def _merge_top16(xs, ys):
    zs = []
    for k in range(16):
        a, b = xs[k], ys[15 - k]
        zs.append(a if b is None else (b if a is None else jnp.maximum(a, b)))
    _compare_exchange(zs, _BITONIC16)
    return zs


def _top16_of_128(x3):
    xs = [x3[r] for r in range(16)]
    _compare_exchange(xs, _SORT16)
    for shift in (4, 2, 1):
        xs = _merge_top16(xs, [pltpu.roll(x, shift, 0) for x in xs])
    return xs


def _peer_select(s1, s2):
    n = s1.shape[-1]
    a3 = s1.reshape(16, 8, n)
    b3 = s2.reshape(16, 8, n)
    t1 = _top16_of_128(a3)
    t2 = _top16_of_128(b3)
    cand = [[t1[a] + t2[b] for b in range(PEER_TOPK // (a + 1))] for a in range(PEER_TOPK)]
    pad = lambda xs: xs + [None] * (16 - len(xs))
    top = cand[0]
    for a in range(1, 8):
        top = _merge_top16(top, pad(cand[a]))
    top = _merge_top16(top, pad([cand[a][0] for a in range(8, 16)]))
    tau = top[15]
    cmax = cand[0][0]
    inf = jnp.full_like(tau, jnp.inf)
    z = jnp.zeros_like(tau)
    thr = []
    for a in range(PEER_TOPK):
        th = inf
        for b, cv in enumerate(cand[a]):
            sel = cv >= tau
            th = jnp.where(sel, jnp.minimum(th, t2[b]), th)
            z = z + jnp.where(sel, jnp.exp(cv - cmax), 0.0)
        thr.append(th)
    theta = jnp.full_like(a3, jnp.inf)
    for a in range(PEER_TOPK):
        theta = jnp.where(a3 == t1[a][None], thr[a][None], theta)
    c = jnp.exp(a3 - t1[0][None]) / z[None]
    e2 = jnp.exp(b3 - t2[0][None])
    return theta, c, e2.reshape(PEER_NKEYS, n)


def _peer_kernel(ht_ref, wq_ref, sk_ref, u_ref, vt_ref, o_ref, th_s, c_s, s2_s, e2_s, h8_s, a_s, w_s, acc_s):
    e = pl.program_id(1)
    tb = ht_ref.shape[1]

    @pl.when(e == 0)
    def _():
        qt = jnp.dot(wq_ref[...], ht_ref[...], preferred_element_type=F32)
        sk = sk_ref[...]
        for h in range(PEER_HEADS):
            s1 = jnp.dot(sk[0], qt[h * 128:h * 128 + 64], precision=HIGHEST, preferred_element_type=F32)
            s2 = jnp.dot(sk[1], qt[h * 128 + 64:h * 128 + 128], precision=HIGHEST, preferred_element_type=F32)
            theta, c, e2 = _peer_select(s1, s2)
            th_s[h] = theta
            c_s[h] = c * PEER_W_SCALE
            for lc in range(tb // 128):
                s2_s[h, lc] = s2[:, lc * 128:(lc + 1) * 128]
                e2_s[h, lc] = e2[:, lc * 128:(lc + 1) * 128]
        acc_s[...] = jnp.zeros_like(acc_s)
        h8_s[...] = jnp.clip(ht_ref[...], -F8_MAX, F8_MAX).astype(F8)

    a = jnp.dot(u_ref[...], h8_s[...], preferred_element_type=F32) * (1.0 / PEER_U_SCALE)
    for lc in range(tb // 128):
        a_s[lc] = a[:, lc * 128:(lc + 1) * 128]
    for lc in range(tb // 128):
        lanes = slice(lc * 128, (lc + 1) * 128)
        for i in range(PEER_ROWS):
            g = jnp.zeros((PEER_NKEYS, 128), F32)
            for h in range(PEER_HEADS):
                th = th_s[h, e * (PEER_ROWS // 8) + i // 8, i % 8:i % 8 + 1, lanes]
                cw = c_s[h, e * (PEER_ROWS // 8) + i // 8, i % 8:i % 8 + 1, lanes]
                g = g + jnp.where(s2_s[h, lc] >= th, e2_s[h, lc] * cw, 0.0)
            rows = slice(i * PEER_NKEYS, (i + 1) * PEER_NKEYS)
            w = g * jax.nn.gelu(a_s[lc, rows, :], approximate=True)
            w_s[rows, lanes] = jnp.minimum(w, F8_MAX).astype(F8)
    acc_s[...] += jnp.dot(vt_ref[...], w_s[...], preferred_element_type=F32)

    @pl.when(e == pl.num_programs(1) - 1)
    def _():
        o_ref[...] = acc_s[...].T * (1.0 / PEER_W_SCALE)


def _peer(ht, wq_t, sk, u, v_t):
    n = ht.shape[1]
    tb = PEER_TB
    ne = PEER_EXPERTS // PEER_EB
    sel = pltpu.VMEM((PEER_HEADS, tb // 128, PEER_NKEYS, 128), F32)
    sel8 = pltpu.VMEM((PEER_HEADS, PEER_NKEYS // 8, 8, tb), F32)
    return pl.pallas_call(
        _peer_kernel,
        grid=(n // tb, ne),
        in_specs=[pl.BlockSpec((D, tb), lambda t, e: (0, t)),
                  pl.BlockSpec((D, D), lambda t, e: (0, 0)),
                  pl.BlockSpec((2, PEER_NKEYS, PEER_HALF), lambda t, e: (0, 0, 0)),
                  pl.BlockSpec((PEER_EB, D), lambda t, e: (e, 0)),
                  pl.BlockSpec((D, PEER_EB), lambda t, e: (0, e))],
        out_specs=pl.BlockSpec((tb, D), lambda t, e: (t, 0)),
        out_shape=jax.ShapeDtypeStruct((n, D), F32),
        scratch_shapes=[sel8, sel8, sel, sel, pltpu.VMEM((D, tb), F8),
                        pltpu.VMEM((tb // 128, PEER_EB, 128), F32), pltpu.VMEM((PEER_EB, tb), F8),
                        pltpu.VMEM((D, tb), F32)],
        compiler_params=_params("arbitrary", "arbitrary"),
        name="peer",
    )(ht, wq_t, sk, u, v_t)


def _final_kernel(x_ref, pr_ref, mod_ref, g_ref, o_ref):
    x = x_ref[0] + mod_ref[0][:, 5 * D:6 * D] * pr_ref[0]
    o_ref[0] = _rms(x) * g_ref[...]


def _final(x, peer, mod, g):
    nb = x.shape[0]
    lat = pl.BlockSpec((1, TM, D), lambda b, t: (b, t, 0))
    return pl.pallas_call(
        _final_kernel,
        grid=(nb, SEQ // TM),
        in_specs=[lat, lat, pl.BlockSpec((1, 1, 6 * D), lambda b, t: (b, 0, 0)),
                  pl.BlockSpec((1, D), lambda b, t: (0, 0))],
        out_specs=lat,
        out_shape=jax.ShapeDtypeStruct((nb, SEQ, D), F32),
        compiler_params=_params("arbitrary", "arbitrary"),
        name="final_norm",
    )(x, peer, mod, g)


def _rope_tables():
    half = DIFF_DH // 4
    inv = ROPE_BASE ** (-jnp.arange(half, dtype=F32) / half)
    ang = jnp.arange(GRID_W, dtype=F32)[:, None] * inv[None, :]

    def table(fn, ctx_value):
        small = fn(ang)
        by_row = jnp.broadcast_to(small[:GRID_ROWS, None, :], (GRID_ROWS, GRID_W, half))
        by_col = jnp.broadcast_to(small[None, :, :], (GRID_ROWS, GRID_W, half))
        per_map = jnp.concatenate([by_row, by_row, by_col, by_col], axis=-1)
        lat = jnp.tile(per_map.reshape(SEQ, DIFF_DH), (1, 2 * DIFF_HEADS))
        return jnp.concatenate([jnp.full((CTX, lat.shape[1]), ctx_value, F32), lat], axis=0)

    return table(jnp.cos, 1.0), table(jnp.sin, 0.0)


def _rot_columns(w):
    q = DIFF_DH // 4
    w4 = w.reshape(w.shape[0], -1, 2, q)
    return jnp.stack([-w4[:, :, 1], w4[:, :, 0]], axis=2).reshape(w.shape)


def _regroup_w_in(w):
    sizes = (128, 128, 256, 256, 16, 16, 256, 256, 256, 512, 512, 512)
    pts = np.cumsum((0,) + sizes)
    p = [w[:, pts[i]:pts[i + 1]] for i in range(12)]
    downs = jnp.concatenate([p[4], p[5], jnp.zeros((D, 128 - 2 * GLA_RANK), w.dtype)], axis=1)
    cols = [p[0], p[1], p[2], p[3], downs, p[6], p[7], p[8], p[9], p[10], p[11], _rot_columns(p[9]), _rot_columns(p[10])]
    return jnp.concatenate(cols, axis=1).astype(BF16)


def _gate_up_padded(gate_up):
    z = jnp.zeros((128, 128), F32)
    gf = z.at[0:GLA_RANK].set(gate_up[0]).astype(BF16)
    gb = z.at[GLA_RANK:2 * GLA_RANK].set(gate_up[1]).astype(BF16)
    return gf, gb


def kernel(x, c, ctx, c_ctx, ada_w, ada_b, norm1_g, norm2_g, w_in, gla_gate_up, gla_gate_b, gla_norm_g, nat_rpb,
           diff_lambda, diff_norm_g, w_out, peer_wq, peer_subkeys, peer_u, peer_v, final_g):
    nb = x.shape[0]
    depth = ada_w.shape[0]
    assert x.shape[1:] == (SEQ, D) and ctx.shape[1:] == (CTX, D) and (nb * T) % PEER_TB == 0 and (nb * SEQ) % PEER_TB == 0

    rows = -(-(nb + 1) // 8) * 8
    cc = jnp.concatenate([c, c_ctx[None], jnp.zeros((rows - nb - 1, D), F32)], axis=0)
    mod_all = _modulation(cc, ada_w, ada_b)
    cos, sin = _rope_tables()

    xa = jnp.concatenate([ctx, x], axis=1)
    peer = None
    mod_prev = None
    for l in range(depth):
        lam_init = 0.8 - 0.6 * math.exp(-0.3 * l)
        mod = mod_all[l].reshape(rows, 1, 6 * D)
        gf, gb = _gate_up_padded(gla_gate_up[l])
        xa, (gq, gk, ggf, ggb, gv, gr, nq, nk, nv, dq, dk, dv) = _norm_proj(
            xa, peer, mod_prev, mod, norm1_g[l][None], _regroup_w_in(w_in[l]), gf, gb, gla_gate_b[l], cos, sin)
        yg = _gla(gq, gk, ggf, ggb, gv, gr, gla_norm_g[l][None])
        need_ctx = l + 1 < depth
        yn = _nat(nq, nk, nv, _nat_bias_table(nat_rpb[l]), need_ctx)
        yd = _diff(dq, dk, dv, diff_lambda[l], diff_norm_g[l][None], lam_init, need_ctx)
        xa, ht = _out_proj(xa, yg, yn, yd, mod, w_out[l].astype(BF16), norm2_g[l][None], need_ctx)
        peer = _peer(ht, peer_wq[l].T.astype(BF16), peer_subkeys[l],
                     _to_f8(peer_u[l] * PEER_U_SCALE), _to_f8(peer_v[l].T)).reshape(xa.shape)
        mod_prev = mod
    return _final(xa, peer, mod_prev, final_g[None])
```

```python
import functools
import math

import numpy as np
import jax
import jax.numpy as jnp
from jax import lax
from jax.experimental import pallas as pl
from jax.experimental.pallas import tpu as pltpu

F32 = jnp.float32
BF16 = jnp.bfloat16
F8 = jnp.float8_e4m3fn
F8_MAX = 448.0
HIGHEST = lax.Precision.HIGHEST

D = 1024
SEQ = 2048
CTX = 256
T = SEQ + CTX
GRID_W = 64
GRID_ROWS = SEQ // GRID_W
EPS = 1e-6
TM = 256
NT_TILES = T // TM

GLA_HEADS, GLA_DK, GLA_DV, GLA_RANK, GLA_TAU, GLA_CHUNK = 4, 32, 64, 16, 16.0, 64
NAT_HEADS, NAT_DH, NAT_WIN_R, NAT_WIN_C = 4, 64, 8, 16
DIFF_HEADS, DIFF_DH = 4, 64
ROPE_BASE = 10000.0
PEER_HEADS, PEER_NKEYS, PEER_HALF, PEER_TOPK = 8, 128, 64, 16
PEER_EXPERTS = PEER_NKEYS * PEER_NKEYS
PEER_TB = 512
PEER_ROWS = 16
PEER_EB = PEER_ROWS * PEER_NKEYS
PEER_U_SCALE = 32.0
PEER_W_SCALE = 4.0

C_GQ, C_GK, C_GV, C_GR, C_GD = 0, 128, 256, 512, 768
C_NQ, C_NK, C_NV = 896, 1152, 1408
C_DQ, C_DK, C_DV, C_DQR, C_DKR = 1664, 2176, 2688, 3200, 3712
PROJ_COLS = 4224

_NT_DIMS = (((1,), (1,)), ((), ()))


def _nt(a, b):
    return lax.dot_general(a, b, _NT_DIMS, preferred_element_type=F32)


def _params(*sem):
    return pltpu.CompilerParams(dimension_semantics=sem, vmem_limit_bytes=56 * 1024 * 1024)


def _to_f8(x):
    return jnp.clip(x, -F8_MAX, F8_MAX).astype(F8)


def _rms(x):
    return x * lax.rsqrt(jnp.mean(x * x, axis=-1, keepdims=True) + EPS)


def _sigmoid(x):
    return 1.0 / (1.0 + jnp.exp(-x))


def _log_sigmoid(x):
    return jnp.minimum(x, 0.0) - jnp.log1p(jnp.exp(-jnp.abs(x)))


def _mod_kernel(c_ref, w_ref, b_ref, o_ref):
    c = c_ref[...]
    o_ref[0] = jnp.dot(c * _sigmoid(c), w_ref[0], precision=HIGHEST, preferred_element_type=F32) + b_ref[0]


def _modulation(cc, ada_w, ada_b):
    depth = ada_w.shape[0]
    rows = cc.shape[0]
    return pl.pallas_call(
        _mod_kernel,
        grid=(depth, 6),
        in_specs=[pl.BlockSpec((rows, D), lambda l, j: (0, 0)),
                  pl.BlockSpec((1, D, D), lambda l, j: (l, 0, j)),
                  pl.BlockSpec((1, 1, D), lambda l, j: (l, 0, j))],
        out_specs=pl.BlockSpec((1, rows, D), lambda l, j: (l, 0, j)),
        out_shape=jax.ShapeDtypeStruct((depth, rows, 6 * D), F32),
        compiler_params=_params("arbitrary", "arbitrary"),
        name="adaln_mod",
    )(cc, ada_w, ada_b.reshape(depth, 1, 6 * D))


def _proj_kernel(*refs, fuse_res):
    if fuse_res:
        x_ref, pr_ref, modp_ref = refs[:3]
        refs = refs[3:]
    else:
        x_ref = refs[0]
        refs = refs[1:]
    (mod_ref, g_ref, w_ref, gf_ref, gb_ref, gbias_ref, cos_ref, sin_ref) = refs[:8]
    outs = refs[8:]
    if fuse_res:
        xo_ref = outs[0]
        outs = outs[1:]
    (gq_ref, gk_ref, ggf_ref, ggb_ref, gv_ref, gr_ref, nq_ref, nk_ref, nv_ref, dq_ref, dk_ref, dv_ref) = outs

    x = x_ref[0]
    if fuse_res:
        x = x + modp_ref[0][:, 5 * D:6 * D] * pr_ref[0]
        xo_ref[0] = x
    m = mod_ref[0]
    h = _rms(x) * g_ref[...] * (1.0 + m[:, D:2 * D]) + m[:, 0:D]
    p = jnp.dot(h.astype(BF16), w_ref[...], preferred_element_type=F32)

    gq_ref[0] = p[:, C_GQ:C_GQ + 128] * (GLA_DK ** -0.5)
    gk_ref[0] = p[:, C_GK:C_GK + 128]
    gv_ref[0] = p[:, C_GV:C_GV + 256]
    gr_ref[0] = p[:, C_GR:C_GR + 256]
    pd = p[:, C_GD:C_GD + 128].astype(BF16)
    gbias = gbias_ref[...]
    xf = jnp.dot(pd, gf_ref[...], preferred_element_type=F32) + gbias[0:1]
    xb = jnp.dot(pd, gb_ref[...], preferred_element_type=F32) + gbias[1:2]
    ggf_ref[0] = _log_sigmoid(xf) / GLA_TAU
    ggb_ref[0] = _log_sigmoid(xb) / GLA_TAU

    nq_ref[0] = (p[:, C_NQ:C_NQ + 256] * (NAT_DH ** -0.5)).astype(BF16)
    nk_ref[0] = p[:, C_NK:C_NK + 256].astype(BF16)
    nv_ref[0] = p[:, C_NV:C_NV + 256].astype(BF16)

    cos = cos_ref[...]
    sin = sin_ref[...]
    dq = p[:, C_DQ:C_DQ + 512] * cos + p[:, C_DQR:C_DQR + 512] * sin
    dq_ref[0] = (dq * (DIFF_DH ** -0.5)).astype(BF16)
    dk_ref[0] = (p[:, C_DK:C_DK + 512] * cos + p[:, C_DKR:C_DKR + 512] * sin).astype(BF16)
    dv_ref[0] = p[:, C_DV:C_DV + 512].astype(BF16)


def _mod_spec(nb, t0=0):
    return pl.BlockSpec((1, 1, 6 * D), lambda b, t: (jnp.where(t + t0 == 0, nb, b), 0, 0))


def _norm_proj(x, peer, mod_prev, mod, g1, w_all, gf, gb, gbias, cos, sin):
    nb = x.shape[0]
    fuse = peer is not None
    tile = lambda w: pl.BlockSpec((1, TM, w), lambda b, t: (b, t, 0))
    full = lambda a: pl.BlockSpec(a.shape, lambda b, t: (0,) * a.ndim)
    ins, specs = [x], [tile(D)]
    if fuse:
        ins += [peer, mod_prev]
        specs += [tile(D), _mod_spec(nb)]
    ins += [mod, g1, w_all, gf, gb, gbias, cos, sin]
    specs += [_mod_spec(nb), full(g1), full(w_all), full(gf), full(gb), full(gbias),
              pl.BlockSpec((TM, 512), lambda b, t: (t, 0)), pl.BlockSpec((TM, 512), lambda b, t: (t, 0))]
    widths = [(128, F32)] * 4 + [(256, F32)] * 2 + [(256, BF16)] * 3 + [(512, BF16)] * 3
    out_shape = [jax.ShapeDtypeStruct((nb, T, w), dt) for w, dt in widths]
    out_specs = [tile(w) for w, _ in widths]
    if fuse:
        out_shape = [jax.ShapeDtypeStruct((nb, T, D), F32)] + out_shape
        out_specs = [tile(D)] + out_specs
    res = pl.pallas_call(
        functools.partial(_proj_kernel, fuse_res=fuse),
        grid=(nb, NT_TILES),
        in_specs=specs, out_specs=out_specs, out_shape=out_shape,
        compiler_params=_params("arbitrary", "arbitrary"),
        name="norm_proj",
    )(*ins)
    if fuse:
        return res[0], res[1:]
    return x, res


def _gla_kernel(q_ref, k_ref, gf_ref, gb_ref, v_ref, r_ref, ng_ref, o_ref, of_s, ob_s, sf_s, sb_s):
    C = GLA_CHUNK
    nch = T // C
    nctx = CTX // C
    ii = lax.broadcasted_iota(jnp.int32, (C, C), 0)
    jj = lax.broadcasted_iota(jnp.int32, (C, C), 1)
    tri_f = (jj <= ii).astype(F32)
    tri_b = (jj >= ii).astype(F32)
    it = lax.broadcasted_iota(jnp.int32, (C, 4 * C), 0)
    jt = lax.broadcasted_iota(jnp.int32, (C, 4 * C), 1) & (C - 1)
    cm_f = jt <= it
    cm_b = jt >= it
    kmask = ((lax.broadcasted_iota(jnp.int32, (4 * C, 128), 0) >> 6)
             == (lax.broadcasted_iota(jnp.int32, (4 * C, 128), 1) >> 5)).astype(F32)
    vmask = ((lax.broadcasted_iota(jnp.int32, (4 * C, 256), 0) >> 6)
             == (lax.broadcasted_iota(jnp.int32, (4 * C, 256), 1) >> 6)).astype(F32)
    smask = ((lax.broadcasted_iota(jnp.int32, (256, 128), 0) >> 6)
             == (lax.broadcasted_iota(jnp.int32, (256, 128), 1) >> 5)).astype(F32)

    sf_s[...] = jnp.zeros_like(sf_s)
    sb_s[...] = jnp.zeros_like(sb_s)

    def chunk(off, g_ref, s_ref, dst, tri, cm, last, mid):
        rows = pl.ds(off, C)
        q = q_ref[0, rows, :]
        k = k_ref[0, rows, :]
        v = v_ref[0, rows, :]
        g = g_ref[0, rows, :]
        st = s_ref[...]
        b = jnp.dot(tri, g, precision=HIGHEST, preferred_element_type=F32)
        b_last = b[last:last + 1]
        b_mid = b[mid:mid + 1]
        qi = q * jnp.exp(jnp.clip(b - b_mid, -80.0, 80.0))
        ki = k * jnp.exp(jnp.clip(b_mid - b, -80.0, 80.0))
        krows = jnp.concatenate([ki, ki, ki, ki], axis=0) * kmask
        a = jnp.where(cm, _nt(qi.astype(BF16), krows.astype(BF16)), 0.0)
        vblk = jnp.concatenate([v, v, v, v], axis=0) * vmask
        o = jnp.dot(a.astype(BF16), vblk.astype(BF16), preferred_element_type=F32)
        o = o + _nt((q * jnp.exp(b)).astype(BF16), st.astype(BF16))
        dst[rows, :] = o
        kd = k * jnp.exp(b_last - b)
        upd = jnp.dot(v.T.astype(BF16), kd.astype(BF16), preferred_element_type=F32)
        s_ref[...] = st * jnp.exp(b_last) + upd * smask

    def body(t, carry):
        cb = jnp.where(t < nctx, nctx - 1 - t, nch + nctx - 1 - t)
        chunk(pl.multiple_of(t * C, C), gf_ref, sf_s, of_s, tri_f, cm_f, C - 1, C // 2 - 1)
        chunk(pl.multiple_of(cb * C, C), gb_ref, sb_s, ob_s, tri_b, cm_b, 0, C // 2)
        return carry

    lax.fori_loop(0, nch, body, 0, unroll=2)

    hones = ((lax.broadcasted_iota(jnp.int32, (256, 256), 0) >> 6)
             == (lax.broadcasted_iota(jnp.int32, (256, 256), 1) >> 6)).astype(F32) * (1.0 / GLA_DV)
    ng = ng_ref[...]
    for t in range(NT_TILES):
        rows = pl.ds(t * TM, TM)
        o = of_s[rows, :] + ob_s[rows, :]
        ms = jnp.dot(o * o, hones, precision=HIGHEST, preferred_element_type=F32)
        r = r_ref[0, rows, :]
        o_ref[0, rows, :] = (o * lax.rsqrt(ms + EPS) * ng * (r * _sigmoid(r))).astype(BF16)


def _gla(gq, gk, ggf, ggb, gv, gr, ng):
    nb = gq.shape[0]
    seq = lambda w: pl.BlockSpec((1, T, w), lambda b: (b, 0, 0))
    return pl.pallas_call(
        _gla_kernel,
        grid=(nb,),
        in_specs=[seq(128), seq(128), seq(128), seq(128), seq(256), seq(256),
                  pl.BlockSpec((1, 256), lambda b: (0, 0))],
        out_specs=seq(256),
        out_shape=jax.ShapeDtypeStruct((nb, T, 256), BF16),
        scratch_shapes=[pltpu.VMEM((T, 256), F32), pltpu.VMEM((T, 256), F32),
                        pltpu.VMEM((256, 128), F32), pltpu.VMEM((256, 128), F32)],
        compiler_params=_params("arbitrary"),
        name="gla",
    )(gq, gk, ggf, ggb, gv, gr, ng)


def _stacked_heads(q, n):
    rows = lax.broadcasted_iota(jnp.int32, (NAT_HEADS * n, 256), 0)
    lanes = lax.broadcasted_iota(jnp.int32, (NAT_HEADS * n, 256), 1) >> 6
    head = jnp.zeros_like(rows)
    for h in range(1, NAT_HEADS):
        head = head + (rows >= h * n).astype(jnp.int32)
    own = head == lanes
    q4 = jnp.concatenate([q] * NAT_HEADS, axis=0)
    return jnp.where(own, q4, jnp.zeros_like(q4)), own


def _unstack_heads(pv, own, n):
    pv = jnp.where(own, pv, 0.0)
    out = pv[0:n]
    for h in range(1, NAT_HEADS):
        out = out + pv[h * n:(h + 1) * n]
    return out


def _nat_kernel(q_ref, k_ref, v_ref, bias_ref, o_ref, *, need_ctx):
    out0 = CTX if need_ctx else 0
    if need_ctx:
        kc = k_ref[0, 0:CTX, :]
        vc = v_ref[0, 0:CTX, :]
        q4, own = _stacked_heads(q_ref[0, 0:CTX, :], CTX)
        s = _nt(q4, kc)
        e = jnp.exp(s - jnp.max(s, axis=-1, keepdims=True))
        p = (e * (1.0 / jnp.sum(e, axis=-1, keepdims=True))).astype(BF16)
        o_ref[0, 0:CTX, :] = _unstack_heads(jnp.dot(p, vc, preferred_element_type=F32), own, CTX).astype(BF16)

    nwin = NAT_WIN_R * GRID_W

    def row(r, carry):
        rs = jnp.clip(r - NAT_WIN_R // 2, 0, GRID_ROWS - NAT_WIN_R)
        case = rs - r + NAT_WIN_R - 1
        qrows = pl.ds(pl.multiple_of(CTX + r * GRID_W, GRID_W), GRID_W)
        wrows = pl.ds(pl.multiple_of(CTX + rs * GRID_W, GRID_W), nwin)
        q4, own = _stacked_heads(q_ref[0, qrows, :], GRID_W)
        sw = _nt(q4, k_ref[0, wrows, :]) + bias_ref[case]
        sc = _nt(q4, k_ref[0, 0:CTX, :])
        m = jnp.maximum(jnp.max(sw, axis=-1, keepdims=True), jnp.max(sc, axis=-1, keepdims=True))
        ew = jnp.exp(sw - m)
        ec = jnp.exp(sc - m)
        inv = 1.0 / (jnp.sum(ew, axis=-1, keepdims=True) + jnp.sum(ec, axis=-1, keepdims=True))
        pv = jnp.dot((ew * inv).astype(BF16), v_ref[0, wrows, :], preferred_element_type=F32)
        pv = pv + jnp.dot((ec * inv).astype(BF16), v_ref[0, 0:CTX, :], preferred_element_type=F32)
        orows = pl.ds(pl.multiple_of(out0 + r * GRID_W, GRID_W), GRID_W)
        o_ref[0, orows, :] = _unstack_heads(pv, own, GRID_W).astype(BF16)
        return carry

    lax.fori_loop(0, GRID_ROWS, row, 0, unroll=8)


def _nat(nq, nk, nv, bias, need_ctx):
    nb = nq.shape[0]
    rows_out = T if need_ctx else SEQ
    seq = pl.BlockSpec((1, T, 256), lambda b: (b, 0, 0))
    return pl.pallas_call(
        functools.partial(_nat_kernel, need_ctx=need_ctx),
        grid=(nb,),
        in_specs=[seq, seq, seq, pl.BlockSpec(bias.shape, lambda b: (0, 0, 0))],
        out_specs=pl.BlockSpec((1, rows_out, 256), lambda b: (b, 0, 0)),
        out_shape=jax.ShapeDtypeStruct((nb, rows_out, 256), BF16),
        compiler_params=_params("arbitrary"),
        name="nat",
    )(nq, nk, nv, bias)


def _nat_bias_table(rpb):
    cidx = np.arange(GRID_W)
    col_start = np.clip(cidx - NAT_WIN_C // 2, 0, GRID_W - NAT_WIN_C)
    col_mask = (cidx[None, :] >= col_start[:, None]) & (cidx[None, :] < col_start[:, None] + NAT_WIN_C)
    dc = np.clip(cidx[None, :] - cidx[:, None] + NAT_WIN_C - 1, 0, 2 * NAT_WIN_C - 2)
    dr = np.arange(NAT_WIN_R)[:, None] + np.arange(NAT_WIN_R)[None, :]
    pick_r = np.eye(2 * NAT_WIN_R - 1, dtype=np.float32)[dr]
    pick_c = np.eye(2 * NAT_WIN_C - 1, dtype=np.float32)[dc]
    t = jnp.einsum('hab,cja,qkb->hcjqk', rpb.astype(F32), pick_r, pick_c, precision=HIGHEST)
    t = jnp.where(col_mask[None, None, None], t, -jnp.inf)
    return t.transpose(1, 0, 3, 2, 4).reshape(NAT_WIN_R, NAT_HEADS * GRID_W, NAT_WIN_R * GRID_W)


def _diff_kernel(lam_ref, q_ref, k_ref, v_ref, g_ref, o_ref, *, lam_init, need_ctx):
    lv = lam_ref[...]
    lam = (jnp.exp(jnp.sum(lv[0:1] * lv[1:2], axis=-1, keepdims=True))
           - jnp.exp(jnp.sum(lv[2:3] * lv[3:4], axis=-1, keepdims=True)) + lam_init)
    q = q_ref[0]
    first = lax.broadcasted_iota(jnp.int32, (1, 2 * DIFF_DH), 1) < DIFF_DH
    q1 = jnp.where(first, q, jnp.zeros_like(q))
    q2 = jnp.where(first, jnp.zeros_like(q), q)

    def attend(k, v):
        s1 = _nt(q1, k)
        s2 = _nt(q2, k)
        e1 = jnp.exp(s1 - jnp.max(s1, axis=-1, keepdims=True))
        e2 = jnp.exp(s2 - jnp.max(s2, axis=-1, keepdims=True))
        w1 = 1.0 / jnp.sum(e1, axis=-1, keepdims=True)
        w2 = lam / jnp.sum(e2, axis=-1, keepdims=True)
        p = (e1 * w1 - e2 * w2).astype(BF16)
        o = jnp.dot(p, v, preferred_element_type=F32)
        o_ref[0] = (_rms(o) * g_ref[...] * (1.0 - lam_init)).astype(BF16)

    if need_ctx:
        @pl.when(pl.program_id(2) == 0)
        def _():
            attend(k_ref[0, 0:CTX, :], v_ref[0, 0:CTX, :])

        @pl.when(pl.program_id(2) > 0)
        def _():
            attend(k_ref[0], v_ref[0])
    else:
        attend(k_ref[0], v_ref[0])


def _diff(dq, dk, dv, lam_vec, dg, lam_init, need_ctx):
    nb = dq.shape[0]
    w = 2 * DIFF_DH
    t0 = 0 if need_ctx else 1
    return pl.pallas_call(
        functools.partial(_diff_kernel, lam_init=lam_init, need_ctx=need_ctx),
        grid=(nb, DIFF_HEADS, NT_TILES - t0),
        in_specs=[pl.BlockSpec((4, DIFF_DH), lambda b, h, t: (0, 0)),
                  pl.BlockSpec((1, TM, w), lambda b, h, t: (b, t + t0, h)),
                  pl.BlockSpec((1, T, w), lambda b, h, t: (b, 0, h)),
                  pl.BlockSpec((1, T, w), lambda b, h, t: (b, 0, h)),
                  pl.BlockSpec((1, w), lambda b, h, t: (0, h))],
        out_specs=pl.BlockSpec((1, TM, w), lambda b, h, t: (b, t, h)),
        out_shape=jax.ShapeDtypeStruct((nb, T - t0 * TM, DIFF_HEADS * w), BF16),
        compiler_params=_params("arbitrary", "arbitrary", "arbitrary"),
        name="diff_attn",
    )(lam_vec, dq, dk, dv, dg)


def _out_kernel(x_ref, yg_ref, yn_ref, yd_ref, mod_ref, w_ref, g_ref, xo_ref, ht_ref):
    attn = jnp.dot(yg_ref[0], w_ref[0:256, :], preferred_element_type=F32)
    attn = attn + jnp.dot(yn_ref[0], w_ref[256:512, :], preferred_element_type=F32)
    attn = attn + jnp.dot(yd_ref[0], w_ref[512:1024, :], preferred_element_type=F32)
    m = mod_ref[0]
    x = x_ref[0] + m[:, 2 * D:3 * D] * attn
    xo_ref[0] = x
    h2 = _rms(x) * g_ref[...] * (1.0 + m[:, 4 * D:5 * D]) + m[:, 3 * D:4 * D]
    ht_ref[...] = h2.T.astype(BF16)


def _out_proj(x, yg, yn, yd, mod, w_out, g2, need_ctx):
    nb = x.shape[0]
    t0 = 0 if need_ctx else 1
    nt = NT_TILES - t0
    full = lambda w: pl.BlockSpec((1, TM, w), lambda b, t: (b, t + t0, 0))
    tile = lambda w: pl.BlockSpec((1, TM, w), lambda b, t: (b, t, 0))
    return pl.pallas_call(
        _out_kernel,
        grid=(nb, nt),
        in_specs=[full(D), full(256), tile(256), tile(512), _mod_spec(nb, t0),
                  pl.BlockSpec((D, D), lambda b, t: (0, 0)), pl.BlockSpec((1, D), lambda b, t: (0, 0))],
        out_specs=[tile(D), pl.BlockSpec((D, TM), lambda b, t: (0, b * nt + t))],
        out_shape=[jax.ShapeDtypeStruct((nb, nt * TM, D), F32), jax.ShapeDtypeStruct((D, nb * nt * TM), BF16)],
        compiler_params=_params("arbitrary", "arbitrary"),
        name="out_proj",
    )(x, yg, yn, yd, mod, w_out, g2)


def _oddeven_pairs(n):
    pairs = []

    def merge(lo, hi, r):
        step = r * 2
        if step < hi - lo:
            merge(lo, hi, step)
            merge(lo + r, hi, step)
            pairs.extend((i, i + r) for i in range(lo + r, hi - r, step))
        else:
            pairs.append((lo, lo + r))

    def sort(lo, hi):
        if hi - lo >= 1:
            mid = lo + (hi - lo) // 2
            sort(lo, mid)
            sort(mid + 1, hi)
            merge(lo, hi, 1)

    sort(0, n - 1)
    return pairs


def _bitonic_pairs(n):
    pairs = []
    d = n // 2
    while d >= 1:
        pairs.extend((i, i + d) for i in range(n) if (i // d) % 2 == 0)
        d //= 2
    return pairs


_SORT16 = _oddeven_pairs(16)
_BITONIC16 = _bitonic_pairs(16)


def _compare_exchange(xs, pairs):
    for i, j in pairs:
        a, b = xs[i], xs[j]
        if b is None:
            continue
        if a is None:
            xs[i], xs[j] = b, None
        else:
            xs[i], xs[j] = jnp.maximum(a, b), jnp.minimum(a, b)


def _merge_top16(xs, ys):
    zs = []
    for k in range(16):
        a, b = xs[k], ys[15 - k]
        zs.append(a if b is None else (b if a is None else jnp.maximum(a, b)))
    _compare_exchange(zs, _BITONIC16)
    return zs


def _top16_of_128(x3):
    xs = [x3[r] for r in range(16)]
    _compare_exchange(xs, _SORT16)
    for shift in (4, 2, 1):
        xs = _merge_top16(xs, [pltpu.roll(x, shift, 0) for x in xs])
    return xs


def _peer_select(s1, s2):
    n = s1.shape[-1]
    a3 = s1.reshape(16, 8, n)
    b3 = s2.reshape(16, 8, n)
    t1 = _top16_of_128(a3)
    t2 = _top16_of_128(b3)
    cand = [[t1[a] + t2[b] for b in range(PEER_TOPK // (a + 1))] for a in range(PEER_TOPK)]
    pad = lambda xs: xs + [None] * (16 - len(xs))
    top = cand[0]
    for a in range(1, 8):
        top = _merge_top16(top, pad(cand[a]))
    top = _merge_top16(top, pad([cand[a][0] for a in range(8, 16)]))
    tau = top[15]
    cmax = cand[0][0]
    inf = jnp.full_like(tau, jnp.inf)
    z = jnp.zeros_like(tau)
    thr = []
    for a in range(PEER_TOPK):
        th = inf
        for b, cv in enumerate(cand[a]):
            sel = cv >= tau
            th = jnp.where(sel, jnp.minimum(th, t2[b]), th)
            z = z + jnp.where(sel, jnp.exp(cv - cmax), 0.0)
        thr.append(th)
    theta = jnp.full_like(a3, jnp.inf)
    for a in range(PEER_TOPK):
        theta = jnp.where(a3 == t1[a][None], thr[a][None], theta)
    c = jnp.exp(a3 - t1[0][None]) / z[None]
    e2 = jnp.exp(b3 - t2[0][None])
    return theta, c, e2.reshape(PEER_NKEYS, n)


def _peer_kernel(ht_ref, wq_ref, sk_ref, u_ref, vt_ref, o_ref, th_s, c_s, s2_s, e2_s, h8_s, a_s, w_s, acc_s):
    e = pl.program_id(1)
    tb = ht_ref.shape[1]

    @pl.when(e == 0)
    def _():
        qt = jnp.dot(wq_ref[...], ht_ref[...], preferred_element_type=F32)
        sk = sk_ref[...]
        for h in range(PEER_HEADS):
            s1 = jnp.dot(sk[0], qt[h * 128:h * 128 + 64], precision=HIGHEST, preferred_element_type=F32)
            s2 = jnp.dot(sk[1], qt[h * 128 + 64:h * 128 + 128], precision=HIGHEST, preferred_element_type=F32)
            theta, c, e2 = _peer_select(s1, s2)
            th_s[h] = theta
            c_s[h] = c * PEER_W_SCALE
            for lc in range(tb // 128):
                s2_s[h, lc] = s2[:, lc * 128:(lc + 1) * 128]
                e2_s[h, lc] = e2[:, lc * 128:(lc + 1) * 128]
        acc_s[...] = jnp.zeros_like(acc_s)
        h8_s[...] = jnp.clip(ht_ref[...], -F8_MAX, F8_MAX).astype(F8)

    a = jnp.dot(u_ref[...], h8_s[...], preferred_element_type=F32) * (1.0 / PEER_U_SCALE)
    for lc in range(tb // 128):
        a_s[lc] = a[:, lc * 128:(lc + 1) * 128]
    for lc in range(tb // 128):
        lanes = slice(lc * 128, (lc + 1) * 128)
        for i in range(PEER_ROWS):
            g = jnp.zeros((PEER_NKEYS, 128), F32)
            for h in range(PEER_HEADS):
                th = th_s[h, e * (PEER_ROWS // 8) + i // 8, i % 8:i % 8 + 1, lanes]
                cw = c_s[h, e * (PEER_ROWS // 8) + i // 8, i % 8:i % 8 + 1, lanes]
                g = g + jnp.where(s2_s[h, lc] >= th, e2_s[h, lc] * cw, 0.0)
            rows = slice(i * PEER_NKEYS, (i + 1) * PEER_NKEYS)
            w = g * jax.nn.gelu(a_s[lc, rows, :], approximate=True)
            w_s[rows, lanes] = jnp.minimum(w, F8_MAX).astype(F8)
    acc_s[...] += jnp.dot(vt_ref[...], w_s[...], preferred_element_type=F32)

    @pl.when(e == pl.num_programs(1) - 1)
    def _():
        o_ref[...] = acc_s[...].T * (1.0 / PEER_W_SCALE)


def _peer(ht, wq_t, sk, u, v_t):
    n = ht.shape[1]
    tb = PEER_TB
    ne = PEER_EXPERTS // PEER_EB
    sel = pltpu.VMEM((PEER_HEADS, tb // 128, PEER_NKEYS, 128), F32)
    sel8 = pltpu.VMEM((PEER_HEADS, PEER_NKEYS // 8, 8, tb), F32)
    return pl.pallas_call(
        _peer_kernel,
        grid=(n // tb, ne),
        in_specs=[pl.BlockSpec((D, tb), lambda t, e: (0, t)),
                  pl.BlockSpec((D, D), lambda t, e: (0, 0)),
                  pl.BlockSpec((2, PEER_NKEYS, PEER_HALF), lambda t, e: (0, 0, 0)),
                  pl.BlockSpec((PEER_EB, D), lambda t, e: (e, 0)),
                  pl.BlockSpec((D, PEER_EB), lambda t, e: (0, e))],
        out_specs=pl.BlockSpec((tb, D), lambda t, e: (t, 0)),
        out_shape=jax.ShapeDtypeStruct((n, D), F32),
        scratch_shapes=[sel8, sel8, sel, sel, pltpu.VMEM((D, tb), F8),
                        pltpu.VMEM((tb // 128, PEER_EB, 128), F32), pltpu.VMEM((PEER_EB, tb), F8),
                        pltpu.VMEM((D, tb), F32)],
        compiler_params=_params("arbitrary", "arbitrary"),
        name="peer",
    )(ht, wq_t, sk, u, v_t)


def _final_kernel(x_ref, pr_ref, mod_ref, g_ref, o_ref):
    x = x_ref[0] + mod_ref[0][:, 5 * D:6 * D] * pr_ref[0]
    o_ref[0] = _rms(x) * g_ref[...]


def _final(x, peer, mod, g):
    nb = x.shape[0]
    lat = pl.BlockSpec((1, TM, D), lambda b, t: (b, t, 0))
    return pl.pallas_call(
        _final_kernel,
        grid=(nb, SEQ // TM),
        in_specs=[lat, lat, pl.BlockSpec((1, 1, 6 * D), lambda b, t: (b, 0, 0)),
                  pl.BlockSpec((1, D), lambda b, t: (0, 0))],
        out_specs=lat,
        out_shape=jax.ShapeDtypeStruct((nb, SEQ, D), F32),
        compiler_params=_params("arbitrary", "arbitrary"),
        name="final_norm",
    )(x, peer, mod, g)


def _rope_tables():
    half = DIFF_DH // 4
    inv = ROPE_BASE ** (-jnp.arange(half, dtype=F32) / half)
    ang = jnp.arange(GRID_W, dtype=F32)[:, None] * inv[None, :]

    def table(fn, ctx_value):
        small = fn(ang)
        by_row = jnp.broadcast_to(small[:GRID_ROWS, None, :], (GRID_ROWS, GRID_W, half))
        by_col = jnp.broadcast_to(small[None, :, :], (GRID_ROWS, GRID_W, half))
        per_map = jnp.concatenate([by_row, by_row, by_col, by_col], axis=-1)
        lat = jnp.tile(per_map.reshape(SEQ, DIFF_DH), (1, 2 * DIFF_HEADS))
        return jnp.concatenate([jnp.full((CTX, lat.shape[1]), ctx_value, F32), lat], axis=0)

    return table(jnp.cos, 1.0), table(jnp.sin, 0.0)


def _rot_columns(w):
    q = DIFF_DH // 4
    w4 = w.reshape(w.shape[0], -1, 2, q)
    return jnp.stack([-w4[:, :, 1], w4[:, :, 0]], axis=2).reshape(w.shape)


def _regroup_w_in(w):
    sizes = (128, 128, 256, 256, 16, 16, 256, 256, 256, 512, 512, 512)
    pts = np.cumsum((0,) + sizes)
    p = [w[:, pts[i]:pts[i + 1]] for i in range(12)]
    downs = jnp.concatenate([p[4], p[5], jnp.zeros((D, 128 - 2 * GLA_RANK), w.dtype)], axis=1)
    cols = [p[0], p[1], p[2], p[3], downs, p[6], p[7], p[8], p[9], p[10], p[11], _rot_columns(p[9]), _rot_columns(p[10])]
    return jnp.concatenate(cols, axis=1).astype(BF16)


def _gate_up_padded(gate_up):
    z = jnp.zeros((128, 128), F32)
    gf = z.at[0:GLA_RANK].set(gate_up[0]).astype(BF16)
    gb = z.at[GLA_RANK:2 * GLA_RANK].set(gate_up[1]).astype(BF16)
    return gf, gb


def kernel(x, c, ctx, c_ctx, ada_w, ada_b, norm1_g, norm2_g, w_in, gla_gate_up, gla_gate_b, gla_norm_g, nat_rpb,
           diff_lambda, diff_norm_g, w_out, peer_wq, peer_subkeys, peer_u, peer_v, final_g):
    nb = x.shape[0]
    depth = ada_w.shape[0]
    assert x.shape[1:] == (SEQ, D) and ctx.shape[1:] == (CTX, D) and (nb * T) % PEER_TB == 0 and (nb * SEQ) % PEER_TB == 0

    rows = -(-(nb + 1) // 8) * 8
    cc = jnp.concatenate([c, c_ctx[None], jnp.zeros((rows - nb - 1, D), F32)], axis=0)
    mod_all = _modulation(cc, ada_w, ada_b)
    cos, sin = _rope_tables()

    xa = jnp.concatenate([ctx, x], axis=1)
    peer = None
    mod_prev = None
    for l in range(depth):
        lam_init = 0.8 - 0.6 * math.exp(-0.3 * l)
        mod = mod_all[l].reshape(rows, 1, 6 * D)
        gf, gb = _gate_up_padded(gla_gate_up[l])
        xa, (gq, gk, ggf, ggb, gv, gr, nq, nk, nv, dq, dk, dv) = _norm_proj(
            xa, peer, mod_prev, mod, norm1_g[l][None], _regroup_w_in(w_in[l]), gf, gb, gla_gate_b[l], cos, sin)
        yg = _gla(gq, gk, ggf, ggb, gv, gr, gla_norm_g[l][None])
        need_ctx = l + 1 < depth
        yn = _nat(nq, nk, nv, _nat_bias_table(nat_rpb[l]), need_ctx)
        yd = _diff(dq, dk, dv, diff_lambda[l], diff_norm_g[l][None], lam_init, need_ctx)
        xa, ht = _out_proj(xa, yg, yn, yd, mod, w_out[l].astype(BF16), norm2_g[l][None], need_ctx)
        peer = _peer(ht, peer_wq[l].T.astype(BF16), peer_subkeys[l],
                     _to_f8(peer_u[l] * PEER_U_SCALE), _to_f8(peer_v[l].T)).reshape(xa.shape)
        mod_prev = mod
    return _final(xa, peer, mod_prev, final_g[None])
```

```python
import functools
import math

import numpy as np
import jax
import jax.numpy as jnp
from jax import lax
from jax.experimental import pallas as pl
from jax.experimental.pallas import tpu as pltpu

F32 = jnp.float32
BF16 = jnp.bfloat16
F8 = jnp.float8_e4m3fn
F8_MAX = 448.0
HIGHEST = lax.Precision.HIGHEST

D = 1024
SEQ = 2048
CTX = 256
T = SEQ + CTX
GRID_W = 64
GRID_ROWS = SEQ // GRID_W
EPS = 1e-6
TM = 256
NT_TILES = T // TM

GLA_HEADS, GLA_DK, GLA_DV, GLA_RANK, GLA_TAU, GLA_CHUNK = 4, 32, 64, 16, 16.0, 64
NAT_HEADS, NAT_DH, NAT_WIN_R, NAT_WIN_C = 4, 64, 8, 16
DIFF_HEADS, DIFF_DH = 4, 64
ROPE_BASE = 10000.0
PEER_HEADS, PEER_NKEYS, PEER_HALF, PEER_TOPK = 8, 128, 64, 16
PEER_EXPERTS = PEER_NKEYS * PEER_NKEYS
PEER_TB = 512
PEER_ROWS = 16
PEER_EB = PEER_ROWS * PEER_NKEYS
PEER_U_SCALE = 32.0
PEER_W_SCALE = 4.0

C_GQ, C_GK, C_GV, C_GR, C_GD = 0, 128, 256, 512, 768
C_NQ, C_NK, C_NV = 896, 1152, 1408
C_DQ, C_DK, C_DV, C_DQR, C_DKR = 1664, 2176, 2688, 3200, 3712
PROJ_COLS = 4224

_NT_DIMS = (((1,), (1,)), ((), ()))


def _nt(a, b):
    return lax.dot_general(a, b, _NT_DIMS, preferred_element_type=F32)


def _params(*sem):
    return pltpu.CompilerParams(dimension_semantics=sem, vmem_limit_bytes=56 * 1024 * 1024)


def _to_f8(x):
    return jnp.clip(x, -F8_MAX, F8_MAX).astype(F8)


def _rms(x):
    return x * lax.rsqrt(jnp.mean(x * x, axis=-1, keepdims=True) + EPS)


def _sigmoid(x):
    return 1.0 / (1.0 + jnp.exp(-x))


def _log_sigmoid(x):
    return jnp.minimum(x, 0.0) - jnp.log1p(jnp.exp(-jnp.abs(x)))


def _mod_kernel(c_ref, w_ref, b_ref, o_ref):
    c = c_ref[...]
    o_ref[0] = jnp.dot(c * _sigmoid(c), w_ref[0], precision=HIGHEST, preferred_element_type=F32) + b_ref[0]


def _modulation(cc, ada_w, ada_b):
    depth = ada_w.shape[0]
    rows = cc.shape[0]
    return pl.pallas_call(
        _mod_kernel,
        grid=(depth, 6),
        in_specs=[pl.BlockSpec((rows, D), lambda l, j: (0, 0)),
                  pl.BlockSpec((1, D, D), lambda l, j: (l, 0, j)),
                  pl.BlockSpec((1, 1, D), lambda l, j: (l, 0, j))],
        out_specs=pl.BlockSpec((1, rows, D), lambda l, j: (l, 0, j)),
        out_shape=jax.ShapeDtypeStruct((depth, rows, 6 * D), F32),
        compiler_params=_params("arbitrary", "arbitrary"),
        name="adaln_mod",
    )(cc, ada_w, ada_b.reshape(depth, 1, 6 * D))


def _proj_kernel(*refs, fuse_res):
    if fuse_res:
        x_ref, pr_ref, modp_ref = refs[:3]
        refs = refs[3:]
    else:
        x_ref = refs[0]
        refs = refs[1:]
    (mod_ref, g_ref, w_ref, gf_ref, gb_ref, gbias_ref, cos_ref, sin_ref) = refs[:8]
    outs = refs[8:]
    if fuse_res:
        xo_ref = outs[0]
        outs = outs[1:]
    (gq_ref, gk_ref, ggf_ref, ggb_ref, gv_ref, gr_ref, nq_ref, nk_ref, nv_ref, dq_ref, dk_ref, dv_ref) = outs

    x = x_ref[0]
    if fuse_res:
        x = x + modp_ref[0][:, 5 * D:6 * D] * pr_ref[0]
        xo_ref[0] = x
    m = mod_ref[0]
    h = _rms(x) * g_ref[...] * (1.0 + m[:, D:2 * D]) + m[:, 0:D]
    p = jnp.dot(h.astype(BF16), w_ref[...], preferred_element_type=F32)

    gq_ref[0] = p[:, C_GQ:C_GQ + 128] * (GLA_DK ** -0.5)
    gk_ref[0] = p[:, C_GK:C_GK + 128]
    gv_ref[0] = p[:, C_GV:C_GV + 256]
    gr_ref[0] = p[:, C_GR:C_GR + 256]
    pd = p[:, C_GD:C_GD + 128].astype(BF16)
    gbias = gbias_ref[...]
    xf = jnp.dot(pd, gf_ref[...], preferred_element_type=F32) + gbias[0:1]
    xb = jnp.dot(pd, gb_ref[...], preferred_element_type=F32) + gbias[1:2]
    ggf_ref[0] = _log_sigmoid(xf) / GLA_TAU
    ggb_ref[0] = _log_sigmoid(xb) / GLA_TAU

    nq_ref[0] = (p[:, C_NQ:C_NQ + 256] * (NAT_DH ** -0.5)).astype(BF16)
    nk_ref[0] = p[:, C_NK:C_NK + 256].astype(BF16)
    nv_ref[0] = p[:, C_NV:C_NV + 256].astype(BF16)

    cos = cos_ref[...]
    sin = sin_ref[...]
    dq = p[:, C_DQ:C_DQ + 512] * cos + p[:, C_DQR:C_DQR + 512] * sin
    dq_ref[0] = (dq * (DIFF_DH ** -0.5)).astype(BF16)
    dk_ref[0] = (p[:, C_DK:C_DK + 512] * cos + p[:, C_DKR:C_DKR + 512] * sin).astype(BF16)
    dv_ref[0] = p[:, C_DV:C_DV + 512].astype(BF16)


def _mod_spec(nb, t0=0):
    return pl.BlockSpec((1, 1, 6 * D), lambda b, t: (jnp.where(t + t0 == 0, nb, b), 0, 0))


def _norm_proj(x, peer, mod_prev, mod, g1, w_all, gf, gb, gbias, cos, sin):
    nb = x.shape[0]
    fuse = peer is not None
    tile = lambda w: pl.BlockSpec((1, TM, w), lambda b, t: (b, t, 0))
    full = lambda a: pl.BlockSpec(a.shape, lambda b, t: (0,) * a.ndim)
    ins, specs = [x], [tile(D)]
    if fuse:
        ins += [peer, mod_prev]
        specs += [tile(D), _mod_spec(nb)]
    ins += [mod, g1, w_all, gf, gb, gbias, cos, sin]
    specs += [_mod_spec(nb), full(g1), full(w_all), full(gf), full(gb), full(gbias),
              pl.BlockSpec((TM, 512), lambda b, t: (t, 0)), pl.BlockSpec((TM, 512), lambda b, t: (t, 0))]
    widths = [(128, F32)] * 4 + [(256, F32)] * 2 + [(256, BF16)] * 3 + [(512, BF16)] * 3
    out_shape = [jax.ShapeDtypeStruct((nb, T, w), dt) for w, dt in widths]
    out_specs = [tile(w) for w, _ in widths]
    if fuse:
        out_shape = [jax.ShapeDtypeStruct((nb, T, D), F32)] + out_shape
        out_specs = [tile(D)] + out_specs
    res = pl.pallas_call(
        functools.partial(_proj_kernel, fuse_res=fuse),
        grid=(nb, NT_TILES),
        in_specs=specs, out_specs=out_specs, out_shape=out_shape,
        compiler_params=_params("arbitrary", "arbitrary"),
        name="norm_proj",
    )(*ins)
    if fuse:
        return res[0], res[1:]
    return x, res


def _gla_kernel(q_ref, k_ref, gf_ref, gb_ref, v_ref, r_ref, ng_ref, o_ref, of_s, ob_s, sf_s, sb_s):
    C = GLA_CHUNK
    nch = T // C
    nctx = CTX // C
    ii = lax.broadcasted_iota(jnp.int32, (C, C), 0)
    jj = lax.broadcasted_iota(jnp.int32, (C, C), 1)
    tri_f = (jj <= ii).astype(F32)
    tri_b = (jj >= ii).astype(F32)
    it = lax.broadcasted_iota(jnp.int32, (C, 4 * C), 0)
    jt = lax.broadcasted_iota(jnp.int32, (C, 4 * C), 1) & (C - 1)
    cm_f = jt <= it
    cm_b = jt >= it
    kmask = ((lax.broadcasted_iota(jnp.int32, (4 * C, 128), 0) >> 6)
             == (lax.broadcasted_iota(jnp.int32, (4 * C, 128), 1) >> 5)).astype(F32)
    vmask = ((lax.broadcasted_iota(jnp.int32, (4 * C, 256), 0) >> 6)
             == (lax.broadcasted_iota(jnp.int32, (4 * C, 256), 1) >> 6)).astype(F32)
    smask = ((lax.broadcasted_iota(jnp.int32, (256, 128), 0) >> 6)
             == (lax.broadcasted_iota(jnp.int32, (256, 128), 1) >> 5)).astype(F32)

    sf_s[...] = jnp.zeros_like(sf_s)
    sb_s[...] = jnp.zeros_like(sb_s)

    def chunk(off, g_ref, s_ref, dst, tri, cm, last, mid):
        rows = pl.ds(off, C)
        q = q_ref[0, rows, :]
        k = k_ref[0, rows, :]
        v = v_ref[0, rows, :]
        g = g_ref[0, rows, :]
        st = s_ref[...]
        b = jnp.dot(tri, g, precision=HIGHEST, preferred_element_type=F32)
        b_last = b[last:last + 1]
        b_mid = b[mid:mid + 1]
        qi = q * jnp.exp(jnp.clip(b - b_mid, -80.0, 80.0))
        ki = k * jnp.exp(jnp.clip(b_mid - b, -80.0, 80.0))
        krows = jnp.concatenate([ki, ki, ki, ki], axis=0) * kmask
        a = jnp.where(cm, _nt(qi.astype(BF16), krows.astype(BF16)), 0.0)
        vblk = jnp.concatenate([v, v, v, v], axis=0) * vmask
        o = jnp.dot(a.astype(BF16), vblk.astype(BF16), preferred_element_type=F32)
        o = o + _nt((q * jnp.exp(b)).astype(BF16), st.astype(BF16))
        dst[rows, :] = o
        kd = k * jnp.exp(b_last - b)
        upd = jnp.dot(v.T.astype(BF16), kd.astype(BF16), preferred_element_type=F32)
        s_ref[...] = st * jnp.exp(b_last) + upd * smask

    def body(t, carry):
        cb = jnp.where(t < nctx, nctx - 1 - t, nch + nctx - 1 - t)
        chunk(pl.multiple_of(t * C, C), gf_ref, sf_s, of_s, tri_f, cm_f, C - 1, C // 2 - 1)
        chunk(pl.multiple_of(cb * C, C), gb_ref, sb_s, ob_s, tri_b, cm_b, 0, C // 2)
        return carry

    lax.fori_loop(0, nch, body, 0, unroll=4)

    hones = ((lax.broadcasted_iota(jnp.int32, (256, 256), 0) >> 6)
             == (lax.broadcasted_iota(jnp.int32, (256, 256), 1) >> 6)).astype(F32) * (1.0 / GLA_DV)
    ng = ng_ref[...]
    for t in range(NT_TILES):
        rows = pl.ds(t * TM, TM)
        o = of_s[rows, :] + ob_s[rows, :]
        ms = jnp.dot(o * o, hones, precision=HIGHEST, preferred_element_type=F32)
        r = r_ref[0, rows, :]
        o_ref[0, rows, :] = (o * lax.rsqrt(ms + EPS) * ng * (r * _sigmoid(r))).astype(BF16)


def _gla(gq, gk, ggf, ggb, gv, gr, ng):
    nb = gq.shape[0]
    seq = lambda w: pl.BlockSpec((1, T, w), lambda b: (b, 0, 0))
    return pl.pallas_call(
        _gla_kernel,
        grid=(nb,),
        in_specs=[seq(128), seq(128), seq(128), seq(128), seq(256), seq(256),
                  pl.BlockSpec((1, 256), lambda b: (0, 0))],
        out_specs=seq(256),
        out_shape=jax.ShapeDtypeStruct((nb, T, 256), BF16),
        scratch_shapes=[pltpu.VMEM((T, 256), F32), pltpu.VMEM((T, 256), F32),
                        pltpu.VMEM((256, 128), F32), pltpu.VMEM((256, 128), F32)],
        compiler_params=_params("arbitrary"),
        name="gla",
    )(gq, gk, ggf, ggb, gv, gr, ng)


def _stacked_heads(q, n):
    rows = lax.broadcasted_iota(jnp.int32, (NAT_HEADS * n, 256), 0)
    lanes = lax.broadcasted_iota(jnp.int32, (NAT_HEADS * n, 256), 1) >> 6
    head = jnp.zeros_like(rows)
    for h in range(1, NAT_HEADS):
        head = head + (rows >= h * n).astype(jnp.int32)
    own = head == lanes
    q4 = jnp.concatenate([q] * NAT_HEADS, axis=0)
    return jnp.where(own, q4, jnp.zeros_like(q4)), own


def _unstack_heads(pv, own, n):
    pv = jnp.where(own, pv, 0.0)
    out = pv[0:n]
    for h in range(1, NAT_HEADS):
        out = out + pv[h * n:(h + 1) * n]
    return out


def _nat_kernel(q_ref, k_ref, v_ref, bias_ref, o_ref, *, need_ctx):
    out0 = CTX if need_ctx else 0
    if need_ctx:
        kc = k_ref[0, 0:CTX, :]
        vc = v_ref[0, 0:CTX, :]
        q4, own = _stacked_heads(q_ref[0, 0:CTX, :], CTX)
        s = _nt(q4, kc)
        e = jnp.exp(s - jnp.max(s, axis=-1, keepdims=True))
        p = (e * (1.0 / jnp.sum(e, axis=-1, keepdims=True))).astype(BF16)
        o_ref[0, 0:CTX, :] = _unstack_heads(jnp.dot(p, vc, preferred_element_type=F32), own, CTX).astype(BF16)

    nwin = NAT_WIN_R * GRID_W

    def row(r, carry):
        rs = jnp.clip(r - NAT_WIN_R // 2, 0, GRID_ROWS - NAT_WIN_R)
        case = rs - r + NAT_WIN_R - 1
        qrows = pl.ds(pl.multiple_of(CTX + r * GRID_W, GRID_W), GRID_W)
        wrows = pl.ds(pl.multiple_of(CTX + rs * GRID_W, GRID_W), nwin)
        q4, own = _stacked_heads(q_ref[0, qrows, :], GRID_W)
        sw = _nt(q4, k_ref[0, wrows, :]) + bias_ref[case]
        sc = _nt(q4, k_ref[0, 0:CTX, :])
        m = jnp.maximum(jnp.max(sw, axis=-1, keepdims=True), jnp.max(sc, axis=-1, keepdims=True))
        ew = jnp.exp(sw - m)
        ec = jnp.exp(sc - m)
        inv = 1.0 / (jnp.sum(ew, axis=-1, keepdims=True) + jnp.sum(ec, axis=-1, keepdims=True))
        pv = jnp.dot((ew * inv).astype(BF16), v_ref[0, wrows, :], preferred_element_type=F32)
        pv = pv + jnp.dot((ec * inv).astype(BF16), v_ref[0, 0:CTX, :], preferred_element_type=F32)
        orows = pl.ds(pl.multiple_of(out0 + r * GRID_W, GRID_W), GRID_W)
        o_ref[0, orows, :] = _unstack_heads(pv, own, GRID_W).astype(BF16)
        return carry

    lax.fori_loop(0, GRID_ROWS, row, 0, unroll=8)


def _nat(nq, nk, nv, bias, need_ctx):
    nb = nq.shape[0]
    rows_out = T if need_ctx else SEQ
    seq = pl.BlockSpec((1, T, 256), lambda b: (b, 0, 0))
    return pl.pallas_call(
        functools.partial(_nat_kernel, need_ctx=need_ctx),
        grid=(nb,),
        in_specs=[seq, seq, seq, pl.BlockSpec(bias.shape, lambda b: (0, 0, 0))],
        out_specs=pl.BlockSpec((1, rows_out, 256), lambda b: (b, 0, 0)),
        out_shape=jax.ShapeDtypeStruct((nb, rows_out, 256), BF16),
        compiler_params=_params("arbitrary"),
        name="nat",
    )(nq, nk, nv, bias)


def _nat_bias_table(rpb):
    cidx = np.arange(GRID_W)
    col_start = np.clip(cidx - NAT_WIN_C // 2, 0, GRID_W - NAT_WIN_C)
    col_mask = (cidx[None, :] >= col_start[:, None]) & (cidx[None, :] < col_start[:, None] + NAT_WIN_C)
    dc = np.clip(cidx[None, :] - cidx[:, None] + NAT_WIN_C - 1, 0, 2 * NAT_WIN_C - 2)
    dr = np.arange(NAT_WIN_R)[:, None] + np.arange(NAT_WIN_R)[None, :]
    pick_r = np.eye(2 * NAT_WIN_R - 1, dtype=np.float32)[dr]
    pick_c = np.eye(2 * NAT_WIN_C - 1, dtype=np.float32)[dc]
    t = jnp.einsum('hab,cja,qkb->hcjqk', rpb.astype(F32), pick_r, pick_c, precision=HIGHEST)
    t = jnp.where(col_mask[None, None, None], t, -jnp.inf)
    return t.transpose(1, 0, 3, 2, 4).reshape(NAT_WIN_R, NAT_HEADS * GRID_W, NAT_WIN_R * GRID_W)


def _diff_kernel(lam_ref, q_ref, k_ref, v_ref, g_ref, o_ref, *, lam_init, need_ctx):
    lv = lam_ref[...]
    lam = (jnp.exp(jnp.sum(lv[0:1] * lv[1:2], axis=-1, keepdims=True))
           - jnp.exp(jnp.sum(lv[2:3] * lv[3:4], axis=-1, keepdims=True)) + lam_init)
    w = 2 * DIFF_DH
    first = lax.broadcasted_iota(jnp.int32, (1, w), 1) < DIFF_DH

    def attend(nkeys):
        for hh in range(2):
            cols = slice(hh * w, (hh + 1) * w)
            q = q_ref[0, :, cols]
            k = k_ref[0, 0:nkeys, cols]
            v = v_ref[0, 0:nkeys, cols]
            s1 = _nt(jnp.where(first, q, jnp.zeros_like(q)), k)
            s2 = _nt(jnp.where(first, jnp.zeros_like(q), q), k)
            e1 = jnp.exp(s1 - jnp.max(s1, axis=-1, keepdims=True))
            e2 = jnp.exp(s2 - jnp.max(s2, axis=-1, keepdims=True))
            w1 = 1.0 / jnp.sum(e1, axis=-1, keepdims=True)
            w2 = lam / jnp.sum(e2, axis=-1, keepdims=True)
            p = (e1 * w1 - e2 * w2).astype(BF16)
            o = jnp.dot(p, v, preferred_element_type=F32)
            o_ref[0, :, cols] = (_rms(o) * g_ref[:, cols] * (1.0 - lam_init)).astype(BF16)

    if need_ctx:
        @pl.when(pl.program_id(2) == 0)
        def _():
            attend(CTX)

        @pl.when(pl.program_id(2) > 0)
        def _():
            attend(T)
    else:
        attend(T)


def _diff(dq, dk, dv, lam_vec, dg, lam_init, need_ctx):
    nb = dq.shape[0]
    w = 4 * DIFF_DH
    t0 = 0 if need_ctx else 1
    return pl.pallas_call(
        functools.partial(_diff_kernel, lam_init=lam_init, need_ctx=need_ctx),
        grid=(nb, DIFF_HEADS // 2, NT_TILES - t0),
        in_specs=[pl.BlockSpec((4, DIFF_DH), lambda b, h, t: (0, 0)),
                  pl.BlockSpec((1, TM, w), lambda b, h, t: (b, t + t0, h)),
                  pl.BlockSpec((1, T, w), lambda b, h, t: (b, 0, h)),
                  pl.BlockSpec((1, T, w), lambda b, h, t: (b, 0, h)),
                  pl.BlockSpec((1, w), lambda b, h, t: (0, h))],
        out_specs=pl.BlockSpec((1, TM, w), lambda b, h, t: (b, t, h)),
        out_shape=jax.ShapeDtypeStruct((nb, T - t0 * TM, DIFF_HEADS * 2 * DIFF_DH), BF16),
        compiler_params=_params("arbitrary", "arbitrary", "arbitrary"),
        name="diff_attn",
    )(lam_vec, dq, dk, dv, dg)


def _out_kernel(x_ref, yg_ref, yn_ref, yd_ref, mod_ref, w_ref, g_ref, xo_ref, ht_ref):
    attn = jnp.dot(yg_ref[0], w_ref[0:256, :], preferred_element_type=F32)
    attn = attn + jnp.dot(yn_ref[0], w_ref[256:512, :], preferred_element_type=F32)
    attn = attn + jnp.dot(yd_ref[0], w_ref[512:1024, :], preferred_element_type=F32)
    m = mod_ref[0]
    x = x_ref[0] + m[:, 2 * D:3 * D] * attn
    xo_ref[0] = x
    h2 = _rms(x) * g_ref[...] * (1.0 + m[:, 4 * D:5 * D]) + m[:, 3 * D:4 * D]
    ht_ref[...] = h2.T.astype(BF16)


def _out_proj(x, yg, yn, yd, mod, w_out, g2, need_ctx):
    nb = x.shape[0]
    t0 = 0 if need_ctx else 1
    nt = NT_TILES - t0
    full = lambda w: pl.BlockSpec((1, TM, w), lambda b, t: (b, t + t0, 0))
    tile = lambda w: pl.BlockSpec((1, TM, w), lambda b, t: (b, t, 0))
    return pl.pallas_call(
        _out_kernel,
        grid=(nb, nt),
        in_specs=[full(D), full(256), tile(256), tile(512), _mod_spec(nb, t0),
                  pl.BlockSpec((D, D), lambda b, t: (0, 0)), pl.BlockSpec((1, D), lambda b, t: (0, 0))],
        out_specs=[tile(D), pl.BlockSpec((D, TM), lambda b, t: (0, b * nt + t))],
        out_shape=[jax.ShapeDtypeStruct((nb, nt * TM, D), F32), jax.ShapeDtypeStruct((D, nb * nt * TM), BF16)],
        compiler_params=_params("arbitrary", "arbitrary"),
        name="out_proj",
    )(x, yg, yn, yd, mod, w_out, g2)


def _oddeven_pairs(n):
    pairs = []

    def merge(lo, hi, r):
        step = r * 2
        if step < hi - lo:
            merge(lo, hi, step)
            merge(lo + r, hi, step)
            pairs.extend((i, i + r) for i in range(lo + r, hi - r, step))
        else:
            pairs.append((lo, lo + r))

    def sort(lo, hi):
        if hi - lo >= 1:
            mid = lo + (hi - lo) // 2
            sort(lo, mid)
            sort(mid + 1, hi)
            merge(lo, hi, 1)

    sort(0, n - 1)
    return pairs


def _bitonic_pairs(n):
    pairs = []
    d = n // 2
    while d >= 1:
        pairs.extend((i, i + d) for i in range(n) if (i // d) % 2 == 0)
        d //= 2
    return pairs


_SORT16 = _oddeven_pairs(16)
_BITONIC16 = _bitonic_pairs(16)


def _compare_exchange(xs, pairs):
    for i, j in pairs:
        a, b = xs[i], xs[j]
        if b is None:
            continue
        if a is None:
            xs[i], xs[j] = b, None
        else:
            xs[i], xs[j] = jnp.maximum(a, b), jnp.minimum(a, b)


def _merge_top16(xs, ys):
    zs = []
    for k in range(16):
        a, b = xs[k], ys[15 - k]
        zs.append(a if b is None else (b if a is None else jnp.maximum(a, b)))
    _compare_exchange(zs, _BITONIC16)
    return zs


def _top16_of_128(x3):
    xs = [x3[r] for r in range(16)]
    _compare_exchange(xs, _SORT16)
    for shift in (4, 2, 1):
        xs = _merge_top16(xs, [pltpu.roll(x, shift, 0) for x in xs])
    return xs


def _peer_select(s1, s2):
    n = s1.shape[-1]
    a3 = s1.reshape(16, 8, n)
    b3 = s2.reshape(16, 8, n)
    t1 = _top16_of_128(a3)
    t2 = _top16_of_128(b3)
    cand = [[t1[a] + t2[b] for b in range(PEER_TOPK // (a + 1))] for a in range(PEER_TOPK)]
    pad = lambda xs: xs + [None] * (16 - len(xs))
    top = cand[0]
    for a in range(1, 8):
        top = _merge_top16(top, pad(cand[a]))
    top = _merge_top16(top, pad([cand[a][0] for a in range(8, 16)]))
    tau = top[15]
    cmax = cand[0][0]
    inf = jnp.full_like(tau, jnp.inf)
    z = jnp.zeros_like(tau)
    thr = []
    for a in range(PEER_TOPK):
        th = inf
        for b, cv in enumerate(cand[a]):
            sel = cv >= tau
            th = jnp.where(sel, jnp.minimum(th, t2[b]), th)
            z = z + jnp.where(sel, jnp.exp(cv - cmax), 0.0)
        thr.append(th)
    theta = jnp.full_like(a3, jnp.inf)
    for a in range(PEER_TOPK):
        theta = jnp.where(a3 == t1[a][None], thr[a][None], theta)
    c = jnp.exp(a3 - t1[0][None]) / z[None]
    e2 = jnp.exp(b3 - t2[0][None])
    return theta, c, e2.reshape(PEER_NKEYS, n)


def _peer_kernel(ht_ref, wq_ref, sk_ref, u_ref, vt_ref, o_ref, th_s, c_s, s2_s, e2_s, h8_s, a_s, w_s, acc_s):
    e = pl.program_id(1)
    tb = ht_ref.shape[1]

    @pl.when(e == 0)
    def _():
        qt = jnp.dot(wq_ref[...], ht_ref[...], preferred_element_type=F32)
        sk = sk_ref[...]
        for h in range(PEER_HEADS):
            s1 = jnp.dot(sk[0], qt[h * 128:h * 128 + 64], precision=HIGHEST, preferred_element_type=F32)
            s2 = jnp.dot(sk[1], qt[h * 128 + 64:h * 128 + 128], precision=HIGHEST, preferred_element_type=F32)
            theta, c, e2 = _peer_select(s1, s2)
            th_s[h] = theta
            c_s[h] = c * PEER_W_SCALE
            for lc in range(tb // 128):
                s2_s[h, lc] = s2[:, lc * 128:(lc + 1) * 128]
                e2_s[h, lc] = e2[:, lc * 128:(lc + 1) * 128]
        acc_s[...] = jnp.zeros_like(acc_s)
        h8_s[...] = jnp.clip(ht_ref[...], -F8_MAX, F8_MAX).astype(F8)

    a = jnp.dot(u_ref[...], h8_s[...], preferred_element_type=F32) * (1.0 / PEER_U_SCALE)
    for lc in range(tb // 128):
        a_s[lc] = a[:, lc * 128:(lc + 1) * 128]
    for lc in range(tb // 128):
        lanes = slice(lc * 128, (lc + 1) * 128)
        for i in range(PEER_ROWS):
            g = jnp.zeros((PEER_NKEYS, 128), F32)
            for h in range(PEER_HEADS):
                th = th_s[h, e * (PEER_ROWS // 8) + i // 8, i % 8:i % 8 + 1, lanes]
                cw = c_s[h, e * (PEER_ROWS // 8) + i // 8, i % 8:i % 8 + 1, lanes]
                g = g + jnp.where(s2_s[h, lc] >= th, e2_s[h, lc] * cw, 0.0)
            rows = slice(i * PEER_NKEYS, (i + 1) * PEER_NKEYS)
            w = g * jax.nn.gelu(a_s[lc, rows, :], approximate=True)
            w_s[rows, lanes] = jnp.minimum(w, F8_MAX).astype(F8)
    acc_s[...] += jnp.dot(vt_ref[...], w_s[...], preferred_element_type=F32)

    @pl.when(e == pl.num_programs(1) - 1)
    def _():
        o_ref[...] = acc_s[...].T * (1.0 / PEER_W_SCALE)


def _peer(ht, wq_t, sk, u, v_t):
    n = ht.shape[1]
    tb = PEER_TB
    ne = PEER_EXPERTS // PEER_EB
    sel = pltpu.VMEM((PEER_HEADS, tb // 128, PEER_NKEYS, 128), F32)
    sel8 = pltpu.VMEM((PEER_HEADS, PEER_NKEYS // 8, 8, tb), F32)
    return pl.pallas_call(
        _peer_kernel,
        grid=(n // tb, ne),
        in_specs=[pl.BlockSpec((D, tb), lambda t, e: (0, t)),
                  pl.BlockSpec((D, D), lambda t, e: (0, 0)),
                  pl.BlockSpec((2, PEER_NKEYS, PEER_HALF), lambda t, e: (0, 0, 0)),
                  pl.BlockSpec((PEER_EB, D), lambda t, e: (e, 0)),
                  pl.BlockSpec((D, PEER_EB), lambda t, e: (0, e))],
        out_specs=pl.BlockSpec((tb, D), lambda t, e: (t, 0)),
        out_shape=jax.ShapeDtypeStruct((n, D), F32),
        scratch_shapes=[sel8, sel8, sel, sel, pltpu.VMEM((D, tb), F8),
                        pltpu.VMEM((tb // 128, PEER_EB, 128), F32), pltpu.VMEM((PEER_EB, tb), F8),
                        pltpu.VMEM((D, tb), F32)],
        compiler_params=_params("arbitrary", "arbitrary"),
        name="peer",
    )(ht, wq_t, sk, u, v_t)


def _final_kernel(x_ref, pr_ref, mod_ref, g_ref, o_ref):
    x = x_ref[0] + mod_ref[0][:, 5 * D:6 * D] * pr_ref[0]
    o_ref[0] = _rms(x) * g_ref[...]


def _final(x, peer, mod, g):
    nb = x.shape[0]
    lat = pl.BlockSpec((1, TM, D), lambda b, t: (b, t, 0))
    return pl.pallas_call(
        _final_kernel,
        grid=(nb, SEQ // TM),
        in_specs=[lat, lat, pl.BlockSpec((1, 1, 6 * D), lambda b, t: (b, 0, 0)),
                  pl.BlockSpec((1, D), lambda b, t: (0, 0))],
        out_specs=lat,
        out_shape=jax.ShapeDtypeStruct((nb, SEQ, D), F32),
        compiler_params=_params("arbitrary", "arbitrary"),
        name="final_norm",
    )(x, peer, mod, g)


def _rope_tables():
    half = DIFF_DH // 4
    inv = ROPE_BASE ** (-jnp.arange(half, dtype=F32) / half)
    ang = jnp.arange(GRID_W, dtype=F32)[:, None] * inv[None, :]

    def table(fn, ctx_value):
        small = fn(ang)
        by_row = jnp.broadcast_to(small[:GRID_ROWS, None, :], (GRID_ROWS, GRID_W, half))
        by_col = jnp.broadcast_to(small[None, :, :], (GRID_ROWS, GRID_W, half))
        per_map = jnp.concatenate([by_row, by_row, by_col, by_col], axis=-1)
        lat = jnp.tile(per_map.reshape(SEQ, DIFF_DH), (1, 2 * DIFF_HEADS))
        return jnp.concatenate([jnp.full((CTX, lat.shape[1]), ctx_value, F32), lat], axis=0)

    return table(jnp.cos, 1.0), table(jnp.sin, 0.0)


def _rot_columns(w):
    q = DIFF_DH // 4
    w4 = w.reshape(w.shape[0], -1, 2, q)
    return jnp.stack([-w4[:, :, 1], w4[:, :, 0]], axis=2).reshape(w.shape)


def _regroup_w_in(w):
    sizes = (128, 128, 256, 256, 16, 16, 256, 256, 256, 512, 512, 512)
    pts = np.cumsum((0,) + sizes)
    p = [w[:, pts[i]:pts[i + 1]] for i in range(12)]
    downs = jnp.concatenate([p[4], p[5], jnp.zeros((D, 128 - 2 * GLA_RANK), w.dtype)], axis=1)
    cols = [p[0], p[1], p[2], p[3], downs, p[6], p[7], p[8], p[9], p[10], p[11], _rot_columns(p[9]), _rot_columns(p[10])]
    return jnp.concatenate(cols, axis=1).astype(BF16)


def _gate_up_padded(gate_up):
    z = jnp.zeros((128, 128), F32)
    gf = z.at[0:GLA_RANK].set(gate_up[0]).astype(BF16)
    gb = z.at[GLA_RANK:2 * GLA_RANK].set(gate_up[1]).astype(BF16)
    return gf, gb


def kernel(x, c, ctx, c_ctx, ada_w, ada_b, norm1_g, norm2_g, w_in, gla_gate_up, gla_gate_b, gla_norm_g, nat_rpb,
           diff_lambda, diff_norm_g, w_out, peer_wq, peer_subkeys, peer_u, peer_v, final_g):
    nb = x.shape[0]
    depth = ada_w.shape[0]
    assert x.shape[1:] == (SEQ, D) and ctx.shape[1:] == (CTX, D) and (nb * T) % PEER_TB == 0 and (nb * SEQ) % PEER_TB == 0

    rows = -(-(nb + 1) // 8) * 8
    cc = jnp.concatenate([c, c_ctx[None], jnp.zeros((rows - nb - 1, D), F32)], axis=0)
    mod_all = _modulation(cc, ada_w, ada_b)
    cos, sin = _rope_tables()

    xa = jnp.concatenate([ctx, x], axis=1)
    peer = None
    mod_prev = None
    for l in range(depth):
        lam_init = 0.8 - 0.6 * math.exp(-0.3 * l)
        mod = mod_all[l].reshape(rows, 1, 6 * D)
        gf, gb = _gate_up_padded(gla_gate_up[l])
        xa, (gq, gk, ggf, ggb, gv, gr, nq, nk, nv, dq, dk, dv) = _norm_proj(
            xa, peer, mod_prev, mod, norm1_g[l][None], _regroup_w_in(w_in[l]), gf, gb, gla_gate_b[l], cos, sin)
        yg = _gla(gq, gk, ggf, ggb, gv, gr, gla_norm_g[l][None])
        need_ctx = l + 1 < depth
        yn = _nat(nq, nk, nv, _nat_bias_table(nat_rpb[l]), need_ctx)
        yd = _diff(dq, dk, dv, diff_lambda[l], diff_norm_g[l][None], lam_init, need_ctx)
        xa, ht = _out_proj(xa, yg, yn, yd, mod, w_out[l].astype(BF16), norm2_g[l][None], need_ctx)
        peer = _peer(ht, peer_wq[l].T.astype(BF16), peer_subkeys[l],
                     _to_f8(peer_u[l] * PEER_U_SCALE), _to_f8(peer_v[l].T)).reshape(xa.shape)
        mod_prev = mod
    return _final(xa, peer, mod_prev, final_g[None])
```

```python
import functools
import math

import numpy as np
import jax
import jax.numpy as jnp
from jax import lax
from jax.experimental import pallas as pl
from jax.experimental.pallas import tpu as pltpu

F32 = jnp.float32
BF16 = jnp.bfloat16
F8 = jnp.float8_e4m3fn
F8_MAX = 448.0
HIGHEST = lax.Precision.HIGHEST

D = 1024
SEQ = 2048
CTX = 256
T = SEQ + CTX
GRID_W = 64
GRID_ROWS = SEQ // GRID_W
EPS = 1e-6
TM = 256
NT_TILES = T // TM

GLA_HEADS, GLA_DK, GLA_DV, GLA_RANK, GLA_TAU, GLA_CHUNK = 4, 32, 64, 16, 16.0, 64
NAT_HEADS, NAT_DH, NAT_WIN_R, NAT_WIN_C = 4, 64, 8, 16
DIFF_HEADS, DIFF_DH = 4, 64
DIFF_GROUP = 4
ROPE_BASE = 10000.0
PEER_HEADS, PEER_NKEYS, PEER_HALF, PEER_TOPK = 8, 128, 64, 16
PEER_EXPERTS = PEER_NKEYS * PEER_NKEYS
PEER_TB = 512
PEER_ROWS = 16
PEER_EB = PEER_ROWS * PEER_NKEYS
PEER_U_SCALE = 32.0
PEER_W_SCALE = 4.0

C_GQ, C_GK, C_GV, C_GR, C_GD = 0, 128, 256, 512, 768
C_NQ, C_NK, C_NV = 896, 1152, 1408
C_DQ, C_DK, C_DV, C_DQR, C_DKR = 1664, 2176, 2688, 3200, 3712
PROJ_COLS = 4224

_NT_DIMS = (((1,), (1,)), ((), ()))


def _nt(a, b):
    return lax.dot_general(a, b, _NT_DIMS, preferred_element_type=F32)


def _params(*sem):
    return pltpu.CompilerParams(dimension_semantics=sem, vmem_limit_bytes=56 * 1024 * 1024)


def _to_f8(x):
    return jnp.clip(x, -F8_MAX, F8_MAX).astype(F8)


def _rms(x):
    return x * lax.rsqrt(jnp.mean(x * x, axis=-1, keepdims=True) + EPS)


def _sigmoid(x):
    return 1.0 / (1.0 + jnp.exp(-x))


def _log_sigmoid(x):
    return jnp.minimum(x, 0.0) - jnp.log1p(jnp.exp(-jnp.abs(x)))


def _mod_kernel(c_ref, w_ref, b_ref, o_ref):
    c = c_ref[...]
    o_ref[0] = jnp.dot(c * _sigmoid(c), w_ref[0], precision=HIGHEST, preferred_element_type=F32) + b_ref[0]


def _modulation(cc, ada_w, ada_b):
    depth = ada_w.shape[0]
    rows = cc.shape[0]
    return pl.pallas_call(
        _mod_kernel,
        grid=(depth, 6),
        in_specs=[pl.BlockSpec((rows, D), lambda l, j: (0, 0)),
                  pl.BlockSpec((1, D, D), lambda l, j: (l, 0, j)),
                  pl.BlockSpec((1, 1, D), lambda l, j: (l, 0, j))],
        out_specs=pl.BlockSpec((1, rows, D), lambda l, j: (l, 0, j)),
        out_shape=jax.ShapeDtypeStruct((depth, rows, 6 * D), F32),
        compiler_params=_params("arbitrary", "arbitrary"),
        name="adaln_mod",
    )(cc, ada_w, ada_b.reshape(depth, 1, 6 * D))


def _proj_kernel(*refs, fuse_res):
    if fuse_res:
        x_ref, pr_ref, modp_ref = refs[:3]
        refs = refs[3:]
    else:
        x_ref = refs[0]
        refs = refs[1:]
    (mod_ref, g_ref, w_ref, gf_ref, gb_ref, gbias_ref, cos_ref, sin_ref) = refs[:8]
    outs = refs[8:]
    if fuse_res:
        xo_ref = outs[0]
        outs = outs[1:]
    (gq_ref, gk_ref, ggf_ref, ggb_ref, gv_ref, gr_ref, nq_ref, nk_ref, nv_ref, dq_ref, dk_ref, dv_ref) = outs

    x = x_ref[0]
    if fuse_res:
        x = x + modp_ref[0][:, 5 * D:6 * D] * pr_ref[0]
        xo_ref[0] = x
    m = mod_ref[0]
    h = _rms(x) * g_ref[...] * (1.0 + m[:, D:2 * D]) + m[:, 0:D]
    p = jnp.dot(h.astype(BF16), w_ref[...], preferred_element_type=F32)

    gq_ref[0] = p[:, C_GQ:C_GQ + 128] * (GLA_DK ** -0.5)
    gk_ref[0] = p[:, C_GK:C_GK + 128]
    gv_ref[0] = p[:, C_GV:C_GV + 256]
    gr_ref[0] = p[:, C_GR:C_GR + 256]
    pd = p[:, C_GD:C_GD + 128].astype(BF16)
    gbias = gbias_ref[...]
    xf = jnp.dot(pd, gf_ref[...], preferred_element_type=F32) + gbias[0:1]
    xb = jnp.dot(pd, gb_ref[...], preferred_element_type=F32) + gbias[1:2]
    ggf_ref[0] = _log_sigmoid(xf) / GLA_TAU
    ggb_ref[0] = _log_sigmoid(xb) / GLA_TAU

    nq_ref[0] = (p[:, C_NQ:C_NQ + 256] * (NAT_DH ** -0.5)).astype(BF16)
    nk_ref[0] = p[:, C_NK:C_NK + 256].astype(BF16)
    nv_ref[0] = p[:, C_NV:C_NV + 256].astype(BF16)

    cos = cos_ref[...]
    sin = sin_ref[...]
    dq = p[:, C_DQ:C_DQ + 512] * cos + p[:, C_DQR:C_DQR + 512] * sin
    dq_ref[0] = (dq * (DIFF_DH ** -0.5)).astype(BF16)
    dk_ref[0] = (p[:, C_DK:C_DK + 512] * cos + p[:, C_DKR:C_DKR + 512] * sin).astype(BF16)
    dv_ref[0] = p[:, C_DV:C_DV + 512].astype(BF16)


def _mod_spec(nb, t0=0):
    return pl.BlockSpec((1, 1, 6 * D), lambda b, t: (jnp.where(t + t0 == 0, nb, b), 0, 0))


def _norm_proj(x, peer, mod_prev, mod, g1, w_all, gf, gb, gbias, cos, sin):
    nb = x.shape[0]
    fuse = peer is not None
    tile = lambda w: pl.BlockSpec((1, TM, w), lambda b, t: (b, t, 0))
    full = lambda a: pl.BlockSpec(a.shape, lambda b, t: (0,) * a.ndim)
    ins, specs = [x], [tile(D)]
    if fuse:
        ins += [peer, mod_prev]
        specs += [tile(D), _mod_spec(nb)]
    ins += [mod, g1, w_all, gf, gb, gbias, cos, sin]
    specs += [_mod_spec(nb), full(g1), full(w_all), full(gf), full(gb), full(gbias),
              pl.BlockSpec((TM, 512), lambda b, t: (t, 0)), pl.BlockSpec((TM, 512), lambda b, t: (t, 0))]
    widths = [(128, F32)] * 4 + [(256, F32)] * 2 + [(256, BF16)] * 3 + [(512, BF16)] * 3
    out_shape = [jax.ShapeDtypeStruct((nb, T, w), dt) for w, dt in widths]
    out_specs = [tile(w) for w, _ in widths]
    if fuse:
        out_shape = [jax.ShapeDtypeStruct((nb, T, D), F32)] + out_shape
        out_specs = [tile(D)] + out_specs
    res = pl.pallas_call(
        functools.partial(_proj_kernel, fuse_res=fuse),
        grid=(nb, NT_TILES),
        in_specs=specs, out_specs=out_specs, out_shape=out_shape,
        compiler_params=_params("arbitrary", "arbitrary"),
        name="norm_proj",
    )(*ins)
    if fuse:
        return res[0], res[1:]
    return x, res


def _gla_kernel(q_ref, k_ref, gf_ref, gb_ref, v_ref, r_ref, ng_ref, o_ref, of_s, ob_s, sf_s, sb_s):
    C = GLA_CHUNK
    nch = T // C
    nctx = CTX // C
    ii = lax.broadcasted_iota(jnp.int32, (C, C), 0)
    jj = lax.broadcasted_iota(jnp.int32, (C, C), 1)
    tri_f = (jj <= ii).astype(F32)
    tri_b = (jj >= ii).astype(F32)
    it = lax.broadcasted_iota(jnp.int32, (C, 4 * C), 0)
    jt = lax.broadcasted_iota(jnp.int32, (C, 4 * C), 1) & (C - 1)
    cm_f = jt <= it
    cm_b = jt >= it
    kmask = ((lax.broadcasted_iota(jnp.int32, (4 * C, 128), 0) >> 6)
             == (lax.broadcasted_iota(jnp.int32, (4 * C, 128), 1) >> 5)).astype(F32)
    vmask = ((lax.broadcasted_iota(jnp.int32, (4 * C, 256), 0) >> 6)
             == (lax.broadcasted_iota(jnp.int32, (4 * C, 256), 1) >> 6)).astype(F32)
    smask = ((lax.broadcasted_iota(jnp.int32, (256, 128), 0) >> 6)
             == (lax.broadcasted_iota(jnp.int32, (256, 128), 1) >> 5)).astype(F32)

    sf_s[...] = jnp.zeros_like(sf_s)
    sb_s[...] = jnp.zeros_like(sb_s)

    def chunk(off, g_ref, s_ref, dst, tri, cm, last, mid):
        rows = pl.ds(off, C)
        q = q_ref[0, rows, :]
        k = k_ref[0, rows, :]
        v = v_ref[0, rows, :]
        g = g_ref[0, rows, :]
        st = s_ref[...]
        b = jnp.dot(tri, g, precision=HIGHEST, preferred_element_type=F32)
        b_last = b[last:last + 1]
        b_mid = b[mid:mid + 1]
        qi = q * jnp.exp(jnp.clip(b - b_mid, -80.0, 80.0))
        ki = k * jnp.exp(jnp.clip(b_mid - b, -80.0, 80.0))
        krows = jnp.concatenate([ki, ki, ki, ki], axis=0) * kmask
        a = jnp.where(cm, _nt(qi.astype(BF16), krows.astype(BF16)), 0.0)
        vblk = jnp.concatenate([v, v, v, v], axis=0) * vmask
        o = jnp.dot(a.astype(BF16), vblk.astype(BF16), preferred_element_type=F32)
        o = o + _nt((q * jnp.exp(b)).astype(BF16), st.astype(BF16))
        dst[rows, :] = o
        kd = k * jnp.exp(b_last - b)
        upd = jnp.dot(v.T.astype(BF16), kd.astype(BF16), preferred_element_type=F32)
        s_ref[...] = st * jnp.exp(b_last) + upd * smask

    def body(t, carry):
        cb = jnp.where(t < nctx, nctx - 1 - t, nch + nctx - 1 - t)
        chunk(pl.multiple_of(t * C, C), gf_ref, sf_s, of_s, tri_f, cm_f, C - 1, C // 2 - 1)
        chunk(pl.multiple_of(cb * C, C), gb_ref, sb_s, ob_s, tri_b, cm_b, 0, C // 2)
        return carry

    lax.fori_loop(0, nch, body, 0, unroll=4)

    hones = ((lax.broadcasted_iota(jnp.int32, (256, 256), 0) >> 6)
             == (lax.broadcasted_iota(jnp.int32, (256, 256), 1) >> 6)).astype(F32) * (1.0 / GLA_DV)
    ng = ng_ref[...]
    for t in range(NT_TILES):
        rows = pl.ds(t * TM, TM)
        o = of_s[rows, :] + ob_s[rows, :]
        ms = jnp.dot(o * o, hones, precision=HIGHEST, preferred_element_type=F32)
        r = r_ref[0, rows, :]
        o_ref[0, rows, :] = (o * lax.rsqrt(ms + EPS) * ng * (r * _sigmoid(r))).astype(BF16)


def _gla(gq, gk, ggf, ggb, gv, gr, ng):
    nb = gq.shape[0]
    seq = lambda w: pl.BlockSpec((1, T, w), lambda b: (b, 0, 0))
    return pl.pallas_call(
        _gla_kernel,
        grid=(nb,),
        in_specs=[seq(128), seq(128), seq(128), seq(128), seq(256), seq(256),
                  pl.BlockSpec((1, 256), lambda b: (0, 0))],
        out_specs=seq(256),
        out_shape=jax.ShapeDtypeStruct((nb, T, 256), BF16),
        scratch_shapes=[pltpu.VMEM((T, 256), F32), pltpu.VMEM((T, 256), F32),
                        pltpu.VMEM((256, 128), F32), pltpu.VMEM((256, 128), F32)],
        compiler_params=_params("arbitrary"),
        name="gla",
    )(gq, gk, ggf, ggb, gv, gr, ng)


def _stacked_heads(q, n):
    rows = lax.broadcasted_iota(jnp.int32, (NAT_HEADS * n, 256), 0)
    lanes = lax.broadcasted_iota(jnp.int32, (NAT_HEADS * n, 256), 1) >> 6
    head = jnp.zeros_like(rows)
    for h in range(1, NAT_HEADS):
        head = head + (rows >= h * n).astype(jnp.int32)
    own = head == lanes
    q4 = jnp.concatenate([q] * NAT_HEADS, axis=0)
    return jnp.where(own, q4, jnp.zeros_like(q4)), own


def _unstack_heads(pv, own, n):
    pv = jnp.where(own, pv, 0.0)
    out = pv[0:n]
    for h in range(1, NAT_HEADS):
        out = out + pv[h * n:(h + 1) * n]
    return out


def _nat_kernel(q_ref, k_ref, v_ref, bias_ref, o_ref, *, need_ctx):
    out0 = CTX if need_ctx else 0
    if need_ctx:
        kc = k_ref[0, 0:CTX, :]
        vc = v_ref[0, 0:CTX, :]
        q4, own = _stacked_heads(q_ref[0, 0:CTX, :], CTX)
        s = _nt(q4, kc)
        e = jnp.exp(s - jnp.max(s, axis=-1, keepdims=True))
        p = (e * (1.0 / jnp.sum(e, axis=-1, keepdims=True))).astype(BF16)
        o_ref[0, 0:CTX, :] = _unstack_heads(jnp.dot(p, vc, preferred_element_type=F32), own, CTX).astype(BF16)

    nwin = NAT_WIN_R * GRID_W

    def row(r, carry):
        rs = jnp.clip(r - NAT_WIN_R // 2, 0, GRID_ROWS - NAT_WIN_R)
        case = rs - r + NAT_WIN_R - 1
        qrows = pl.ds(pl.multiple_of(CTX + r * GRID_W, GRID_W), GRID_W)
        wrows = pl.ds(pl.multiple_of(CTX + rs * GRID_W, GRID_W), nwin)
        q4, own = _stacked_heads(q_ref[0, qrows, :], GRID_W)
        sw = _nt(q4, k_ref[0, wrows, :]) + bias_ref[case]
        sc = _nt(q4, k_ref[0, 0:CTX, :])
        m = jnp.maximum(jnp.max(sw, axis=-1, keepdims=True), jnp.max(sc, axis=-1, keepdims=True))
        ew = jnp.exp(sw - m)
        ec = jnp.exp(sc - m)
        inv = 1.0 / (jnp.sum(ew, axis=-1, keepdims=True) + jnp.sum(ec, axis=-1, keepdims=True))
        pv = jnp.dot((ew * inv).astype(BF16), v_ref[0, wrows, :], preferred_element_type=F32)
        pv = pv + jnp.dot((ec * inv).astype(BF16), v_ref[0, 0:CTX, :], preferred_element_type=F32)
        orows = pl.ds(pl.multiple_of(out0 + r * GRID_W, GRID_W), GRID_W)
        o_ref[0, orows, :] = _unstack_heads(pv, own, GRID_W).astype(BF16)
        return carry

    lax.fori_loop(0, GRID_ROWS, row, 0, unroll=8)


def _nat(nq, nk, nv, bias, need_ctx):
    nb = nq.shape[0]
    rows_out = T if need_ctx else SEQ
    seq = pl.BlockSpec((1, T, 256), lambda b: (b, 0, 0))
    return pl.pallas_call(
        functools.partial(_nat_kernel, need_ctx=need_ctx),
        grid=(nb,),
        in_specs=[seq, seq, seq, pl.BlockSpec(bias.shape, lambda b: (0, 0, 0))],
        out_specs=pl.BlockSpec((1, rows_out, 256), lambda b: (b, 0, 0)),
        out_shape=jax.ShapeDtypeStruct((nb, rows_out, 256), BF16),
        compiler_params=_params("arbitrary"),
        name="nat",
    )(nq, nk, nv, bias)


def _nat_bias_table(rpb):
    cidx = np.arange(GRID_W)
    col_start = np.clip(cidx - NAT_WIN_C // 2, 0, GRID_W - NAT_WIN_C)
    col_mask = (cidx[None, :] >= col_start[:, None]) & (cidx[None, :] < col_start[:, None] + NAT_WIN_C)
    dc = np.clip(cidx[None, :] - cidx[:, None] + NAT_WIN_C - 1, 0, 2 * NAT_WIN_C - 2)
    dr = np.arange(NAT_WIN_R)[:, None] + np.arange(NAT_WIN_R)[None, :]
    pick_r = np.eye(2 * NAT_WIN_R - 1, dtype=np.float32)[dr]
    pick_c = np.eye(2 * NAT_WIN_C - 1, dtype=np.float32)[dc]
    t = jnp.einsum('hab,cja,qkb->hcjqk', rpb.astype(F32), pick_r, pick_c, precision=HIGHEST)
    t = jnp.where(col_mask[None, None, None], t, -jnp.inf)
    return t.transpose(1, 0, 3, 2, 4).reshape(NAT_WIN_R, NAT_HEADS * GRID_W, NAT_WIN_R * GRID_W)


def _diff_kernel(lam_ref, q_ref, k_ref, v_ref, g_ref, o_ref, *, lam_init, need_ctx):
    lv = lam_ref[...]
    lam = (jnp.exp(jnp.sum(lv[0:1] * lv[1:2], axis=-1, keepdims=True))
           - jnp.exp(jnp.sum(lv[2:3] * lv[3:4], axis=-1, keepdims=True)) + lam_init)
    w = 2 * DIFF_DH
    first = lax.broadcasted_iota(jnp.int32, (1, w), 1) < DIFF_DH

    def attend(nkeys):
        for hh in range(DIFF_GROUP):
            cols = slice(hh * w, (hh + 1) * w)
            q = q_ref[0, :, cols]
            k = k_ref[0, 0:nkeys, cols]
            v = v_ref[0, 0:nkeys, cols]
            s1 = _nt(jnp.where(first, q, jnp.zeros_like(q)), k)
            s2 = _nt(jnp.where(first, jnp.zeros_like(q), q), k)
            e1 = jnp.exp(s1 - jnp.max(s1, axis=-1, keepdims=True))
            e2 = jnp.exp(s2 - jnp.max(s2, axis=-1, keepdims=True))
            w1 = 1.0 / jnp.sum(e1, axis=-1, keepdims=True)
            w2 = lam / jnp.sum(e2, axis=-1, keepdims=True)
            p = (e1 * w1 - e2 * w2).astype(BF16)
            o = jnp.dot(p, v, preferred_element_type=F32)
            o_ref[0, :, cols] = (_rms(o) * g_ref[:, cols] * (1.0 - lam_init)).astype(BF16)

    if need_ctx:
        @pl.when(pl.program_id(2) == 0)
        def _():
            attend(CTX)

        @pl.when(pl.program_id(2) > 0)
        def _():
            attend(T)
    else:
        attend(T)


def _diff(dq, dk, dv, lam_vec, dg, lam_init, need_ctx):
    nb = dq.shape[0]
    w = DIFF_GROUP * 2 * DIFF_DH
    t0 = 0 if need_ctx else 1
    return pl.pallas_call(
        functools.partial(_diff_kernel, lam_init=lam_init, need_ctx=need_ctx),
        grid=(nb, DIFF_HEADS // DIFF_GROUP, NT_TILES - t0),
        in_specs=[pl.BlockSpec((4, DIFF_DH), lambda b, h, t: (0, 0)),
                  pl.BlockSpec((1, TM, w), lambda b, h, t: (b, t + t0, h)),
                  pl.BlockSpec((1, T, w), lambda b, h, t: (b, 0, h)),
                  pl.BlockSpec((1, T, w), lambda b, h, t: (b, 0, h)),
                  pl.BlockSpec((1, w), lambda b, h, t: (0, h))],
        out_specs=pl.BlockSpec((1, TM, w), lambda b, h, t: (b, t, h)),
        out_shape=jax.ShapeDtypeStruct((nb, T - t0 * TM, DIFF_HEADS * 2 * DIFF_DH), BF16),
        compiler_params=_params("arbitrary", "arbitrary", "arbitrary"),
        name="diff_attn",
    )(lam_vec, dq, dk, dv, dg)


def _out_kernel(x_ref, yg_ref, yn_ref, yd_ref, mod_ref, w_ref, g_ref, xo_ref, ht_ref):
    attn = jnp.dot(yg_ref[0], w_ref[0:256, :], preferred_element_type=F32)
    attn = attn + jnp.dot(yn_ref[0], w_ref[256:512, :], preferred_element_type=F32)
    attn = attn + jnp.dot(yd_ref[0], w_ref[512:1024, :], preferred_element_type=F32)
    m = mod_ref[0]
    x = x_ref[0] + m[:, 2 * D:3 * D] * attn
    xo_ref[0] = x
    h2 = _rms(x) * g_ref[...] * (1.0 + m[:, 4 * D:5 * D]) + m[:, 3 * D:4 * D]
    ht_ref[...] = h2.T.astype(BF16)


def _out_proj(x, yg, yn, yd, mod, w_out, g2, need_ctx):
    nb = x.shape[0]
    t0 = 0 if need_ctx else 1
    nt = NT_TILES - t0
    full = lambda w: pl.BlockSpec((1, TM, w), lambda b, t: (b, t + t0, 0))
    tile = lambda w: pl.BlockSpec((1, TM, w), lambda b, t: (b, t, 0))
    return pl.pallas_call(
        _out_kernel,
        grid=(nb, nt),
        in_specs=[full(D), full(256), tile(256), tile(512), _mod_spec(nb, t0),
                  pl.BlockSpec((D, D), lambda b, t: (0, 0)), pl.BlockSpec((1, D), lambda b, t: (0, 0))],
        out_specs=[tile(D), pl.BlockSpec((D, TM), lambda b, t: (0, b * nt + t))],
        out_shape=[jax.ShapeDtypeStruct((nb, nt * TM, D), F32), jax.ShapeDtypeStruct((D, nb * nt * TM), BF16)],
        compiler_params=_params("arbitrary", "arbitrary"),
        name="out_proj",
    )(x, yg, yn, yd, mod, w_out, g2)


def _oddeven_pairs(n):
    pairs = []

    def merge(lo, hi, r):
        step = r * 2
        if step < hi - lo:
            merge(lo, hi, step)
            merge(lo + r, hi, step)
            pairs.extend((i, i + r) for i in range(lo + r, hi - r, step))
        else:
            pairs.append((lo, lo + r))

    def sort(lo, hi):
        if hi - lo >= 1:
            mid = lo + (hi - lo) // 2
            sort(lo, mid)
            sort(mid + 1, hi)
            merge(lo, hi, 1)

    sort(0, n - 1)
    return pairs


def _bitonic_pairs(n):
    pairs = []
    d = n // 2
    while d >= 1:
        pairs.extend((i, i + d) for i in range(n) if (i // d) % 2 == 0)
        d //= 2
    return pairs


_SORT16 = _oddeven_pairs(16)
_BITONIC16 = _bitonic_pairs(16)


def _compare_exchange(xs, pairs):
    for i, j in pairs:
        a, b = xs[i], xs[j]
        if b is None:
            continue
        if a is None:
            xs[i], xs[j] = b, None
        else:
            xs[i], xs[j] = jnp.maximum(a, b), jnp.minimum(a, b)


def _merge_top16(xs, ys):
    zs = []
    for k in range(16):
        a, b = xs[k], ys[15 - k]
        zs.append(a if b is None else (b if a is None else jnp.maximum(a, b)))
    _compare_exchange(zs, _BITONIC16)
    return zs


def _top16_of_128(x3):
    xs = [x3[r] for r in range(16)]
    _compare_exchange(xs, _SORT16)
    for shift in (4, 2, 1):
        xs = _merge_top16(xs, [pltpu.roll(x, shift, 0) for x in xs])
    return xs


def _peer_select(s1, s2):
    n = s1.shape[-1]
    a3 = s1.reshape(16, 8, n)
    b3 = s2.reshape(16, 8, n)
    t1 = _top16_of_128(a3)
    t2 = _top16_of_128(b3)
    cand = [[t1[a] + t2[b] for b in range(PEER_TOPK // (a + 1))] for a in range(PEER_TOPK)]
    pad = lambda xs: xs + [None] * (16 - len(xs))
    top = cand[0]
    for a in range(1, 8):
        top = _merge_top16(top, pad(cand[a]))
    top = _merge_top16(top, pad([cand[a][0] for a in range(8, 16)]))
    tau = top[15]
    cmax = cand[0][0]
    inf = jnp.full_like(tau, jnp.inf)
    z = jnp.zeros_like(tau)
    thr = []
    for a in range(PEER_TOPK):
        th = inf
        for b, cv in enumerate(cand[a]):
            sel = cv >= tau
            th = jnp.where(sel, jnp.minimum(th, t2[b]), th)
            z = z + jnp.where(sel, jnp.exp(cv - cmax), 0.0)
        thr.append(th)
    theta = jnp.full_like(a3, jnp.inf)
    for a in range(PEER_TOPK):
        theta = jnp.where(a3 == t1[a][None], thr[a][None], theta)
    c = jnp.exp(a3 - t1[0][None]) / z[None]
    e2 = jnp.exp(b3 - t2[0][None])
    return theta, c, e2.reshape(PEER_NKEYS, n)


def _peer_kernel(ht_ref, wq_ref, sk_ref, u_ref, vt_ref, o_ref, th_s, c_s, s2_s, e2_s, h8_s, a_s, w_s, acc_s):
    e = pl.program_id(1)
    tb = ht_ref.shape[1]

    @pl.when(e == 0)
    def _():
        qt = jnp.dot(wq_ref[...], ht_ref[...], preferred_element_type=F32)
        sk = sk_ref[...]
        for h in range(PEER_HEADS):
            s1 = jnp.dot(sk[0], qt[h * 128:h * 128 + 64], precision=HIGHEST, preferred_element_type=F32)
            s2 = jnp.dot(sk[1], qt[h * 128 + 64:h * 128 + 128], precision=HIGHEST, preferred_element_type=F32)
            theta, c, e2 = _peer_select(s1, s2)
            th_s[h] = theta
            c_s[h] = c * PEER_W_SCALE
            for lc in range(tb // 128):
                s2_s[h, lc] = s2[:, lc * 128:(lc + 1) * 128]
                e2_s[h, lc] = e2[:, lc * 128:(lc + 1) * 128]
        acc_s[...] = jnp.zeros_like(acc_s)
        h8_s[...] = jnp.clip(ht_ref[...], -F8_MAX, F8_MAX).astype(F8)

    a = jnp.dot(u_ref[...], h8_s[...], preferred_element_type=F32) * (1.0 / PEER_U_SCALE)
    for lc in range(tb // 128):
        a_s[lc] = a[:, lc * 128:(lc + 1) * 128]
    for lc in range(tb // 128):
        lanes = slice(lc * 128, (lc + 1) * 128)
        for i in range(PEER_ROWS):
            g = jnp.zeros((PEER_NKEYS, 128), F32)
            for h in range(PEER_HEADS):
                th = th_s[h, e * (PEER_ROWS // 8) + i // 8, i % 8:i % 8 + 1, lanes]
                cw = c_s[h, e * (PEER_ROWS // 8) + i // 8, i % 8:i % 8 + 1, lanes]
                g = g + jnp.where(s2_s[h, lc] >= th, e2_s[h, lc] * cw, 0.0)
            rows = slice(i * PEER_NKEYS, (i + 1) * PEER_NKEYS)
            w = g * jax.nn.gelu(a_s[lc, rows, :], approximate=True)
            w_s[rows, lanes] = jnp.minimum(w, F8_MAX).astype(F8)
    acc_s[...] += jnp.dot(vt_ref[...], w_s[...], preferred_element_type=F32)

    @pl.when(e == pl.num_programs(1) - 1)
    def _():
        o_ref[...] = acc_s[...].T * (1.0 / PEER_W_SCALE)


def _peer(ht, wq_t, sk, u, v_t):
    n = ht.shape[1]
    tb = PEER_TB
    ne = PEER_EXPERTS // PEER_EB
    sel = pltpu.VMEM((PEER_HEADS, tb // 128, PEER_NKEYS, 128), F32)
    sel8 = pltpu.VMEM((PEER_HEADS, PEER_NKEYS // 8, 8, tb), F32)
    return pl.pallas_call(
        _peer_kernel,
        grid=(n // tb, ne),
        in_specs=[pl.BlockSpec((D, tb), lambda t, e: (0, t)),
                  pl.BlockSpec((D, D), lambda t, e: (0, 0)),
                  pl.BlockSpec((2, PEER_NKEYS, PEER_HALF), lambda t, e: (0, 0, 0)),
                  pl.BlockSpec((PEER_EB, D), lambda t, e: (e, 0)),
                  pl.BlockSpec((D, PEER_EB), lambda t, e: (0, e))],
        out_specs=pl.BlockSpec((tb, D), lambda t, e: (t, 0)),
        out_shape=jax.ShapeDtypeStruct((n, D), F32),
        scratch_shapes=[sel8, sel8, sel, sel, pltpu.VMEM((D, tb), F8),
                        pltpu.VMEM((tb // 128, PEER_EB, 128), F32), pltpu.VMEM((PEER_EB, tb), F8),
                        pltpu.VMEM((D, tb), F32)],
        compiler_params=_params("arbitrary", "arbitrary"),
        name="peer",
    )(ht, wq_t, sk, u, v_t)


def _final_kernel(x_ref, pr_ref, mod_ref, g_ref, o_ref):
    x = x_ref[0] + mod_ref[0][:, 5 * D:6 * D] * pr_ref[0]
    o_ref[0] = _rms(x) * g_ref[...]


def _final(x, peer, mod, g):
    nb = x.shape[0]
    lat = pl.BlockSpec((1, TM, D), lambda b, t: (b, t, 0))
    return pl.pallas_call(
        _final_kernel,
        grid=(nb, SEQ // TM),
        in_specs=[lat, lat, pl.BlockSpec((1, 1, 6 * D), lambda b, t: (b, 0, 0)),
                  pl.BlockSpec((1, D), lambda b, t: (0, 0))],
        out_specs=lat,
        out_shape=jax.ShapeDtypeStruct((nb, SEQ, D), F32),
        compiler_params=_params("arbitrary", "arbitrary"),
        name="final_norm",
    )(x, peer, mod, g)


def _rope_tables():
    half = DIFF_DH // 4
    inv = ROPE_BASE ** (-jnp.arange(half, dtype=F32) / half)
    ang = jnp.arange(GRID_W, dtype=F32)[:, None] * inv[None, :]

    def table(fn, ctx_value):
        small = fn(ang)
        by_row = jnp.broadcast_to(small[:GRID_ROWS, None, :], (GRID_ROWS, GRID_W, half))
        by_col = jnp.broadcast_to(small[None, :, :], (GRID_ROWS, GRID_W, half))
        per_map = jnp.concatenate([by_row, by_row, by_col, by_col], axis=-1)
        lat = jnp.tile(per_map.reshape(SEQ, DIFF_DH), (1, 2 * DIFF_HEADS))
        return jnp.concatenate([jnp.full((CTX, lat.shape[1]), ctx_value, F32), lat], axis=0)

    return table(jnp.cos, 1.0), table(jnp.sin, 0.0)


def _rot_columns(w):
    q = DIFF_DH // 4
    w4 = w.reshape(w.shape[0], -1, 2, q)
    return jnp.stack([-w4[:, :, 1], w4[:, :, 0]], axis=2).reshape(w.shape)


def _regroup_w_in(w):
    sizes = (128, 128, 256, 256, 16, 16, 256, 256, 256, 512, 512, 512)
    pts = np.cumsum((0,) + sizes)
    p = [w[:, pts[i]:pts[i + 1]] for i in range(12)]
    downs = jnp.concatenate([p[4], p[5], jnp.zeros((D, 128 - 2 * GLA_RANK), w.dtype)], axis=1)
    cols = [p[0], p[1], p[2], p[3], downs, p[6], p[7], p[8], p[9], p[10], p[11], _rot_columns(p[9]), _rot_columns(p[10])]
    return jnp.concatenate(cols, axis=1).astype(BF16)


def _gate_up_padded(gate_up):
    z = jnp.zeros((128, 128), F32)
    gf = z.at[0:GLA_RANK].set(gate_up[0]).astype(BF16)
    gb = z.at[GLA_RANK:2 * GLA_RANK].set(gate_up[1]).astype(BF16)
    return gf, gb


def kernel(x, c, ctx, c_ctx, ada_w, ada_b, norm1_g, norm2_g, w_in, gla_gate_up, gla_gate_b, gla_norm_g, nat_rpb,
           diff_lambda, diff_norm_g, w_out, peer_wq, peer_subkeys, peer_u, peer_v, final_g):
    nb = x.shape[0]
    depth = ada_w.shape[0]
    assert x.shape[1:] == (SEQ, D) and ctx.shape[1:] == (CTX, D) and (nb * T) % PEER_TB == 0 and (nb * SEQ) % PEER_TB == 0

    rows = -(-(nb + 1) // 8) * 8
    cc = jnp.concatenate([c, c_ctx[None], jnp.zeros((rows - nb - 1, D), F32)], axis=0)
    mod_all = _modulation(cc, ada_w, ada_b)
    cos, sin = _rope_tables()

    u8 = _to_f8(peer_u * PEER_U_SCALE)
    v8t = jnp.swapaxes(lax.optimization_barrier(_to_f8(peer_v)), 1, 2)

    xa = jnp.concatenate([ctx, x], axis=1)
    peer = None
    mod_prev = None
    for l in range(depth):
        lam_init = 0.8 - 0.6 * math.exp(-0.3 * l)
        mod = mod_all[l].reshape(rows, 1, 6 * D)
        gf, gb = _gate_up_padded(gla_gate_up[l])
        xa, (gq, gk, ggf, ggb, gv, gr, nq, nk, nv, dq, dk, dv) = _norm_proj(
            xa, peer, mod_prev, mod, norm1_g[l][None], _regroup_w_in(w_in[l]), gf, gb, gla_gate_b[l], cos, sin)
        yg = _gla(gq, gk, ggf, ggb, gv, gr, gla_norm_g[l][None])
        need_ctx = l + 1 < depth
        yn = _nat(nq, nk, nv, _nat_bias_table(nat_rpb[l]), need_ctx)
        yd = _diff(dq, dk, dv, diff_lambda[l], diff_norm_g[l][None], lam_init, need_ctx)
        xa, ht = _out_proj(xa, yg, yn, yd, mod, w_out[l].astype(BF16), norm2_g[l][None], need_ctx)
        peer = _peer(ht, peer_wq[l].T.astype(BF16), peer_subkeys[l], u8[l], v8t[l]).reshape(xa.shape)
        mod_prev = mod
    return _final(xa, peer, mod_prev, final_g[None])
```

```python
import functools
import math

import numpy as np
import jax
import jax.numpy as jnp
from jax import lax
from jax.experimental import pallas as pl
from jax.experimental.pallas import tpu as pltpu

F32 = jnp.float32
BF16 = jnp.bfloat16
F8 = jnp.float8_e4m3fn
F8_MAX = 448.0
HIGHEST = lax.Precision.HIGHEST

D = 1024
SEQ = 2048
CTX = 256
T = SEQ + CTX
GRID_W = 64
GRID_ROWS = SEQ // GRID_W
EPS = 1e-6
TM = 256
NT_TILES = T // TM

GLA_HEADS, GLA_DK, GLA_DV, GLA_RANK, GLA_TAU, GLA_CHUNK = 4, 32, 64, 16, 16.0, 64
NAT_HEADS, NAT_DH, NAT_WIN_R, NAT_WIN_C = 4, 64, 8, 16
DIFF_HEADS, DIFF_DH = 4, 64
DIFF_GROUP = 4
ROPE_BASE = 10000.0
PEER_HEADS, PEER_NKEYS, PEER_HALF, PEER_TOPK = 8, 128, 64, 16
PEER_EXPERTS = PEER_NKEYS * PEER_NKEYS
PEER_TB = 512
PEER_ROWS = 16
PEER_EB = PEER_ROWS * PEER_NKEYS
PEER_U_SCALE = 32.0
PEER_W_SCALE = 4.0

C_GQ, C_GK, C_GV, C_GR, C_GD = 0, 128, 256, 512, 768
C_NQ, C_NK, C_NV = 896, 1152, 1408
C_DQ, C_DK, C_DV, C_DQR, C_DKR = 1664, 2176, 2688, 3200, 3712
PROJ_COLS = 4224

_NT_DIMS = (((1,), (1,)), ((), ()))


def _nt(a, b):
    return lax.dot_general(a, b, _NT_DIMS, preferred_element_type=F32)


def _params(*sem):
    return pltpu.CompilerParams(dimension_semantics=sem, vmem_limit_bytes=56 * 1024 * 1024)


def _to_f8(x):
    return jnp.clip(x, -F8_MAX, F8_MAX).astype(F8)


def _rms(x):
    return x * lax.rsqrt(jnp.mean(x * x, axis=-1, keepdims=True) + EPS)


def _sigmoid(x):
    return 1.0 / (1.0 + jnp.exp(-x))


def _log_sigmoid(x):
    return jnp.minimum(x, 0.0) - jnp.log1p(jnp.exp(-jnp.abs(x)))


def _mod_kernel(c_ref, w_ref, b_ref, o_ref):
    c = c_ref[...]
    o_ref[0] = jnp.dot(c * _sigmoid(c), w_ref[0], precision=HIGHEST, preferred_element_type=F32) + b_ref[0]


def _modulation(cc, ada_w, ada_b):
    depth = ada_w.shape[0]
    rows = cc.shape[0]
    return pl.pallas_call(
        _mod_kernel,
        grid=(depth, 6),
        in_specs=[pl.BlockSpec((rows, D), lambda l, j: (0, 0)),
                  pl.BlockSpec((1, D, D), lambda l, j: (l, 0, j)),
                  pl.BlockSpec((1, 1, D), lambda l, j: (l, 0, j))],
        out_specs=pl.BlockSpec((1, rows, D), lambda l, j: (l, 0, j)),
        out_shape=jax.ShapeDtypeStruct((depth, rows, 6 * D), F32),
        compiler_params=_params("arbitrary", "arbitrary"),
        name="adaln_mod",
    )(cc, ada_w, ada_b.reshape(depth, 1, 6 * D))


def _proj_kernel(*refs, fuse_res):
    if fuse_res:
        x_ref, pr_ref, modp_ref = refs[:3]
        refs = refs[3:]
    else:
        x_ref = refs[0]
        refs = refs[1:]
    (mod_ref, g_ref, w_ref, gf_ref, gb_ref, gbias_ref, cos_ref, sin_ref) = refs[:8]
    outs = refs[8:]
    if fuse_res:
        xo_ref = outs[0]
        outs = outs[1:]
    (gq_ref, gk_ref, ggf_ref, ggb_ref, gv_ref, gr_ref, nq_ref, nk_ref, nv_ref, dq_ref, dk_ref, dv_ref) = outs

    x = x_ref[0]
    if fuse_res:
        x = x + modp_ref[0][:, 5 * D:6 * D] * pr_ref[0]
        xo_ref[0] = x
    m = mod_ref[0]
    h = _rms(x) * g_ref[...] * (1.0 + m[:, D:2 * D]) + m[:, 0:D]
    p = jnp.dot(h.astype(BF16), w_ref[...], preferred_element_type=F32)

    gq_ref[0] = p[:, C_GQ:C_GQ + 128] * (GLA_DK ** -0.5)
    gk_ref[0] = p[:, C_GK:C_GK + 128]
    gv_ref[0] = p[:, C_GV:C_GV + 256]
    gr_ref[0] = p[:, C_GR:C_GR + 256]
    pd = p[:, C_GD:C_GD + 128].astype(BF16)
    gbias = gbias_ref[...]
    xf = jnp.dot(pd, gf_ref[...], preferred_element_type=F32) + gbias[0:1]
    xb = jnp.dot(pd, gb_ref[...], preferred_element_type=F32) + gbias[1:2]
    ggf_ref[0] = _log_sigmoid(xf) / GLA_TAU
    ggb_ref[0] = _log_sigmoid(xb) / GLA_TAU

    nq_ref[0] = (p[:, C_NQ:C_NQ + 256] * (NAT_DH ** -0.5)).astype(BF16)
    nk_ref[0] = p[:, C_NK:C_NK + 256].astype(BF16)
    nv_ref[0] = p[:, C_NV:C_NV + 256].astype(BF16)

    cos = cos_ref[...]
    sin = sin_ref[...]
    dq = p[:, C_DQ:C_DQ + 512] * cos + p[:, C_DQR:C_DQR + 512] * sin
    dq_ref[0] = (dq * (DIFF_DH ** -0.5)).astype(BF16)
    dk_ref[0] = (p[:, C_DK:C_DK + 512] * cos + p[:, C_DKR:C_DKR + 512] * sin).astype(BF16)
    dv_ref[0] = p[:, C_DV:C_DV + 512].astype(BF16)


def _mod_spec(nb, t0=0):
    return pl.BlockSpec((1, 1, 6 * D), lambda b, t: (jnp.where(t + t0 == 0, nb, b), 0, 0))


def _norm_proj(x, peer, mod_prev, mod, g1, w_all, gf, gb, gbias, cos, sin):
    nb = x.shape[0]
    fuse = peer is not None
    tile = lambda w: pl.BlockSpec((1, TM, w), lambda b, t: (b, t, 0))
    full = lambda a: pl.BlockSpec(a.shape, lambda b, t: (0,) * a.ndim)
    ins, specs = [x], [tile(D)]
    if fuse:
        ins += [peer, mod_prev]
        specs += [tile(D), _mod_spec(nb)]
    ins += [mod, g1, w_all, gf, gb, gbias, cos, sin]
    specs += [_mod_spec(nb), full(g1), full(w_all), full(gf), full(gb), full(gbias),
              pl.BlockSpec((TM, 512), lambda b, t: (t, 0)), pl.BlockSpec((TM, 512), lambda b, t: (t, 0))]
    widths = [(128, F32)] * 4 + [(256, F32)] * 2 + [(256, BF16)] * 3 + [(512, BF16)] * 3
    out_shape = [jax.ShapeDtypeStruct((nb, T, w), dt) for w, dt in widths]
    out_specs = [tile(w) for w, _ in widths]
    if fuse:
        out_shape = [jax.ShapeDtypeStruct((nb, T, D), F32)] + out_shape
        out_specs = [tile(D)] + out_specs
    res = pl.pallas_call(
        functools.partial(_proj_kernel, fuse_res=fuse),
        grid=(nb, NT_TILES),
        in_specs=specs, out_specs=out_specs, out_shape=out_shape,
        compiler_params=_params("arbitrary", "arbitrary"),
        name="norm_proj",
    )(*ins)
    if fuse:
        return res[0], res[1:]
    return x, res


def _gla_kernel(q_ref, k_ref, gf_ref, gb_ref, v_ref, r_ref, ng_ref, o_ref, of_s, ob_s, sf_s, sb_s):
    C = GLA_CHUNK
    nch = T // C
    nctx = CTX // C
    ii = lax.broadcasted_iota(jnp.int32, (C, C), 0)
    jj = lax.broadcasted_iota(jnp.int32, (C, C), 1)
    tri_f = (jj <= ii).astype(F32)
    tri_b = (jj >= ii).astype(F32)
    it = lax.broadcasted_iota(jnp.int32, (C, 4 * C), 0)
    jt = lax.broadcasted_iota(jnp.int32, (C, 4 * C), 1) & (C - 1)
    cm_f = jt <= it
    cm_b = jt >= it
    kmask = ((lax.broadcasted_iota(jnp.int32, (4 * C, 128), 0) >> 6)
             == (lax.broadcasted_iota(jnp.int32, (4 * C, 128), 1) >> 5)).astype(F32)
    vmask = ((lax.broadcasted_iota(jnp.int32, (4 * C, 256), 0) >> 6)
             == (lax.broadcasted_iota(jnp.int32, (4 * C, 256), 1) >> 6)).astype(F32)
    smask = ((lax.broadcasted_iota(jnp.int32, (256, 128), 0) >> 6)
             == (lax.broadcasted_iota(jnp.int32, (256, 128), 1) >> 5)).astype(F32)

    sf_s[...] = jnp.zeros_like(sf_s)
    sb_s[...] = jnp.zeros_like(sb_s)

    def chunk(off, g_ref, s_ref, dst, tri, cm, last, mid):
        rows = pl.ds(off, C)
        q = q_ref[0, rows, :]
        k = k_ref[0, rows, :]
        v = v_ref[0, rows, :]
        g = g_ref[0, rows, :]
        st = s_ref[...]
        b = jnp.dot(tri, g, precision=HIGHEST, preferred_element_type=F32)
        b_last = b[last:last + 1]
        b_mid = b[mid:mid + 1]
        qi = q * jnp.exp(jnp.clip(b - b_mid, -80.0, 80.0))
        ki = k * jnp.exp(jnp.clip(b_mid - b, -80.0, 80.0))
        krows = jnp.concatenate([ki, ki, ki, ki], axis=0) * kmask
        a = jnp.where(cm, _nt(qi.astype(BF16), krows.astype(BF16)), 0.0)
        vblk = jnp.concatenate([v, v, v, v], axis=0) * vmask
        o = jnp.dot(a.astype(BF16), vblk.astype(BF16), preferred_element_type=F32)
        o = o + _nt((q * jnp.exp(b)).astype(BF16), st.astype(BF16))
        dst[rows, :] = o
        kd = k * jnp.exp(b_last - b)
        upd = jnp.dot(v.T.astype(BF16), kd.astype(BF16), preferred_element_type=F32)
        s_ref[...] = st * jnp.exp(b_last) + upd * smask

    def body(t, carry):
        cb = jnp.where(t < nctx, nctx - 1 - t, nch + nctx - 1 - t)
        chunk(pl.multiple_of(t * C, C), gf_ref, sf_s, of_s, tri_f, cm_f, C - 1, C // 2 - 1)
        chunk(pl.multiple_of(cb * C, C), gb_ref, sb_s, ob_s, tri_b, cm_b, 0, C // 2)
        return carry

    lax.fori_loop(0, nch, body, 0, unroll=4)

    hones = ((lax.broadcasted_iota(jnp.int32, (256, 256), 0) >> 6)
             == (lax.broadcasted_iota(jnp.int32, (256, 256), 1) >> 6)).astype(F32) * (1.0 / GLA_DV)
    ng = ng_ref[...]
    for t in range(NT_TILES):
        rows = pl.ds(t * TM, TM)
        o = of_s[rows, :] + ob_s[rows, :]
        ms = jnp.dot(o * o, hones, precision=HIGHEST, preferred_element_type=F32)
        r = r_ref[0, rows, :]
        o_ref[0, rows, :] = (o * lax.rsqrt(ms + EPS) * ng * (r * _sigmoid(r))).astype(BF16)


def _gla(gq, gk, ggf, ggb, gv, gr, ng):
    nb = gq.shape[0]
    seq = lambda w: pl.BlockSpec((1, T, w), lambda b: (b, 0, 0))
    return pl.pallas_call(
        _gla_kernel,
        grid=(nb,),
        in_specs=[seq(128), seq(128), seq(128), seq(128), seq(256), seq(256),
                  pl.BlockSpec((1, 256), lambda b: (0, 0))],
        out_specs=seq(256),
        out_shape=jax.ShapeDtypeStruct((nb, T, 256), BF16),
        scratch_shapes=[pltpu.VMEM((T, 256), F32), pltpu.VMEM((T, 256), F32),
                        pltpu.VMEM((256, 128), F32), pltpu.VMEM((256, 128), F32)],
        compiler_params=_params("arbitrary"),
        name="gla",
    )(gq, gk, ggf, ggb, gv, gr, ng)


def _stacked_heads(q, n):
    rows = lax.broadcasted_iota(jnp.int32, (NAT_HEADS * n, 256), 0)
    lanes = lax.broadcasted_iota(jnp.int32, (NAT_HEADS * n, 256), 1) >> 6
    head = jnp.zeros_like(rows)
    for h in range(1, NAT_HEADS):
        head = head + (rows >= h * n).astype(jnp.int32)
    own = head == lanes
    q4 = jnp.concatenate([q] * NAT_HEADS, axis=0)
    return jnp.where(own, q4, jnp.zeros_like(q4)), own


def _unstack_heads(pv, own, n):
    pv = jnp.where(own, pv, 0.0)
    out = pv[0:n]
    for h in range(1, NAT_HEADS):
        out = out + pv[h * n:(h + 1) * n]
    return out


def _nat_kernel(q_ref, k_ref, v_ref, bias_ref, o_ref, *, need_ctx):
    out0 = CTX if need_ctx else 0
    if need_ctx:
        kc = k_ref[0, 0:CTX, :]
        vc = v_ref[0, 0:CTX, :]
        q4, own = _stacked_heads(q_ref[0, 0:CTX, :], CTX)
        s = _nt(q4, kc)
        e = jnp.exp(s - jnp.max(s, axis=-1, keepdims=True))
        p = (e * (1.0 / jnp.sum(e, axis=-1, keepdims=True))).astype(BF16)
        o_ref[0, 0:CTX, :] = _unstack_heads(jnp.dot(p, vc, preferred_element_type=F32), own, CTX).astype(BF16)

    nwin = NAT_WIN_R * GRID_W

    def row(r, carry):
        rs = jnp.clip(r - NAT_WIN_R // 2, 0, GRID_ROWS - NAT_WIN_R)
        case = rs - r + NAT_WIN_R - 1
        qrows = pl.ds(pl.multiple_of(CTX + r * GRID_W, GRID_W), GRID_W)
        wrows = pl.ds(pl.multiple_of(CTX + rs * GRID_W, GRID_W), nwin)
        q4, own = _stacked_heads(q_ref[0, qrows, :], GRID_W)
        sw = _nt(q4, k_ref[0, wrows, :]) + bias_ref[case]
        sc = _nt(q4, k_ref[0, 0:CTX, :])
        m = jnp.maximum(jnp.max(sw, axis=-1, keepdims=True), jnp.max(sc, axis=-1, keepdims=True))
        ew = jnp.exp(sw - m)
        ec = jnp.exp(sc - m)
        inv = 1.0 / (jnp.sum(ew, axis=-1, keepdims=True) + jnp.sum(ec, axis=-1, keepdims=True))
        pv = jnp.dot((ew * inv).astype(BF16), v_ref[0, wrows, :], preferred_element_type=F32)
        pv = pv + jnp.dot((ec * inv).astype(BF16), v_ref[0, 0:CTX, :], preferred_element_type=F32)
        orows = pl.ds(pl.multiple_of(out0 + r * GRID_W, GRID_W), GRID_W)
        o_ref[0, orows, :] = _unstack_heads(pv, own, GRID_W).astype(BF16)
        return carry

    lax.fori_loop(0, GRID_ROWS, row, 0, unroll=8)


def _nat(nq, nk, nv, bias, need_ctx):
    nb = nq.shape[0]
    rows_out = T if need_ctx else SEQ
    seq = pl.BlockSpec((1, T, 256), lambda b: (b, 0, 0))
    return pl.pallas_call(
        functools.partial(_nat_kernel, need_ctx=need_ctx),
        grid=(nb,),
        in_specs=[seq, seq, seq, pl.BlockSpec(bias.shape, lambda b: (0, 0, 0))],
        out_specs=pl.BlockSpec((1, rows_out, 256), lambda b: (b, 0, 0)),
        out_shape=jax.ShapeDtypeStruct((nb, rows_out, 256), BF16),
        compiler_params=_params("arbitrary"),
        name="nat",
    )(nq, nk, nv, bias)


def _nat_bias_table(rpb):
    cidx = np.arange(GRID_W)
    col_start = np.clip(cidx - NAT_WIN_C // 2, 0, GRID_W - NAT_WIN_C)
    col_mask = (cidx[None, :] >= col_start[:, None]) & (cidx[None, :] < col_start[:, None] + NAT_WIN_C)
    dc = np.clip(cidx[None, :] - cidx[:, None] + NAT_WIN_C - 1, 0, 2 * NAT_WIN_C - 2)
    dr = np.arange(NAT_WIN_R)[:, None] + np.arange(NAT_WIN_R)[None, :]
    pick_r = np.eye(2 * NAT_WIN_R - 1, dtype=np.float32)[dr]
    pick_c = np.eye(2 * NAT_WIN_C - 1, dtype=np.float32)[dc]
    t = jnp.einsum('hab,cja,qkb->hcjqk', rpb.astype(F32), pick_r, pick_c, precision=HIGHEST)
    t = jnp.where(col_mask[None, None, None], t, -jnp.inf)
    return t.transpose(1, 0, 3, 2, 4).reshape(NAT_WIN_R, NAT_HEADS * GRID_W, NAT_WIN_R * GRID_W)


def _diff_kernel(lam_ref, q_ref, k_ref, v_ref, g_ref, o_ref, *, lam_init, need_ctx):
    lv = lam_ref[...]
    lam = (jnp.exp(jnp.sum(lv[0:1] * lv[1:2], axis=-1, keepdims=True))
           - jnp.exp(jnp.sum(lv[2:3] * lv[3:4], axis=-1, keepdims=True)) + lam_init)
    w = 2 * DIFF_DH
    first = lax.broadcasted_iota(jnp.int32, (1, w), 1) < DIFF_DH

    def attend(nkeys):
        for hh in range(DIFF_GROUP):
            cols = slice(hh * w, (hh + 1) * w)
            q = q_ref[0, :, cols]
            k = k_ref[0, 0:nkeys, cols]
            v = v_ref[0, 0:nkeys, cols]
            s1 = _nt(jnp.where(first, q, jnp.zeros_like(q)), k)
            s2 = _nt(jnp.where(first, jnp.zeros_like(q), q), k)
            e1 = jnp.exp(s1 - jnp.max(s1, axis=-1, keepdims=True))
            e2 = jnp.exp(s2 - jnp.max(s2, axis=-1, keepdims=True))
            w1 = 1.0 / jnp.sum(e1, axis=-1, keepdims=True)
            w2 = lam / jnp.sum(e2, axis=-1, keepdims=True)
            p = (e1 * w1 - e2 * w2).astype(BF16)
            o = jnp.dot(p, v, preferred_element_type=F32)
            o_ref[0, :, cols] = (_rms(o) * g_ref[:, cols] * (1.0 - lam_init)).astype(BF16)

    if need_ctx:
        @pl.when(pl.program_id(2) == 0)
        def _():
            attend(CTX)

        @pl.when(pl.program_id(2) > 0)
        def _():
            attend(T)
    else:
        attend(T)


def _diff(dq, dk, dv, lam_vec, dg, lam_init, need_ctx):
    nb = dq.shape[0]
    w = DIFF_GROUP * 2 * DIFF_DH
    t0 = 0 if need_ctx else 1
    return pl.pallas_call(
        functools.partial(_diff_kernel, lam_init=lam_init, need_ctx=need_ctx),
        grid=(nb, DIFF_HEADS // DIFF_GROUP, NT_TILES - t0),
        in_specs=[pl.BlockSpec((4, DIFF_DH), lambda b, h, t: (0, 0)),
                  pl.BlockSpec((1, TM, w), lambda b, h, t: (b, t + t0, h)),
                  pl.BlockSpec((1, T, w), lambda b, h, t: (b, 0, h)),
                  pl.BlockSpec((1, T, w), lambda b, h, t: (b, 0, h)),
                  pl.BlockSpec((1, w), lambda b, h, t: (0, h))],
        out_specs=pl.BlockSpec((1, TM, w), lambda b, h, t: (b, t, h)),
        out_shape=jax.ShapeDtypeStruct((nb, T - t0 * TM, DIFF_HEADS * 2 * DIFF_DH), BF16),
        compiler_params=_params("arbitrary", "arbitrary", "arbitrary"),
        name="diff_attn",
    )(lam_vec, dq, dk, dv, dg)


def _out_kernel(x_ref, yg_ref, yn_ref, yd_ref, mod_ref, w_ref, g_ref, xo_ref, ht_ref):
    attn = jnp.dot(yg_ref[0], w_ref[0:256, :], preferred_element_type=F32)
    attn = attn + jnp.dot(yn_ref[0], w_ref[256:512, :], preferred_element_type=F32)
    attn = attn + jnp.dot(yd_ref[0], w_ref[512:1024, :], preferred_element_type=F32)
    m = mod_ref[0]
    x = x_ref[0] + m[:, 2 * D:3 * D] * attn
    xo_ref[0] = x
    h2 = _rms(x) * g_ref[...] * (1.0 + m[:, 4 * D:5 * D]) + m[:, 3 * D:4 * D]
    ht_ref[...] = h2.T.astype(BF16)


def _out_proj(x, yg, yn, yd, mod, w_out, g2, need_ctx):
    nb = x.shape[0]
    t0 = 0 if need_ctx else 1
    nt = NT_TILES - t0
    full = lambda w: pl.BlockSpec((1, TM, w), lambda b, t: (b, t + t0, 0))
    tile = lambda w: pl.BlockSpec((1, TM, w), lambda b, t: (b, t, 0))
    return pl.pallas_call(
        _out_kernel,
        grid=(nb, nt),
        in_specs=[full(D), full(256), tile(256), tile(512), _mod_spec(nb, t0),
                  pl.BlockSpec((D, D), lambda b, t: (0, 0)), pl.BlockSpec((1, D), lambda b, t: (0, 0))],
        out_specs=[tile(D), pl.BlockSpec((D, TM), lambda b, t: (0, b * nt + t))],
        out_shape=[jax.ShapeDtypeStruct((nb, nt * TM, D), F32), jax.ShapeDtypeStruct((D, nb * nt * TM), BF16)],
        compiler_params=_params("arbitrary", "arbitrary"),
        name="out_proj",
    )(x, yg, yn, yd, mod, w_out, g2)


def _oddeven_pairs(n):
    pairs = []

    def merge(lo, hi, r):
        step = r * 2
        if step < hi - lo:
            merge(lo, hi, step)
            merge(lo + r, hi, step)
            pairs.extend((i, i + r) for i in range(lo + r, hi - r, step))
        else:
            pairs.append((lo, lo + r))

    def sort(lo, hi):
        if hi - lo >= 1:
            mid = lo + (hi - lo) // 2
            sort(lo, mid)
            sort(mid + 1, hi)
            merge(lo, hi, 1)

    sort(0, n - 1)
    return pairs


def _bitonic_pairs(n):
    pairs = []
    d = n // 2
    while d >= 1:
        pairs.extend((i, i + d) for i in range(n) if (i // d) % 2 == 0)
        d //= 2
    return pairs


_SORT16 = _oddeven_pairs(16)
_BITONIC16 = _bitonic_pairs(16)


def _compare_exchange(xs, pairs):
    for i, j in pairs:
        a, b = xs[i], xs[j]
        if b is None:
            continue
        if a is None:
            xs[i], xs[j] = b, None
        else:
            xs[i], xs[j] = jnp.maximum(a, b), jnp.minimum(a, b)


def _merge_top16(xs, ys):
    zs = []
    for k in range(16):
        a, b = xs[k], ys[15 - k]
        zs.append(a if b is None else (b if a is None else jnp.maximum(a, b)))
    _compare_exchange(zs, _BITONIC16)
    return zs


def _top16_of_128(x3):
    xs = [x3[r] for r in range(16)]
    _compare_exchange(xs, _SORT16)
    for shift in (4, 2, 1):
        xs = _merge_top16(xs, [pltpu.roll(x, shift, 0) for x in xs])
    return xs


def _peer_select(s1, s2):
    n = s1.shape[-1]
    a3 = s1.reshape(16, 8, n)
    b3 = s2.reshape(16, 8, n)
    t1 = _top16_of_128(a3)
    t2 = _top16_of_128(b3)
    cand = [[t1[a] + t2[b] for b in range(PEER_TOPK // (a + 1))] for a in range(PEER_TOPK)]
    pad = lambda xs: xs + [None] * (16 - len(xs))
    top = cand[0]
    for a in range(1, 8):
        top = _merge_top16(top, pad(cand[a]))
    top = _merge_top16(top, pad([cand[a][0] for a in range(8, 16)]))
    tau = top[15]
    cmax = cand[0][0]
    inf = jnp.full_like(tau, jnp.inf)
    z = jnp.zeros_like(tau)
    thr = []
    for a in range(PEER_TOPK):
        th = inf
        for b, cv in enumerate(cand[a]):
            sel = cv >= tau
            th = jnp.where(sel, jnp.minimum(th, t2[b]), th)
            z = z + jnp.where(sel, jnp.exp(cv - cmax), 0.0)
        thr.append(th)
    theta = jnp.full_like(a3, jnp.inf)
    for a in range(PEER_TOPK):
        theta = jnp.where(a3 == t1[a][None], thr[a][None], theta)
    c = jnp.exp(a3 - t1[0][None]) / z[None]
    e2 = jnp.exp(b3 - t2[0][None])
    return theta, c, e2.reshape(PEER_NKEYS, n)


def _peer_kernel(ht_ref, wq_ref, sk_ref, u_ref, vt_ref, o_ref, th_s, c_s, s2_s, e2_s, h8_s, a_s, w_s, acc_s):
    e = pl.program_id(1)
    tb = ht_ref.shape[1]

    @pl.when(e == 0)
    def _():
        qt = jnp.dot(wq_ref[...], ht_ref[...], preferred_element_type=F32)
        sk = sk_ref[...]
        for h in range(PEER_HEADS):
            s1 = jnp.dot(sk[0], qt[h * 128:h * 128 + 64], precision=HIGHEST, preferred_element_type=F32)
            s2 = jnp.dot(sk[1], qt[h * 128 + 64:h * 128 + 128], precision=HIGHEST, preferred_element_type=F32)
            theta, c, e2 = _peer_select(s1, s2)
            th_s[h] = theta
            c_s[h] = c * PEER_W_SCALE
            for lc in range(tb // 128):
                s2_s[h, lc] = s2[:, lc * 128:(lc + 1) * 128]
                e2_s[h, lc] = e2[:, lc * 128:(lc + 1) * 128]
        acc_s[...] = jnp.zeros_like(acc_s)
        h8_s[...] = jnp.clip(ht_ref[...], -F8_MAX, F8_MAX).astype(F8)

    a = jnp.dot(u_ref[...], h8_s[...], preferred_element_type=F32)
    for lc in range(tb // 128):
        a_s[lc] = a[:, lc * 128:(lc + 1) * 128]
    k_half = 0.5 / PEER_U_SCALE
    k_lin = math.sqrt(2.0 / math.pi) / PEER_U_SCALE
    k_cub = math.sqrt(2.0 / math.pi) * 0.044715 / PEER_U_SCALE ** 3
    for lc in range(tb // 128):
        lanes = slice(lc * 128, (lc + 1) * 128)
        for i in range(PEER_ROWS):
            g = jnp.zeros((PEER_NKEYS, 128), F32)
            for h in range(PEER_HEADS):
                th = th_s[h, e * (PEER_ROWS // 8) + i // 8, i % 8:i % 8 + 1, lanes]
                cw = c_s[h, e * (PEER_ROWS // 8) + i // 8, i % 8:i % 8 + 1, lanes]
                g = g + jnp.where(s2_s[h, lc] >= th, e2_s[h, lc] * cw, 0.0)
            rows = slice(i * PEER_NKEYS, (i + 1) * PEER_NKEYS)
            xr = a_s[lc, rows, :]
            w = g * (k_half * xr * (1.0 + jnp.tanh(xr * (k_lin + k_cub * (xr * xr)))))
            w_s[rows, lanes] = jnp.minimum(w, F8_MAX).astype(F8)
    acc_s[...] += jnp.dot(vt_ref[...], w_s[...], preferred_element_type=F32)

    @pl.when(e == pl.num_programs(1) - 1)
    def _():
        o_ref[...] = acc_s[...].T * (1.0 / PEER_W_SCALE)


def _peer(ht, wq_t, sk, u, v_t):
    n = ht.shape[1]
    tb = PEER_TB
    ne = PEER_EXPERTS // PEER_EB
    sel = pltpu.VMEM((PEER_HEADS, tb // 128, PEER_NKEYS, 128), F32)
    sel8 = pltpu.VMEM((PEER_HEADS, PEER_NKEYS // 8, 8, tb), F32)
    return pl.pallas_call(
        _peer_kernel,
        grid=(n // tb, ne),
        in_specs=[pl.BlockSpec((D, tb), lambda t, e: (0, t)),
                  pl.BlockSpec((D, D), lambda t, e: (0, 0)),
                  pl.BlockSpec((2, PEER_NKEYS, PEER_HALF), lambda t, e: (0, 0, 0)),
                  pl.BlockSpec((PEER_EB, D), lambda t, e: (e, 0)),
                  pl.BlockSpec((D, PEER_EB), lambda t, e: (0, e))],
        out_specs=pl.BlockSpec((tb, D), lambda t, e: (t, 0)),
        out_shape=jax.ShapeDtypeStruct((n, D), F32),
        scratch_shapes=[sel8, sel8, sel, sel, pltpu.VMEM((D, tb), F8),
                        pltpu.VMEM((tb // 128, PEER_EB, 128), F32), pltpu.VMEM((PEER_EB, tb), F8),
                        pltpu.VMEM((D, tb), F32)],
        compiler_params=_params("arbitrary", "arbitrary"),
        name="peer",
    )(ht, wq_t, sk, u, v_t)


def _final_kernel(x_ref, pr_ref, mod_ref, g_ref, o_ref):
    x = x_ref[0] + mod_ref[0][:, 5 * D:6 * D] * pr_ref[0]
    o_ref[0] = _rms(x) * g_ref[...]


def _final(x, peer, mod, g):
    nb = x.shape[0]
    lat = pl.BlockSpec((1, TM, D), lambda b, t: (b, t, 0))
    return pl.pallas_call(
        _final_kernel,
        grid=(nb, SEQ // TM),
        in_specs=[lat, lat, pl.BlockSpec((1, 1, 6 * D), lambda b, t: (b, 0, 0)),
                  pl.BlockSpec((1, D), lambda b, t: (0, 0))],
        out_specs=lat,
        out_shape=jax.ShapeDtypeStruct((nb, SEQ, D), F32),
        compiler_params=_params("arbitrary", "arbitrary"),
        name="final_norm",
    )(x, peer, mod, g)


def _rope_tables():
    half = DIFF_DH // 4
    inv = ROPE_BASE ** (-jnp.arange(half, dtype=F32) / half)
    ang = jnp.arange(GRID_W, dtype=F32)[:, None] * inv[None, :]

    def table(fn, ctx_value):
        small = fn(ang)
        by_row = jnp.broadcast_to(small[:GRID_ROWS, None, :], (GRID_ROWS, GRID_W, half))
        by_col = jnp.broadcast_to(small[None, :, :], (GRID_ROWS, GRID_W, half))
        per_map = jnp.concatenate([by_row, by_row, by_col, by_col], axis=-1)
        lat = jnp.tile(per_map.reshape(SEQ, DIFF_DH), (1, 2 * DIFF_HEADS))
        return jnp.concatenate([jnp.full((CTX, lat.shape[1]), ctx_value, F32), lat], axis=0)

    return table(jnp.cos, 1.0), table(jnp.sin, 0.0)


def _rot_columns(w):
    q = DIFF_DH // 4
    w4 = w.reshape(w.shape[0], -1, 2, q)
    return jnp.stack([-w4[:, :, 1], w4[:, :, 0]], axis=2).reshape(w.shape)


def _regroup_w_in(w):
    sizes = (128, 128, 256, 256, 16, 16, 256, 256, 256, 512, 512, 512)
    pts = np.cumsum((0,) + sizes)
    p = [w[:, pts[i]:pts[i + 1]] for i in range(12)]
    downs = jnp.concatenate([p[4], p[5], jnp.zeros((D, 128 - 2 * GLA_RANK), w.dtype)], axis=1)
    cols = [p[0], p[1], p[2], p[3], downs, p[6], p[7], p[8], p[9], p[10], p[11], _rot_columns(p[9]), _rot_columns(p[10])]
    return jnp.concatenate(cols, axis=1).astype(BF16)


def _gate_up_padded(gate_up):
    z = jnp.zeros((128, 128), F32)
    gf = z.at[0:GLA_RANK].set(gate_up[0]).astype(BF16)
    gb = z.at[GLA_RANK:2 * GLA_RANK].set(gate_up[1]).astype(BF16)
    return gf, gb


def kernel(x, c, ctx, c_ctx, ada_w, ada_b, norm1_g, norm2_g, w_in, gla_gate_up, gla_gate_b, gla_norm_g, nat_rpb,
           diff_lambda, diff_norm_g, w_out, peer_wq, peer_subkeys, peer_u, peer_v, final_g):
    nb = x.shape[0]
    depth = ada_w.shape[0]
    assert x.shape[1:] == (SEQ, D) and ctx.shape[1:] == (CTX, D) and (nb * T) % PEER_TB == 0 and (nb * SEQ) % PEER_TB == 0

    rows = -(-(nb + 1) // 8) * 8
    cc = jnp.concatenate([c, c_ctx[None], jnp.zeros((rows - nb - 1, D), F32)], axis=0)
    mod_all = _modulation(cc, ada_w, ada_b)
    cos, sin = _rope_tables()

    u8 = _to_f8(peer_u * PEER_U_SCALE)
    v8t = jnp.swapaxes(lax.optimization_barrier(_to_f8(peer_v)), 1, 2)

    xa = jnp.concatenate([ctx, x], axis=1)
    peer = None
    mod_prev = None
    for l in range(depth):
        lam_init = 0.8 - 0.6 * math.exp(-0.3 * l)
        mod = mod_all[l].reshape(rows, 1, 6 * D)
        gf, gb = _gate_up_padded(gla_gate_up[l])
        xa, (gq, gk, ggf, ggb, gv, gr, nq, nk, nv, dq, dk, dv) = _norm_proj(
            xa, peer, mod_prev, mod, norm1_g[l][None], _regroup_w_in(w_in[l]), gf, gb, gla_gate_b[l], cos, sin)
        yg = _gla(gq, gk, ggf, ggb, gv, gr, gla_norm_g[l][None])
        need_ctx = l + 1 < depth
        yn = _nat(nq, nk, nv, _nat_bias_table(nat_rpb[l]), need_ctx)
        yd = _diff(dq, dk, dv, diff_lambda[l], diff_norm_g[l][None], lam_init, need_ctx)
        xa, ht = _out_proj(xa, yg, yn, yd, mod, w_out[l].astype(BF16), norm2_g[l][None], need_ctx)
        peer = _peer(ht, peer_wq[l].T.astype(BF16), peer_subkeys[l], u8[l], v8t[l]).reshape(xa.shape)
        mod_prev = mod
    return _final(xa, peer, mod_prev, final_g[None])
```

```python
import functools
import math

import numpy as np
import jax
import jax.numpy as jnp
from jax import lax
from jax.experimental import pallas as pl
from jax.experimental.pallas import tpu as pltpu

F32 = jnp.float32
BF16 = jnp.bfloat16
F8 = jnp.float8_e4m3fn
F8_MAX = 448.0
HIGHEST = lax.Precision.HIGHEST

D = 1024
SEQ = 2048
CTX = 256
T = SEQ + CTX
GRID_W = 64
GRID_ROWS = SEQ // GRID_W
EPS = 1e-6
TM = 256
NT_TILES = T // TM

GLA_HEADS, GLA_DK, GLA_DV, GLA_RANK, GLA_TAU, GLA_CHUNK = 4, 32, 64, 16, 16.0, 64
NAT_HEADS, NAT_DH, NAT_WIN_R, NAT_WIN_C = 4, 64, 8, 16
DIFF_HEADS, DIFF_DH = 4, 64
DIFF_GROUP = 4
ROPE_BASE = 10000.0
PEER_HEADS, PEER_NKEYS, PEER_HALF, PEER_TOPK = 8, 128, 64, 16
PEER_EXPERTS = PEER_NKEYS * PEER_NKEYS
PEER_TB = 512
PEER_ROWS = 16
PEER_EB = PEER_ROWS * PEER_NKEYS
PEER_U_SCALE = 32.0
PEER_W_SCALE = 4.0

C_GQ, C_GK, C_GV, C_GR, C_GD = 0, 128, 256, 512, 768
C_NQ, C_NK, C_NV = 896, 1152, 1408
C_DQ, C_DK, C_DV, C_DQR, C_DKR = 1664, 2176, 2688, 3200, 3712
PROJ_COLS = 4224

_NT_DIMS = (((1,), (1,)), ((), ()))


def _nt(a, b):
    return lax.dot_general(a, b, _NT_DIMS, preferred_element_type=F32)


def _params(*sem):
    return pltpu.CompilerParams(dimension_semantics=sem, vmem_limit_bytes=56 * 1024 * 1024)


def _to_f8(x):
    return jnp.clip(x, -F8_MAX, F8_MAX).astype(F8)


def _rms(x):
    return x * lax.rsqrt(jnp.mean(x * x, axis=-1, keepdims=True) + EPS)


def _sigmoid(x):
    return 1.0 / (1.0 + jnp.exp(-x))


def _log_sigmoid(x):
    return jnp.minimum(x, 0.0) - jnp.log1p(jnp.exp(-jnp.abs(x)))


def _mod_kernel(c_ref, w_ref, b_ref, o_ref):
    c = c_ref[...]
    o_ref[0] = jnp.dot(c * _sigmoid(c), w_ref[0], precision=HIGHEST, preferred_element_type=F32) + b_ref[0]


def _modulation(cc, ada_w, ada_b):
    depth = ada_w.shape[0]
    rows = cc.shape[0]
    return pl.pallas_call(
        _mod_kernel,
        grid=(depth, 6),
        in_specs=[pl.BlockSpec((rows, D), lambda l, j: (0, 0)),
                  pl.BlockSpec((1, D, D), lambda l, j: (l, 0, j)),
                  pl.BlockSpec((1, 1, D), lambda l, j: (l, 0, j))],
        out_specs=pl.BlockSpec((1, rows, D), lambda l, j: (l, 0, j)),
        out_shape=jax.ShapeDtypeStruct((depth, rows, 6 * D), F32),
        compiler_params=_params("arbitrary", "arbitrary"),
        name="adaln_mod",
    )(cc, ada_w, ada_b.reshape(depth, 1, 6 * D))


def _proj_kernel(*refs, fuse_res):
    if fuse_res:
        x_ref, pr_ref, modp_ref = refs[:3]
        refs = refs[3:]
    else:
        x_ref = refs[0]
        refs = refs[1:]
    (mod_ref, g_ref, w_ref, gf_ref, gb_ref, gbias_ref, cos_ref, sin_ref) = refs[:8]
    outs = refs[8:]
    if fuse_res:
        xo_ref = outs[0]
        outs = outs[1:]
    (gq_ref, gk_ref, ggf_ref, ggb_ref, gv_ref, gr_ref, nq_ref, nk_ref, nv_ref, dq_ref, dk_ref, dv_ref) = outs

    x = x_ref[0]
    if fuse_res:
        x = x + modp_ref[0][:, 5 * D:6 * D] * pr_ref[0]
        xo_ref[0] = x
    m = mod_ref[0]
    h = _rms(x) * g_ref[...] * (1.0 + m[:, D:2 * D]) + m[:, 0:D]
    p = jnp.dot(h.astype(BF16), w_ref[...], preferred_element_type=F32)

    gq_ref[0] = p[:, C_GQ:C_GQ + 128] * (GLA_DK ** -0.5)
    gk_ref[0] = p[:, C_GK:C_GK + 128]
    gv_ref[0] = p[:, C_GV:C_GV + 256]
    gr_ref[0] = p[:, C_GR:C_GR + 256]
    pd = p[:, C_GD:C_GD + 128].astype(BF16)
    gbias = gbias_ref[...]
    xf = jnp.dot(pd, gf_ref[...], preferred_element_type=F32) + gbias[0:1]
    xb = jnp.dot(pd, gb_ref[...], preferred_element_type=F32) + gbias[1:2]
    ggf_ref[0] = _log_sigmoid(xf) / GLA_TAU
    ggb_ref[0] = _log_sigmoid(xb) / GLA_TAU

    nq_ref[0] = (p[:, C_NQ:C_NQ + 256] * (NAT_DH ** -0.5)).astype(BF16)
    nk_ref[0] = p[:, C_NK:C_NK + 256].astype(BF16)
    nv_ref[0] = p[:, C_NV:C_NV + 256].astype(BF16)

    cos = cos_ref[...]
    sin = sin_ref[...]
    dq = p[:, C_DQ:C_DQ + 512] * cos + p[:, C_DQR:C_DQR + 512] * sin
    dq_ref[0] = (dq * (DIFF_DH ** -0.5)).astype(BF16)
    dk_ref[0] = (p[:, C_DK:C_DK + 512] * cos + p[:, C_DKR:C_DKR + 512] * sin).astype(BF16)
    dv_ref[0] = p[:, C_DV:C_DV + 512].astype(BF16)


def _mod_spec(nb, t0=0):
    return pl.BlockSpec((1, 1, 6 * D), lambda b, t: (jnp.where(t + t0 == 0, nb, b), 0, 0))


def _norm_proj(x, peer, mod_prev, mod, g1, w_all, gf, gb, gbias, cos, sin):
    nb = x.shape[0]
    fuse = peer is not None
    tile = lambda w: pl.BlockSpec((1, TM, w), lambda b, t: (b, t, 0))
    full = lambda a: pl.BlockSpec(a.shape, lambda b, t: (0,) * a.ndim)
    ins, specs = [x], [tile(D)]
    if fuse:
        ins += [peer, mod_prev]
        specs += [tile(D), _mod_spec(nb)]
    ins += [mod, g1, w_all, gf, gb, gbias, cos, sin]
    specs += [_mod_spec(nb), full(g1), full(w_all), full(gf), full(gb), full(gbias),
              pl.BlockSpec((TM, 512), lambda b, t: (t, 0)), pl.BlockSpec((TM, 512), lambda b, t: (t, 0))]
    widths = [(128, F32)] * 4 + [(256, F32)] * 2 + [(256, BF16)] * 3 + [(512, BF16)] * 3
    out_shape = [jax.ShapeDtypeStruct((nb, T, w), dt) for w, dt in widths]
    out_specs = [tile(w) for w, _ in widths]
    if fuse:
        out_shape = [jax.ShapeDtypeStruct((nb, T, D), F32)] + out_shape
        out_specs = [tile(D)] + out_specs
    res = pl.pallas_call(
        functools.partial(_proj_kernel, fuse_res=fuse),
        grid=(nb, NT_TILES),
        in_specs=specs, out_specs=out_specs, out_shape=out_shape,
        compiler_params=_params("arbitrary", "arbitrary"),
        name="norm_proj",
    )(*ins)
    if fuse:
        return res[0], res[1:]
    return x, res


def _gla_kernel(q_ref, k_ref, gf_ref, gb_ref, v_ref, r_ref, ng_ref, o_ref, of_s, ob_s, sf_s, sb_s):
    C = GLA_CHUNK
    nch = T // C
    nctx = CTX // C
    ii = lax.broadcasted_iota(jnp.int32, (C, C), 0)
    jj = lax.broadcasted_iota(jnp.int32, (C, C), 1)
    tri_f = (jj <= ii).astype(F32)
    tri_b = (jj >= ii).astype(F32)
    it = lax.broadcasted_iota(jnp.int32, (C, 4 * C), 0)
    jt = lax.broadcasted_iota(jnp.int32, (C, 4 * C), 1) & (C - 1)
    cm_f = jt <= it
    cm_b = jt >= it
    kmask = ((lax.broadcasted_iota(jnp.int32, (4 * C, 128), 0) >> 6)
             == (lax.broadcasted_iota(jnp.int32, (4 * C, 128), 1) >> 5)).astype(F32)
    vmask = ((lax.broadcasted_iota(jnp.int32, (4 * C, 256), 0) >> 6)
             == (lax.broadcasted_iota(jnp.int32, (4 * C, 256), 1) >> 6)).astype(F32)
    smask = ((lax.broadcasted_iota(jnp.int32, (256, 128), 0) >> 6)
             == (lax.broadcasted_iota(jnp.int32, (256, 128), 1) >> 5)).astype(F32)

    sf_s[...] = jnp.zeros_like(sf_s)
    sb_s[...] = jnp.zeros_like(sb_s)

    def chunk(off, g_ref, s_ref, dst, tri, cm, last, mid):
        rows = pl.ds(off, C)
        q = q_ref[0, rows, :]
        k = k_ref[0, rows, :]
        v = v_ref[0, rows, :]
        g = g_ref[0, rows, :]
        st = s_ref[...]
        b = jnp.dot(tri, g, precision=HIGHEST, preferred_element_type=F32)
        b_last = b[last:last + 1]
        b_mid = b[mid:mid + 1]
        qi = q * jnp.exp(jnp.clip(b - b_mid, -80.0, 80.0))
        ki = k * jnp.exp(jnp.clip(b_mid - b, -80.0, 80.0))
        krows = jnp.concatenate([ki, ki, ki, ki], axis=0) * kmask
        a = jnp.where(cm, _nt(qi.astype(BF16), krows.astype(BF16)), 0.0)
        vblk = jnp.concatenate([v, v, v, v], axis=0) * vmask
        o = jnp.dot(a.astype(BF16), vblk.astype(BF16), preferred_element_type=F32)
        o = o + _nt((q * jnp.exp(b)).astype(BF16), st.astype(BF16))
        dst[rows, :] = o
        kd = k * jnp.exp(b_last - b)
        upd = jnp.dot(v.T.astype(BF16), kd.astype(BF16), preferred_element_type=F32)
        s_ref[...] = st * jnp.exp(b_last) + upd * smask

    def body(t, carry):
        cb = jnp.where(t < nctx, nctx - 1 - t, nch + nctx - 1 - t)
        chunk(pl.multiple_of(t * C, C), gf_ref, sf_s, of_s, tri_f, cm_f, C - 1, C // 2 - 1)
        chunk(pl.multiple_of(cb * C, C), gb_ref, sb_s, ob_s, tri_b, cm_b, 0, C // 2)
        return carry

    lax.fori_loop(0, nch, body, 0, unroll=4)

    hones = ((lax.broadcasted_iota(jnp.int32, (256, 256), 0) >> 6)
             == (lax.broadcasted_iota(jnp.int32, (256, 256), 1) >> 6)).astype(F32) * (1.0 / GLA_DV)
    ng = ng_ref[...]
    for t in range(NT_TILES):
        rows = pl.ds(t * TM, TM)
        o = of_s[rows, :] + ob_s[rows, :]
        ms = jnp.dot(o * o, hones, precision=HIGHEST, preferred_element_type=F32)
        r = r_ref[0, rows, :]
        o_ref[0, rows, :] = (o * lax.rsqrt(ms + EPS) * ng * (r * _sigmoid(r))).astype(BF16)


def _gla(gq, gk, ggf, ggb, gv, gr, ng):
    nb = gq.shape[0]
    seq = lambda w: pl.BlockSpec((1, T, w), lambda b: (b, 0, 0))
    return pl.pallas_call(
        _gla_kernel,
        grid=(nb,),
        in_specs=[seq(128), seq(128), seq(128), seq(128), seq(256), seq(256),
                  pl.BlockSpec((1, 256), lambda b: (0, 0))],
        out_specs=seq(256),
        out_shape=jax.ShapeDtypeStruct((nb, T, 256), BF16),
        scratch_shapes=[pltpu.VMEM((T, 256), F32), pltpu.VMEM((T, 256), F32),
                        pltpu.VMEM((256, 128), F32), pltpu.VMEM((256, 128), F32)],
        compiler_params=_params("arbitrary"),
        name="gla",
    )(gq, gk, ggf, ggb, gv, gr, ng)


def _stacked_heads(q, n):
    rows = lax.broadcasted_iota(jnp.int32, (NAT_HEADS * n, 256), 0)
    lanes = lax.broadcasted_iota(jnp.int32, (NAT_HEADS * n, 256), 1) >> 6
    head = jnp.zeros_like(rows)
    for h in range(1, NAT_HEADS):
        head = head + (rows >= h * n).astype(jnp.int32)
    own = head == lanes
    q4 = jnp.concatenate([q] * NAT_HEADS, axis=0)
    return jnp.where(own, q4, jnp.zeros_like(q4)), own


def _unstack_heads(pv, own, n):
    pv = jnp.where(own, pv, 0.0)
    out = pv[0:n]
    for h in range(1, NAT_HEADS):
        out = out + pv[h * n:(h + 1) * n]
    return out


def _nat_kernel(q_ref, k_ref, v_ref, bias_ref, o_ref, *, need_ctx):
    out0 = CTX if need_ctx else 0
    if need_ctx:
        kc = k_ref[0, 0:CTX, :]
        vc = v_ref[0, 0:CTX, :]
        q4, own = _stacked_heads(q_ref[0, 0:CTX, :], CTX)
        s = _nt(q4, kc)
        e = jnp.exp(s - jnp.max(s, axis=-1, keepdims=True))
        p = (e * (1.0 / jnp.sum(e, axis=-1, keepdims=True))).astype(BF16)
        o_ref[0, 0:CTX, :] = _unstack_heads(jnp.dot(p, vc, preferred_element_type=F32), own, CTX).astype(BF16)

    nwin = NAT_WIN_R * GRID_W

    def row(r, carry):
        rs = jnp.clip(r - NAT_WIN_R // 2, 0, GRID_ROWS - NAT_WIN_R)
        case = rs - r + NAT_WIN_R - 1
        qrows = pl.ds(pl.multiple_of(CTX + r * GRID_W, GRID_W), GRID_W)
        wrows = pl.ds(pl.multiple_of(CTX + rs * GRID_W, GRID_W), nwin)
        q4, own = _stacked_heads(q_ref[0, qrows, :], GRID_W)
        sw = _nt(q4, k_ref[0, wrows, :]) + bias_ref[case]
        sc = _nt(q4, k_ref[0, 0:CTX, :])
        m = jnp.maximum(jnp.max(sw, axis=-1, keepdims=True), jnp.max(sc, axis=-1, keepdims=True))
        ew = jnp.exp(sw - m)
        ec = jnp.exp(sc - m)
        inv = 1.0 / (jnp.sum(ew, axis=-1, keepdims=True) + jnp.sum(ec, axis=-1, keepdims=True))
        pv = jnp.dot((ew * inv).astype(BF16), v_ref[0, wrows, :], preferred_element_type=F32)
        pv = pv + jnp.dot((ec * inv).astype(BF16), v_ref[0, 0:CTX, :], preferred_element_type=F32)
        orows = pl.ds(pl.multiple_of(out0 + r * GRID_W, GRID_W), GRID_W)
        o_ref[0, orows, :] = _unstack_heads(pv, own, GRID_W).astype(BF16)
        return carry

    lax.fori_loop(0, GRID_ROWS, row, 0, unroll=8)


def _nat(nq, nk, nv, bias, need_ctx):
    nb = nq.shape[0]
    rows_out = T if need_ctx else SEQ
    seq = pl.BlockSpec((1, T, 256), lambda b: (b, 0, 0))
    return pl.pallas_call(
        functools.partial(_nat_kernel, need_ctx=need_ctx),
        grid=(nb,),
        in_specs=[seq, seq, seq, pl.BlockSpec(bias.shape, lambda b: (0, 0, 0))],
        out_specs=pl.BlockSpec((1, rows_out, 256), lambda b: (b, 0, 0)),
        out_shape=jax.ShapeDtypeStruct((nb, rows_out, 256), BF16),
        compiler_params=_params("arbitrary"),
        name="nat",
    )(nq, nk, nv, bias)


def _nat_bias_table(rpb):
    cidx = np.arange(GRID_W)
    col_start = np.clip(cidx - NAT_WIN_C // 2, 0, GRID_W - NAT_WIN_C)
    col_mask = (cidx[None, :] >= col_start[:, None]) & (cidx[None, :] < col_start[:, None] + NAT_WIN_C)
    dc = np.clip(cidx[None, :] - cidx[:, None] + NAT_WIN_C - 1, 0, 2 * NAT_WIN_C - 2)
    dr = np.arange(NAT_WIN_R)[:, None] + np.arange(NAT_WIN_R)[None, :]
    pick_r = np.eye(2 * NAT_WIN_R - 1, dtype=np.float32)[dr]
    pick_c = np.eye(2 * NAT_WIN_C - 1, dtype=np.float32)[dc]
    t = jnp.einsum('hab,cja,qkb->hcjqk', rpb.astype(F32), pick_r, pick_c, precision=HIGHEST)
    t = jnp.where(col_mask[None, None, None], t, -jnp.inf)
    return t.transpose(1, 0, 3, 2, 4).reshape(NAT_WIN_R, NAT_HEADS * GRID_W, NAT_WIN_R * GRID_W)


def _diff_kernel(lam_ref, q_ref, k_ref, v_ref, g_ref, o_ref, *, lam_init, need_ctx):
    lv = lam_ref[...]
    lam = (jnp.exp(jnp.sum(lv[0:1] * lv[1:2], axis=-1, keepdims=True))
           - jnp.exp(jnp.sum(lv[2:3] * lv[3:4], axis=-1, keepdims=True)) + lam_init)
    w = 2 * DIFF_DH
    first = lax.broadcasted_iota(jnp.int32, (1, w), 1) < DIFF_DH

    def attend(nkeys):
        def scores(hh):
            cols = slice(hh * w, (hh + 1) * w)
            q = q_ref[0, :, cols]
            k = k_ref[0, 0:nkeys, cols]
            return (_nt(jnp.where(first, q, jnp.zeros_like(q)), k), _nt(jnp.where(first, jnp.zeros_like(q), q), k))

        def unnormalised(sc, scale):
            e = jnp.exp(sc - jnp.max(sc, axis=-1, keepdims=True))
            return e, scale / jnp.sum(e, axis=-1, keepdims=True)

        def output(hh, p):
            cols = slice(hh * w, (hh + 1) * w)
            o = jnp.dot(p, v_ref[0, 0:nkeys, cols], preferred_element_type=F32)
            o_ref[0, :, cols] = (_rms(o) * g_ref[:, cols] * (1.0 - lam_init)).astype(BF16)

        s = [scores(0)]
        p_prev = None
        for hh in range(DIFF_GROUP):
            e1, w1 = unnormalised(s[hh][0], 1.0)
            if hh + 1 < DIFF_GROUP:
                s.append(scores(hh + 1))
            e2, w2 = unnormalised(s[hh][1], lam)
            if p_prev is not None:
                output(hh - 1, p_prev)
            p_prev = (e1 * w1 - e2 * w2).astype(BF16)
        output(DIFF_GROUP - 1, p_prev)

    if need_ctx:
        @pl.when(pl.program_id(2) == 0)
        def _():
            attend(CTX)

        @pl.when(pl.program_id(2) > 0)
        def _():
            attend(T)
    else:
        attend(T)


def _diff(dq, dk, dv, lam_vec, dg, lam_init, need_ctx):
    nb = dq.shape[0]
    w = DIFF_GROUP * 2 * DIFF_DH
    t0 = 0 if need_ctx else 1
    return pl.pallas_call(
        functools.partial(_diff_kernel, lam_init=lam_init, need_ctx=need_ctx),
        grid=(nb, DIFF_HEADS // DIFF_GROUP, NT_TILES - t0),
        in_specs=[pl.BlockSpec((4, DIFF_DH), lambda b, h, t: (0, 0)),
                  pl.BlockSpec((1, TM, w), lambda b, h, t: (b, t + t0, h)),
                  pl.BlockSpec((1, T, w), lambda b, h, t: (b, 0, h)),
                  pl.BlockSpec((1, T, w), lambda b, h, t: (b, 0, h)),
                  pl.BlockSpec((1, w), lambda b, h, t: (0, h))],
        out_specs=pl.BlockSpec((1, TM, w), lambda b, h, t: (b, t, h)),
        out_shape=jax.ShapeDtypeStruct((nb, T - t0 * TM, DIFF_HEADS * 2 * DIFF_DH), BF16),
        compiler_params=_params("arbitrary", "arbitrary", "arbitrary"),
        name="diff_attn",
    )(lam_vec, dq, dk, dv, dg)


def _out_kernel(x_ref, yg_ref, yn_ref, yd_ref, mod_ref, w_ref, g_ref, xo_ref, ht_ref):
    attn = jnp.dot(yg_ref[0], w_ref[0:256, :], preferred_element_type=F32)
    attn = attn + jnp.dot(yn_ref[0], w_ref[256:512, :], preferred_element_type=F32)
    attn = attn + jnp.dot(yd_ref[0], w_ref[512:1024, :], preferred_element_type=F32)
    m = mod_ref[0]
    x = x_ref[0] + m[:, 2 * D:3 * D] * attn
    xo_ref[0] = x
    h2 = _rms(x) * g_ref[...] * (1.0 + m[:, 4 * D:5 * D]) + m[:, 3 * D:4 * D]
    ht_ref[...] = h2.T.astype(BF16)


def _out_proj(x, yg, yn, yd, mod, w_out, g2, need_ctx):
    nb = x.shape[0]
    t0 = 0 if need_ctx else 1
    nt = NT_TILES - t0
    full = lambda w: pl.BlockSpec((1, TM, w), lambda b, t: (b, t + t0, 0))
    tile = lambda w: pl.BlockSpec((1, TM, w), lambda b, t: (b, t, 0))
    return pl.pallas_call(
        _out_kernel,
        grid=(nb, nt),
        in_specs=[full(D), full(256), tile(256), tile(512), _mod_spec(nb, t0),
                  pl.BlockSpec((D, D), lambda b, t: (0, 0)), pl.BlockSpec((1, D), lambda b, t: (0, 0))],
        out_specs=[tile(D), pl.BlockSpec((D, TM), lambda b, t: (0, b * nt + t))],
        out_shape=[jax.ShapeDtypeStruct((nb, nt * TM, D), F32), jax.ShapeDtypeStruct((D, nb * nt * TM), BF16)],
        compiler_params=_params("arbitrary", "arbitrary"),
        name="out_proj",
    )(x, yg, yn, yd, mod, w_out, g2)


def _oddeven_pairs(n):
    pairs = []

    def merge(lo, hi, r):
        step = r * 2
        if step < hi - lo:
            merge(lo, hi, step)
            merge(lo + r, hi, step)
            pairs.extend((i, i + r) for i in range(lo + r, hi - r, step))
        else:
            pairs.append((lo, lo + r))

    def sort(lo, hi):
        if hi - lo >= 1:
            mid = lo + (hi - lo) // 2
            sort(lo, mid)
            sort(mid + 1, hi)
            merge(lo, hi, 1)

    sort(0, n - 1)
    return pairs


def _bitonic_pairs(n):
    pairs = []
    d = n // 2
    while d >= 1:
        pairs.extend((i, i + d) for i in range(n) if (i // d) % 2 == 0)
        d //= 2
    return pairs


_SORT16 = _oddeven_pairs(16)
_BITONIC16 = _bitonic_pairs(16)


def _compare_exchange(xs, pairs):
    for i, j in pairs:
        a, b = xs[i], xs[j]
        if b is None:
            continue
        if a is None:
            xs[i], xs[j] = b, None
        else:
            xs[i], xs[j] = jnp.maximum(a, b), jnp.minimum(a, b)


def _merge_top16(xs, ys):
    zs = []
    for k in range(16):
        a, b = xs[k], ys[15 - k]
        zs.append(a if b is None else (b if a is None else jnp.maximum(a, b)))
    _compare_exchange(zs, _BITONIC16)
    return zs


def _top16_of_128(x3):
    xs = [x3[r] for r in range(16)]
    _compare_exchange(xs, _SORT16)
    for shift in (4, 2, 1):
        xs = _merge_top16(xs, [pltpu.roll(x, shift, 0) for x in xs])
    return xs


def _peer_select(s1, s2):
    n = s1.shape[-1]
    a3 = s1.reshape(16, 8, n)
    b3 = s2.reshape(16, 8, n)
    t1 = _top16_of_128(a3)
    t2 = _top16_of_128(b3)
    cand = [[t1[a] + t2[b] for b in range(PEER_TOPK // (a + 1))] for a in range(PEER_TOPK)]
    pad = lambda xs: xs + [None] * (16 - len(xs))
    top = cand[0]
    for a in range(1, 8):
        top = _merge_top16(top, pad(cand[a]))
    top = _merge_top16(top, pad([cand[a][0] for a in range(8, 16)]))
    tau = top[15]
    cmax = cand[0][0]
    inf = jnp.full_like(tau, jnp.inf)
    z = jnp.zeros_like(tau)
    thr = []
    for a in range(PEER_TOPK):
        th = inf
        for b, cv in enumerate(cand[a]):
            sel = cv >= tau
            th = jnp.where(sel, jnp.minimum(th, t2[b]), th)
            z = z + jnp.where(sel, jnp.exp(cv - cmax), 0.0)
        thr.append(th)
    theta = jnp.full_like(a3, jnp.inf)
    for a in range(PEER_TOPK):
        theta = jnp.where(a3 == t1[a][None], thr[a][None], theta)
    c = jnp.exp(a3 - t1[0][None]) / z[None]
    e2 = jnp.exp(b3 - t2[0][None])
    return theta, c, e2.reshape(PEER_NKEYS, n)


def _peer_kernel(ht_ref, wq_ref, sk_ref, u_ref, vt_ref, o_ref, th_s, c_s, s2_s, e2_s, h8_s, a_s, w_s, acc_s):
    e = pl.program_id(1)
    tb = ht_ref.shape[1]

    @pl.when(e == 0)
    def _():
        qt = jnp.dot(wq_ref[...], ht_ref[...], preferred_element_type=F32)
        sk = sk_ref[...]
        for h in range(PEER_HEADS):
            s1 = jnp.dot(sk[0], qt[h * 128:h * 128 + 64], precision=HIGHEST, preferred_element_type=F32)
            s2 = jnp.dot(sk[1], qt[h * 128 + 64:h * 128 + 128], precision=HIGHEST, preferred_element_type=F32)
            theta, c, e2 = _peer_select(s1, s2)
            th_s[h] = theta
            c_s[h] = c * PEER_W_SCALE
            for lc in range(tb // 128):
                s2_s[h, lc] = s2[:, lc * 128:(lc + 1) * 128]
                e2_s[h, lc] = e2[:, lc * 128:(lc + 1) * 128]
        acc_s[...] = jnp.zeros_like(acc_s)
        h8_s[...] = jnp.clip(ht_ref[...], -F8_MAX, F8_MAX).astype(F8)

    a = jnp.dot(u_ref[...], h8_s[...], preferred_element_type=F32)
    for lc in range(tb // 128):
        a_s[lc] = a[:, lc * 128:(lc + 1) * 128]
    k_half = 0.5 / PEER_U_SCALE
    k_lin = math.sqrt(2.0 / math.pi) / PEER_U_SCALE
    k_cub = math.sqrt(2.0 / math.pi) * 0.044715 / PEER_U_SCALE ** 3
    for lc in range(tb // 128):
        lanes = slice(lc * 128, (lc + 1) * 128)
        for i in range(PEER_ROWS):
            g = jnp.zeros((PEER_NKEYS, 128), F32)
            for h in range(PEER_HEADS):
                th = th_s[h, e * (PEER_ROWS // 8) + i // 8, i % 8:i % 8 + 1, lanes]
                cw = c_s[h, e * (PEER_ROWS // 8) + i // 8, i % 8:i % 8 + 1, lanes]
                g = g + jnp.where(s2_s[h, lc] >= th, e2_s[h, lc] * cw, 0.0)
            rows = slice(i * PEER_NKEYS, (i + 1) * PEER_NKEYS)
            xr = a_s[lc, rows, :]
            w = g * (k_half * xr * (1.0 + jnp.tanh(xr * (k_lin + k_cub * (xr * xr)))))
            w_s[rows, lanes] = jnp.minimum(w, F8_MAX).astype(F8)
    acc_s[...] += jnp.dot(vt_ref[...], w_s[...], preferred_element_type=F32)

    @pl.when(e == pl.num_programs(1) - 1)
    def _():
        o_ref[...] = acc_s[...].T * (1.0 / PEER_W_SCALE)


def _peer(ht, wq_t, sk, u, v_t):
    n = ht.shape[1]
    tb = PEER_TB
    ne = PEER_EXPERTS // PEER_EB
    sel = pltpu.VMEM((PEER_HEADS, tb // 128, PEER_NKEYS, 128), F32)
    sel8 = pltpu.VMEM((PEER_HEADS, PEER_NKEYS // 8, 8, tb), F32)
    return pl.pallas_call(
        _peer_kernel,
        grid=(n // tb, ne),
        in_specs=[pl.BlockSpec((D, tb), lambda t, e: (0, t)),
                  pl.BlockSpec((D, D), lambda t, e: (0, 0)),
                  pl.BlockSpec((2, PEER_NKEYS, PEER_HALF), lambda t, e: (0, 0, 0)),
                  pl.BlockSpec((PEER_EB, D), lambda t, e: (e, 0)),
                  pl.BlockSpec((D, PEER_EB), lambda t, e: (0, e))],
        out_specs=pl.BlockSpec((tb, D), lambda t, e: (t, 0)),
        out_shape=jax.ShapeDtypeStruct((n, D), F32),
        scratch_shapes=[sel8, sel8, sel, sel, pltpu.VMEM((D, tb), F8),
                        pltpu.VMEM((tb // 128, PEER_EB, 128), F32), pltpu.VMEM((PEER_EB, tb), F8),
                        pltpu.VMEM((D, tb), F32)],
        compiler_params=_params("arbitrary", "arbitrary"),
        name="peer",
    )(ht, wq_t, sk, u, v_t)


def _final_kernel(x_ref, pr_ref, mod_ref, g_ref, o_ref):
    x = x_ref[0] + mod_ref[0][:, 5 * D:6 * D] * pr_ref[0]
    o_ref[0] = _rms(x) * g_ref[...]


def _final(x, peer, mod, g):
    nb = x.shape[0]
    lat = pl.BlockSpec((1, TM, D), lambda b, t: (b, t, 0))
    return pl.pallas_call(
        _final_kernel,
        grid=(nb, SEQ // TM),
        in_specs=[lat, lat, pl.BlockSpec((1, 1, 6 * D), lambda b, t: (b, 0, 0)),
                  pl.BlockSpec((1, D), lambda b, t: (0, 0))],
        out_specs=lat,
        out_shape=jax.ShapeDtypeStruct((nb, SEQ, D), F32),
        compiler_params=_params("arbitrary", "arbitrary"),
        name="final_norm",
    )(x, peer, mod, g)


def _rope_tables():
    half = DIFF_DH // 4
    inv = ROPE_BASE ** (-jnp.arange(half, dtype=F32) / half)
    ang = jnp.arange(GRID_W, dtype=F32)[:, None] * inv[None, :]

    def table(fn, ctx_value):
        small = fn(ang)
        by_row = jnp.broadcast_to(small[:GRID_ROWS, None, :], (GRID_ROWS, GRID_W, half))
        by_col = jnp.broadcast_to(small[None, :, :], (GRID_ROWS, GRID_W, half))
        per_map = jnp.concatenate([by_row, by_row, by_col, by_col], axis=-1)
        lat = jnp.tile(per_map.reshape(SEQ, DIFF_DH), (1, 2 * DIFF_HEADS))
        return jnp.concatenate([jnp.full((CTX, lat.shape[1]), ctx_value, F32), lat], axis=0)

    return table(jnp.cos, 1.0), table(jnp.sin, 0.0)


def _rot_columns(w):
    q = DIFF_DH // 4
    w4 = w.reshape(w.shape[0], -1, 2, q)
    return jnp.stack([-w4[:, :, 1], w4[:, :, 0]], axis=2).reshape(w.shape)


def _regroup_w_in(w):
    sizes = (128, 128, 256, 256, 16, 16, 256, 256, 256, 512, 512, 512)
    pts = np.cumsum((0,) + sizes)
    p = [w[:, pts[i]:pts[i + 1]] for i in range(12)]
    downs = jnp.concatenate([p[4], p[5], jnp.zeros((D, 128 - 2 * GLA_RANK), w.dtype)], axis=1)
    cols = [p[0], p[1], p[2], p[3], downs, p[6], p[7], p[8], p[9], p[10], p[11], _rot_columns(p[9]), _rot_columns(p[10])]
    return jnp.concatenate(cols, axis=1).astype(BF16)


def _gate_up_padded(gate_up):
    z = jnp.zeros((128, 128), F32)
    gf = z.at[0:GLA_RANK].set(gate_up[0]).astype(BF16)
    gb = z.at[GLA_RANK:2 * GLA_RANK].set(gate_up[1]).astype(BF16)
    return gf, gb


def kernel(x, c, ctx, c_ctx, ada_w, ada_b, norm1_g, norm2_g, w_in, gla_gate_up, gla_gate_b, gla_norm_g, nat_rpb,
           diff_lambda, diff_norm_g, w_out, peer_wq, peer_subkeys, peer_u, peer_v, final_g):
    nb = x.shape[0]
    depth = ada_w.shape[0]
    assert x.shape[1:] == (SEQ, D) and ctx.shape[1:] == (CTX, D) and (nb * T) % PEER_TB == 0 and (nb * SEQ) % PEER_TB == 0

    rows = -(-(nb + 1) // 8) * 8
    cc = jnp.concatenate([c, c_ctx[None], jnp.zeros((rows - nb - 1, D), F32)], axis=0)
    mod_all = _modulation(cc, ada_w, ada_b)
    cos, sin = _rope_tables()

    u8 = _to_f8(peer_u * PEER_U_SCALE)
    v8t = jnp.swapaxes(lax.optimization_barrier(_to_f8(peer_v)), 1, 2)

    xa = jnp.concatenate([ctx, x], axis=1)
    peer = None
    mod_prev = None
    for l in range(depth):
        lam_init = 0.8 - 0.6 * math.exp(-0.3 * l)
        mod = mod_all[l].reshape(rows, 1, 6 * D)
        gf, gb = _gate_up_padded(gla_gate_up[l])
        xa, (gq, gk, ggf, ggb, gv, gr, nq, nk, nv, dq, dk, dv) = _norm_proj(
            xa, peer, mod_prev, mod, norm1_g[l][None], _regroup_w_in(w_in[l]), gf, gb, gla_gate_b[l], cos, sin)
        yg = _gla(gq, gk, ggf, ggb, gv, gr, gla_norm_g[l][None])
        need_ctx = l + 1 < depth
        yn = _nat(nq, nk, nv, _nat_bias_table(nat_rpb[l]), need_ctx)
        yd = _diff(dq, dk, dv, diff_lambda[l], diff_norm_g[l][None], lam_init, need_ctx)
        xa, ht = _out_proj(xa, yg, yn, yd, mod, w_out[l].astype(BF16), norm2_g[l][None], need_ctx)
        peer = _peer(ht, peer_wq[l].T.astype(BF16), peer_subkeys[l], u8[l], v8t[l]).reshape(xa.shape)
        mod_prev = mod
    return _final(xa, peer, mod_prev, final_g[None])
```

```python
import functools
import math

import numpy as np
import jax
import jax.numpy as jnp
from jax import lax
from jax.experimental import pallas as pl
from jax.experimental.pallas import tpu as pltpu

F32 = jnp.float32
BF16 = jnp.bfloat16
F8 = jnp.float8_e4m3fn
F8_MAX = 448.0
HIGHEST = lax.Precision.HIGHEST

D = 1024
SEQ = 2048
CTX = 256
T = SEQ + CTX
GRID_W = 64
GRID_ROWS = SEQ // GRID_W
EPS = 1e-6
TM = 256
NT_TILES = T // TM

GLA_HEADS, GLA_DK, GLA_DV, GLA_RANK, GLA_TAU, GLA_CHUNK = 4, 32, 64, 16, 16.0, 64
NAT_HEADS, NAT_DH, NAT_WIN_R, NAT_WIN_C = 4, 64, 8, 16
DIFF_HEADS, DIFF_DH = 4, 64
DIFF_GROUP = 4
ROPE_BASE = 10000.0
PEER_HEADS, PEER_NKEYS, PEER_HALF, PEER_TOPK = 8, 128, 64, 16
PEER_EXPERTS = PEER_NKEYS * PEER_NKEYS
PEER_TB = 512
PEER_ROWS = 16
PEER_EB = PEER_ROWS * PEER_NKEYS
PEER_U_SCALE = 32.0
PEER_W_SCALE = 4.0

C_GQ, C_GK, C_GV, C_GR, C_GD = 0, 128, 256, 512, 768
C_NQ, C_NK, C_NV = 896, 1152, 1408
C_DQ, C_DK, C_DV, C_DQR, C_DKR = 1664, 2176, 2688, 3200, 3712
PROJ_COLS = 4224

_NT_DIMS = (((1,), (1,)), ((), ()))


def _nt(a, b):
    return lax.dot_general(a, b, _NT_DIMS, preferred_element_type=F32)


def _params(*sem):
    return pltpu.CompilerParams(dimension_semantics=sem, vmem_limit_bytes=56 * 1024 * 1024)


def _to_f8(x):
    return jnp.clip(x, -F8_MAX, F8_MAX).astype(F8)


def _rms(x):
    return x * lax.rsqrt(jnp.mean(x * x, axis=-1, keepdims=True) + EPS)


def _sigmoid(x):
    return 1.0 / (1.0 + jnp.exp(-x))


def _log_sigmoid(x):
    return jnp.minimum(x, 0.0) - jnp.log1p(jnp.exp(-jnp.abs(x)))


def _mod_kernel(c_ref, w_ref, b_ref, o_ref):
    c = c_ref[...]
    o_ref[0] = jnp.dot(c * _sigmoid(c), w_ref[0], precision=HIGHEST, preferred_element_type=F32) + b_ref[0]


def _modulation(cc, ada_w, ada_b):
    depth = ada_w.shape[0]
    rows = cc.shape[0]
    return pl.pallas_call(
        _mod_kernel,
        grid=(depth, 6),
        in_specs=[pl.BlockSpec((rows, D), lambda l, j: (0, 0)),
                  pl.BlockSpec((1, D, D), lambda l, j: (l, 0, j)),
                  pl.BlockSpec((1, 1, D), lambda l, j: (l, 0, j))],
        out_specs=pl.BlockSpec((1, rows, D), lambda l, j: (l, 0, j)),
        out_shape=jax.ShapeDtypeStruct((depth, rows, 6 * D), F32),
        compiler_params=_params("arbitrary", "arbitrary"),
        name="adaln_mod",
    )(cc, ada_w, ada_b.reshape(depth, 1, 6 * D))


def _proj_kernel(*refs, fuse_res):
    if fuse_res:
        x_ref, pr_ref, modp_ref = refs[:3]
        refs = refs[3:]
    else:
        x_ref = refs[0]
        refs = refs[1:]
    (mod_ref, g_ref, w_ref, gf_ref, gb_ref, gbias_ref, cos_ref, sin_ref) = refs[:8]
    outs = refs[8:]
    if fuse_res:
        xo_ref = outs[0]
        outs = outs[1:]
    (gq_ref, gk_ref, ggf_ref, ggb_ref, gv_ref, gr_ref, nq_ref, nk_ref, nv_ref, dq_ref, dk_ref, dv_ref) = outs

    x = x_ref[0]
    if fuse_res:
        x = x + modp_ref[0][:, 5 * D:6 * D] * pr_ref[0]
        xo_ref[0] = x
    m = mod_ref[0]
    h = _rms(x) * g_ref[...] * (1.0 + m[:, D:2 * D]) + m[:, 0:D]
    p = jnp.dot(h.astype(BF16), w_ref[...], preferred_element_type=F32)

    gq_ref[0] = p[:, C_GQ:C_GQ + 128] * (GLA_DK ** -0.5)
    gk_ref[0] = p[:, C_GK:C_GK + 128]
    gv_ref[0] = p[:, C_GV:C_GV + 256]
    gr_ref[0] = p[:, C_GR:C_GR + 256]
    pd = p[:, C_GD:C_GD + 128].astype(BF16)
    gbias = gbias_ref[...]
    xf = jnp.dot(pd, gf_ref[...], preferred_element_type=F32) + gbias[0:1]
    xb = jnp.dot(pd, gb_ref[...], preferred_element_type=F32) + gbias[1:2]
    ggf_ref[0] = _log_sigmoid(xf) / GLA_TAU
    ggb_ref[0] = _log_sigmoid(xb) / GLA_TAU

    nq_ref[0] = (p[:, C_NQ:C_NQ + 256] * (NAT_DH ** -0.5)).astype(BF16)
    nk_ref[0] = p[:, C_NK:C_NK + 256].astype(BF16)
    nv_ref[0] = p[:, C_NV:C_NV + 256].astype(BF16)

    cos = cos_ref[...]
    sin = sin_ref[...]
    dq = p[:, C_DQ:C_DQ + 512] * cos + p[:, C_DQR:C_DQR + 512] * sin
    dq_ref[0] = (dq * (DIFF_DH ** -0.5)).astype(BF16)
    dk_ref[0] = (p[:, C_DK:C_DK + 512] * cos + p[:, C_DKR:C_DKR + 512] * sin).astype(BF16)
    dv_ref[0] = p[:, C_DV:C_DV + 512].astype(BF16)


def _mod_spec(nb, t0=0):
    return pl.BlockSpec((1, 1, 6 * D), lambda b, t: (jnp.where(t + t0 == 0, nb, b), 0, 0))


def _norm_proj(x, peer, mod_prev, mod, g1, w_all, gf, gb, gbias, cos, sin):
    nb = x.shape[0]
    fuse = peer is not None
    tile = lambda w: pl.BlockSpec((1, TM, w), lambda b, t: (b, t, 0))
    full = lambda a: pl.BlockSpec(a.shape, lambda b, t: (0,) * a.ndim)
    ins, specs = [x], [tile(D)]
    if fuse:
        ins += [peer, mod_prev]
        specs += [tile(D), _mod_spec(nb)]
    ins += [mod, g1, w_all, gf, gb, gbias, cos, sin]
    specs += [_mod_spec(nb), full(g1), full(w_all), full(gf), full(gb), full(gbias),
              pl.BlockSpec((TM, 512), lambda b, t: (t, 0)), pl.BlockSpec((TM, 512), lambda b, t: (t, 0))]
    widths = [(128, F32)] * 4 + [(256, F32)] * 2 + [(256, BF16)] * 3 + [(512, BF16)] * 3
    out_shape = [jax.ShapeDtypeStruct((nb, T, w), dt) for w, dt in widths]
    out_specs = [tile(w) for w, _ in widths]
    if fuse:
        out_shape = [jax.ShapeDtypeStruct((nb, T, D), F32)] + out_shape
        out_specs = [tile(D)] + out_specs
    res = pl.pallas_call(
        functools.partial(_proj_kernel, fuse_res=fuse),
        grid=(nb, NT_TILES),
        in_specs=specs, out_specs=out_specs, out_shape=out_shape,
        compiler_params=_params("arbitrary", "arbitrary"),
        name="norm_proj",
    )(*ins)
    if fuse:
        return res[0], res[1:]
    return x, res


def _gla_kernel(q_ref, k_ref, gf_ref, gb_ref, v_ref, r_ref, ng_ref, o_ref, of_s, ob_s, sf_s, sb_s):
    C = GLA_CHUNK
    nch = T // C
    nctx = CTX // C
    ii = lax.broadcasted_iota(jnp.int32, (C, C), 0)
    jj = lax.broadcasted_iota(jnp.int32, (C, C), 1)
    tri_f = (jj <= ii).astype(F32)
    tri_b = (jj >= ii).astype(F32)
    it = lax.broadcasted_iota(jnp.int32, (C, 4 * C), 0)
    jt = lax.broadcasted_iota(jnp.int32, (C, 4 * C), 1) & (C - 1)
    cm_f = jt <= it
    cm_b = jt >= it
    kmask = ((lax.broadcasted_iota(jnp.int32, (4 * C, 128), 0) >> 6)
             == (lax.broadcasted_iota(jnp.int32, (4 * C, 128), 1) >> 5)).astype(F32)
    vmask = ((lax.broadcasted_iota(jnp.int32, (4 * C, 256), 0) >> 6)
             == (lax.broadcasted_iota(jnp.int32, (4 * C, 256), 1) >> 6)).astype(F32)
    smask = ((lax.broadcasted_iota(jnp.int32, (256, 128), 0) >> 6)
             == (lax.broadcasted_iota(jnp.int32, (256, 128), 1) >> 5)).astype(F32)

    sf_s[...] = jnp.zeros_like(sf_s)
    sb_s[...] = jnp.zeros_like(sb_s)

    def chunk(off, g_ref, s_ref, dst, tri, cm, last, mid):
        rows = pl.ds(off, C)
        q = q_ref[0, rows, :]
        k = k_ref[0, rows, :]
        v = v_ref[0, rows, :]
        g = g_ref[0, rows, :]
        st = s_ref[...]
        b = jnp.dot(tri, g, precision=HIGHEST, preferred_element_type=F32)
        b_last = b[last:last + 1]
        b_mid = b[mid:mid + 1]
        qi = q * jnp.exp(jnp.clip(b - b_mid, -80.0, 80.0))
        ki = k * jnp.exp(jnp.clip(b_mid - b, -80.0, 80.0))
        krows = jnp.concatenate([ki, ki, ki, ki], axis=0) * kmask
        a = jnp.where(cm, _nt(qi.astype(BF16), krows.astype(BF16)), 0.0)
        vblk = jnp.concatenate([v, v, v, v], axis=0) * vmask
        o = jnp.dot(a.astype(BF16), vblk.astype(BF16), preferred_element_type=F32)
        o = o + _nt((q * jnp.exp(b)).astype(BF16), st.astype(BF16))
        dst[rows, :] = o
        kd = k * jnp.exp(b_last - b)
        upd = jnp.dot(v.T.astype(BF16), kd.astype(BF16), preferred_element_type=F32)
        s_ref[...] = st * jnp.exp(b_last) + upd * smask

    def body(t, carry):
        cb = jnp.where(t < nctx, nctx - 1 - t, nch + nctx - 1 - t)
        chunk(pl.multiple_of(t * C, C), gf_ref, sf_s, of_s, tri_f, cm_f, C - 1, C // 2 - 1)
        chunk(pl.multiple_of(cb * C, C), gb_ref, sb_s, ob_s, tri_b, cm_b, 0, C // 2)
        return carry

    lax.fori_loop(0, nch, body, 0, unroll=4)

    hones = ((lax.broadcasted_iota(jnp.int32, (256, 256), 0) >> 6)
             == (lax.broadcasted_iota(jnp.int32, (256, 256), 1) >> 6)).astype(F32) * (1.0 / GLA_DV)
    ng = ng_ref[...]
    for t in range(NT_TILES):
        rows = pl.ds(t * TM, TM)
        o = of_s[rows, :] + ob_s[rows, :]
        ms = jnp.dot(o * o, hones, precision=HIGHEST, preferred_element_type=F32)
        r = r_ref[0, rows, :]
        o_ref[0, rows, :] = (o * lax.rsqrt(ms + EPS) * ng * (r * _sigmoid(r))).astype(BF16)


def _gla(gq, gk, ggf, ggb, gv, gr, ng):
    nb = gq.shape[0]
    seq = lambda w: pl.BlockSpec((1, T, w), lambda b: (b, 0, 0))
    return pl.pallas_call(
        _gla_kernel,
        grid=(nb,),
        in_specs=[seq(128), seq(128), seq(128), seq(128), seq(256), seq(256),
                  pl.BlockSpec((1, 256), lambda b: (0, 0))],
        out_specs=seq(256),
        out_shape=jax.ShapeDtypeStruct((nb, T, 256), BF16),
        scratch_shapes=[pltpu.VMEM((T, 256), F32), pltpu.VMEM((T, 256), F32),
                        pltpu.VMEM((256, 128), F32), pltpu.VMEM((256, 128), F32)],
        compiler_params=_params("arbitrary"),
        name="gla",
    )(gq, gk, ggf, ggb, gv, gr, ng)


def _stacked_heads(q, n):
    rows = lax.broadcasted_iota(jnp.int32, (NAT_HEADS * n, 256), 0)
    lanes = lax.broadcasted_iota(jnp.int32, (NAT_HEADS * n, 256), 1) >> 6
    head = jnp.zeros_like(rows)
    for h in range(1, NAT_HEADS):
        head = head + (rows >= h * n).astype(jnp.int32)
    own = head == lanes
    q4 = jnp.concatenate([q] * NAT_HEADS, axis=0)
    return jnp.where(own, q4, jnp.zeros_like(q4)), own


def _unstack_heads(pv, own, n):
    pv = jnp.where(own, pv, 0.0)
    out = pv[0:n]
    for h in range(1, NAT_HEADS):
        out = out + pv[h * n:(h + 1) * n]
    return out


def _nat_kernel(q_ref, k_ref, v_ref, bias_ref, o_ref, *, need_ctx):
    out0 = CTX if need_ctx else 0
    if need_ctx:
        kc = k_ref[0, 0:CTX, :]
        vc = v_ref[0, 0:CTX, :]
        q4, own = _stacked_heads(q_ref[0, 0:CTX, :], CTX)
        s = _nt(q4, kc)
        e = jnp.exp(s - jnp.max(s, axis=-1, keepdims=True))
        p = (e * (1.0 / jnp.sum(e, axis=-1, keepdims=True))).astype(BF16)
        o_ref[0, 0:CTX, :] = _unstack_heads(jnp.dot(p, vc, preferred_element_type=F32), own, CTX).astype(BF16)

    nwin = NAT_WIN_R * GRID_W

    def row(r, carry):
        rs = jnp.clip(r - NAT_WIN_R // 2, 0, GRID_ROWS - NAT_WIN_R)
        case = rs - r + NAT_WIN_R - 1
        qrows = pl.ds(pl.multiple_of(CTX + r * GRID_W, GRID_W), GRID_W)
        wrows = pl.ds(pl.multiple_of(CTX + rs * GRID_W, GRID_W), nwin)
        q4, own = _stacked_heads(q_ref[0, qrows, :], GRID_W)
        sw = _nt(q4, k_ref[0, wrows, :]) + bias_ref[case]
        sc = _nt(q4, k_ref[0, 0:CTX, :])
        m = jnp.maximum(jnp.max(sw, axis=-1, keepdims=True), jnp.max(sc, axis=-1, keepdims=True))
        ew = jnp.exp(sw - m)
        ec = jnp.exp(sc - m)
        inv = 1.0 / (jnp.sum(ew, axis=-1, keepdims=True) + jnp.sum(ec, axis=-1, keepdims=True))
        pv = jnp.dot((ew * inv).astype(BF16), v_ref[0, wrows, :], preferred_element_type=F32)
        pv = pv + jnp.dot((ec * inv).astype(BF16), v_ref[0, 0:CTX, :], preferred_element_type=F32)
        orows = pl.ds(pl.multiple_of(out0 + r * GRID_W, GRID_W), GRID_W)
        o_ref[0, orows, :] = _unstack_heads(pv, own, GRID_W).astype(BF16)
        return carry

    lax.fori_loop(0, GRID_ROWS, row, 0, unroll=8)


def _nat(nq, nk, nv, bias, need_ctx):
    nb = nq.shape[0]
    rows_out = T if need_ctx else SEQ
    seq = pl.BlockSpec((1, T, 256), lambda b: (b, 0, 0))
    return pl.pallas_call(
        functools.partial(_nat_kernel, need_ctx=need_ctx),
        grid=(nb,),
        in_specs=[seq, seq, seq, pl.BlockSpec(bias.shape, lambda b: (0, 0, 0))],
        out_specs=pl.BlockSpec((1, rows_out, 256), lambda b: (b, 0, 0)),
        out_shape=jax.ShapeDtypeStruct((nb, rows_out, 256), BF16),
        compiler_params=_params("arbitrary"),
        name="nat",
    )(nq, nk, nv, bias)


def _nat_bias_table(rpb):
    cidx = np.arange(GRID_W)
    col_start = np.clip(cidx - NAT_WIN_C // 2, 0, GRID_W - NAT_WIN_C)
    col_mask = (cidx[None, :] >= col_start[:, None]) & (cidx[None, :] < col_start[:, None] + NAT_WIN_C)
    dc = np.clip(cidx[None, :] - cidx[:, None] + NAT_WIN_C - 1, 0, 2 * NAT_WIN_C - 2)
    dr = np.arange(NAT_WIN_R)[:, None] + np.arange(NAT_WIN_R)[None, :]
    pick_r = np.eye(2 * NAT_WIN_R - 1, dtype=np.float32)[dr]
    pick_c = np.eye(2 * NAT_WIN_C - 1, dtype=np.float32)[dc]
    t = jnp.einsum('hab,cja,qkb->hcjqk', rpb.astype(F32), pick_r, pick_c, precision=HIGHEST)
    t = jnp.where(col_mask[None, None, None], t, -jnp.inf)
    return t.transpose(1, 0, 3, 2, 4).reshape(NAT_WIN_R, NAT_HEADS * GRID_W, NAT_WIN_R * GRID_W)


def _diff_kernel(lam_ref, q_ref, k_ref, v_ref, g_ref, o_ref, *, lam_init, need_ctx):
    lv = lam_ref[...]
    lam = (jnp.exp(jnp.sum(lv[0:1] * lv[1:2], axis=-1, keepdims=True))
           - jnp.exp(jnp.sum(lv[2:3] * lv[3:4], axis=-1, keepdims=True)) + lam_init)
    w = 2 * DIFF_DH
    first = lax.broadcasted_iota(jnp.int32, (1, w), 1) < DIFF_DH

    def attend(nkeys):
        def scores(hh):
            cols = slice(hh * w, (hh + 1) * w)
            q = q_ref[0, :, cols]
            k = k_ref[0, 0:nkeys, cols]
            return (_nt(jnp.where(first, q, jnp.zeros_like(q)), k), _nt(jnp.where(first, jnp.zeros_like(q), q), k))

        def unnormalised(sc, scale):
            e = jnp.exp(sc - jnp.max(sc, axis=-1, keepdims=True))
            return e, scale / jnp.sum(e, axis=-1, keepdims=True)

        def output(hh, p):
            cols = slice(hh * w, (hh + 1) * w)
            o = jnp.dot(p, v_ref[0, 0:nkeys, cols], preferred_element_type=F32)
            o_ref[0, :, cols] = (_rms(o) * g_ref[:, cols] * (1.0 - lam_init)).astype(BF16)

        s = [scores(0)]
        p_prev = None
        for hh in range(DIFF_GROUP):
            e1, w1 = unnormalised(s[hh][0], 1.0)
            if hh + 1 < DIFF_GROUP:
                s.append(scores(hh + 1))
            e2, w2 = unnormalised(s[hh][1], lam)
            if p_prev is not None:
                output(hh - 1, p_prev)
            p_prev = (e1 * w1 - e2 * w2).astype(BF16)
        output(DIFF_GROUP - 1, p_prev)

    if need_ctx:
        @pl.when(pl.program_id(2) == 0)
        def _():
            attend(CTX)

        @pl.when(pl.program_id(2) > 0)
        def _():
            attend(T)
    else:
        attend(T)


def _diff(dq, dk, dv, lam_vec, dg, lam_init, need_ctx):
    nb = dq.shape[0]
    w = DIFF_GROUP * 2 * DIFF_DH
    t0 = 0 if need_ctx else 1
    return pl.pallas_call(
        functools.partial(_diff_kernel, lam_init=lam_init, need_ctx=need_ctx),
        grid=(nb, DIFF_HEADS // DIFF_GROUP, NT_TILES - t0),
        in_specs=[pl.BlockSpec((4, DIFF_DH), lambda b, h, t: (0, 0)),
                  pl.BlockSpec((1, TM, w), lambda b, h, t: (b, t + t0, h)),
                  pl.BlockSpec((1, T, w), lambda b, h, t: (b, 0, h)),
                  pl.BlockSpec((1, T, w), lambda b, h, t: (b, 0, h)),
                  pl.BlockSpec((1, w), lambda b, h, t: (0, h))],
        out_specs=pl.BlockSpec((1, TM, w), lambda b, h, t: (b, t, h)),
        out_shape=jax.ShapeDtypeStruct((nb, T - t0 * TM, DIFF_HEADS * 2 * DIFF_DH), BF16),
        compiler_params=_params("arbitrary", "arbitrary", "arbitrary"),
        name="diff_attn",
    )(lam_vec, dq, dk, dv, dg)


def _out_kernel(x_ref, yg_ref, yn_ref, yd_ref, mod_ref, w_ref, g_ref, xo_ref, ht_ref):
    attn = jnp.dot(yg_ref[0], w_ref[0:256, :], preferred_element_type=F32)
    attn = attn + jnp.dot(yn_ref[0], w_ref[256:512, :], preferred_element_type=F32)
    attn = attn + jnp.dot(yd_ref[0], w_ref[512:1024, :], preferred_element_type=F32)
    m = mod_ref[0]
    x = x_ref[0] + m[:, 2 * D:3 * D] * attn
    xo_ref[0] = x
    h2 = _rms(x) * g_ref[...] * (1.0 + m[:, 4 * D:5 * D]) + m[:, 3 * D:4 * D]
    ht_ref[...] = h2.T.astype(BF16)


def _out_proj(x, yg, yn, yd, mod, w_out, g2, need_ctx):
    nb = x.shape[0]
    t0 = 0 if need_ctx else 1
    nt = NT_TILES - t0
    full = lambda w: pl.BlockSpec((1, TM, w), lambda b, t: (b, t + t0, 0))
    tile = lambda w: pl.BlockSpec((1, TM, w), lambda b, t: (b, t, 0))
    return pl.pallas_call(
        _out_kernel,
        grid=(nb, nt),
        in_specs=[full(D), full(256), tile(256), tile(512), _mod_spec(nb, t0),
                  pl.BlockSpec((D, D), lambda b, t: (0, 0)), pl.BlockSpec((1, D), lambda b, t: (0, 0))],
        out_specs=[tile(D), pl.BlockSpec((D, TM), lambda b, t: (0, b * nt + t))],
        out_shape=[jax.ShapeDtypeStruct((nb, nt * TM, D), F32), jax.ShapeDtypeStruct((D, nb * nt * TM), BF16)],
        compiler_params=_params("arbitrary", "arbitrary"),
        name="out_proj",
    )(x, yg, yn, yd, mod, w_out, g2)


def _oddeven_pairs(n):
    pairs = []

    def merge(lo, hi, r):
        step = r * 2
        if step < hi - lo:
            merge(lo, hi, step)
            merge(lo + r, hi, step)
            pairs.extend((i, i + r) for i in range(lo + r, hi - r, step))
        else:
            pairs.append((lo, lo + r))

    def sort(lo, hi):
        if hi - lo >= 1:
            mid = lo + (hi - lo) // 2
            sort(lo, mid)
            sort(mid + 1, hi)
            merge(lo, hi, 1)

    sort(0, n - 1)
    return pairs


def _bitonic_pairs(n):
    pairs = []
    d = n // 2
    while d >= 1:
        pairs.extend((i, i + d) for i in range(n) if (i // d) % 2 == 0)
        d //= 2
    return pairs


_SORT16 = _oddeven_pairs(16)
_BITONIC16 = _bitonic_pairs(16)


def _compare_exchange(xs, pairs):
    for i, j in pairs:
        a, b = xs[i], xs[j]
        if b is None:
            continue
        if a is None:
            xs[i], xs[j] = b, None
        else:
            xs[i], xs[j] = jnp.maximum(a, b), jnp.minimum(a, b)


def _merge_top16(xs, ys):
    zs = []
    for k in range(16):
        a, b = xs[k], ys[15 - k]
        zs.append(a if b is None else (b if a is None else jnp.maximum(a, b)))
    _compare_exchange(zs, _BITONIC16)
    return zs


def _top16_of_128(x3):
    xs = [x3[r] for r in range(16)]
    _compare_exchange(xs, _SORT16)
    for shift in (4, 2, 1):
        xs = _merge_top16(xs, [pltpu.roll(x, shift, 0) for x in xs])
    return xs


def _peer_select(s1, s2):
    n = s1.shape[-1]
    a3 = s1.reshape(16, 8, n)
    b3 = s2.reshape(16, 8, n)
    t1 = _top16_of_128(a3)
    t2 = _top16_of_128(b3)
    cand = [[t1[a] + t2[b] for b in range(PEER_TOPK // (a + 1))] for a in range(PEER_TOPK)]
    pad = lambda xs: xs + [None] * (16 - len(xs))
    top = cand[0]
    for a in range(1, 8):
        top = _merge_top16(top, pad(cand[a]))
    top = _merge_top16(top, pad([cand[a][0] for a in range(8, 16)]))
    tau = top[15]
    cmax = cand[0][0]
    inf = jnp.full_like(tau, jnp.inf)
    z = jnp.zeros_like(tau)
    thr = []
    for a in range(PEER_TOPK):
        th = inf
        for b, cv in enumerate(cand[a]):
            sel = cv >= tau
            th = jnp.where(sel, jnp.minimum(th, t2[b]), th)
            z = z + jnp.where(sel, jnp.exp(cv - cmax), 0.0)
        thr.append(th)
    theta = jnp.full_like(a3, jnp.inf)
    for a in range(PEER_TOPK):
        theta = jnp.where(a3 == t1[a][None], thr[a][None], theta)
    c = jnp.exp(a3 - t1[0][None]) / z[None]
    e2 = jnp.exp(b3 - t2[0][None])
    return theta, c, e2.reshape(PEER_NKEYS, n)


def _peer_kernel(ht_ref, wq_ref, sk_ref, u_ref, vt_ref, o_ref, th_s, c_s, s2_s, e2_s, h8_s, a_s, w_s, acc_s):
    e = pl.program_id(1)
    tb = ht_ref.shape[1]

    @pl.when(e == 0)
    def _():
        qt = jnp.dot(wq_ref[...], ht_ref[...], preferred_element_type=F32)
        sk = sk_ref[...]
        for h in range(PEER_HEADS):
            s1 = jnp.dot(sk[0], qt[h * 128:h * 128 + 64], precision=HIGHEST, preferred_element_type=F32)
            s2 = jnp.dot(sk[1], qt[h * 128 + 64:h * 128 + 128], precision=HIGHEST, preferred_element_type=F32)
            theta, c, e2 = _peer_select(s1, s2)
            th_s[h] = theta
            c_s[h] = c * PEER_W_SCALE
            for lc in range(tb // 128):
                s2_s[h, lc] = s2[:, lc * 128:(lc + 1) * 128]
                e2_s[h, lc] = e2[:, lc * 128:(lc + 1) * 128]
        acc_s[...] = jnp.zeros_like(acc_s)
        h8_s[...] = jnp.clip(ht_ref[...], -F8_MAX, F8_MAX).astype(F8)

    k_half = 0.5 / PEER_U_SCALE
    k_lin = math.sqrt(2.0 / math.pi) / PEER_U_SCALE
    k_cub = math.sqrt(2.0 / math.pi) * 0.044715 / PEER_U_SCALE ** 3
    rows_per_chunk = 4

    def a_chunk(k):
        rows = slice(k * rows_per_chunk * PEER_NKEYS, (k + 1) * rows_per_chunk * PEER_NKEYS)
        a = jnp.dot(u_ref[rows, :], h8_s[...], preferred_element_type=F32)
        for lc in range(tb // 128):
            a_s[lc, rows, :] = a[:, lc * 128:(lc + 1) * 128]

    a_chunk(0)
    for k in range(PEER_ROWS // rows_per_chunk):
        for lc in range(tb // 128):
            lanes = slice(lc * 128, (lc + 1) * 128)
            if lc == 1 and k + 1 < PEER_ROWS // rows_per_chunk:
                a_chunk(k + 1)
            for i in range(k * rows_per_chunk, (k + 1) * rows_per_chunk):
                g = jnp.zeros((PEER_NKEYS, 128), F32)
                for h in range(PEER_HEADS):
                    th = th_s[h, e * (PEER_ROWS // 8) + i // 8, i % 8:i % 8 + 1, lanes]
                    cw = c_s[h, e * (PEER_ROWS // 8) + i // 8, i % 8:i % 8 + 1, lanes]
                    g = g + jnp.where(s2_s[h, lc] >= th, e2_s[h, lc] * cw, 0.0)
                rows = slice(i * PEER_NKEYS, (i + 1) * PEER_NKEYS)
                xr = a_s[lc, rows, :]
                w = g * (k_half * xr * (1.0 + jnp.tanh(xr * (k_lin + k_cub * (xr * xr)))))
                w_s[rows, lanes] = jnp.minimum(w, F8_MAX).astype(F8)
    acc_s[...] += jnp.dot(vt_ref[...], w_s[...], preferred_element_type=F32)

    @pl.when(e == pl.num_programs(1) - 1)
    def _():
        o_ref[...] = acc_s[...].T * (1.0 / PEER_W_SCALE)


def _peer(ht, wq_t, sk, u, v_t):
    n = ht.shape[1]
    tb = PEER_TB
    ne = PEER_EXPERTS // PEER_EB
    sel = pltpu.VMEM((PEER_HEADS, tb // 128, PEER_NKEYS, 128), F32)
    sel8 = pltpu.VMEM((PEER_HEADS, PEER_NKEYS // 8, 8, tb), F32)
    return pl.pallas_call(
        _peer_kernel,
        grid=(n // tb, ne),
        in_specs=[pl.BlockSpec((D, tb), lambda t, e: (0, t)),
                  pl.BlockSpec((D, D), lambda t, e: (0, 0)),
                  pl.BlockSpec((2, PEER_NKEYS, PEER_HALF), lambda t, e: (0, 0, 0)),
                  pl.BlockSpec((PEER_EB, D), lambda t, e: (e, 0)),
                  pl.BlockSpec((D, PEER_EB), lambda t, e: (0, e))],
        out_specs=pl.BlockSpec((tb, D), lambda t, e: (t, 0)),
        out_shape=jax.ShapeDtypeStruct((n, D), F32),
        scratch_shapes=[sel8, sel8, sel, sel, pltpu.VMEM((D, tb), F8),
                        pltpu.VMEM((tb // 128, PEER_EB, 128), F32), pltpu.VMEM((PEER_EB, tb), F8),
                        pltpu.VMEM((D, tb), F32)],
        compiler_params=_params("arbitrary", "arbitrary"),
        name="peer",
    )(ht, wq_t, sk, u, v_t)


def _final_kernel(x_ref, pr_ref, mod_ref, g_ref, o_ref):
    x = x_ref[0] + mod_ref[0][:, 5 * D:6 * D] * pr_ref[0]
    o_ref[0] = _rms(x) * g_ref[...]


def _final(x, peer, mod, g):
    nb = x.shape[0]
    lat = pl.BlockSpec((1, TM, D), lambda b, t: (b, t, 0))
    return pl.pallas_call(
        _final_kernel,
        grid=(nb, SEQ // TM),
        in_specs=[lat, lat, pl.BlockSpec((1, 1, 6 * D), lambda b, t: (b, 0, 0)),
                  pl.BlockSpec((1, D), lambda b, t: (0, 0))],
        out_specs=lat,
        out_shape=jax.ShapeDtypeStruct((nb, SEQ, D), F32),
        compiler_params=_params("arbitrary", "arbitrary"),
        name="final_norm",
    )(x, peer, mod, g)


def _rope_tables():
    half = DIFF_DH // 4
    inv = ROPE_BASE ** (-jnp.arange(half, dtype=F32) / half)
    ang = jnp.arange(GRID_W, dtype=F32)[:, None] * inv[None, :]

    def table(fn, ctx_value):
        small = fn(ang)
        by_row = jnp.broadcast_to(small[:GRID_ROWS, None, :], (GRID_ROWS, GRID_W, half))
        by_col = jnp.broadcast_to(small[None, :, :], (GRID_ROWS, GRID_W, half))
        per_map = jnp.concatenate([by_row, by_row, by_col, by_col], axis=-1)
        lat = jnp.tile(per_map.reshape(SEQ, DIFF_DH), (1, 2 * DIFF_HEADS))
        return jnp.concatenate([jnp.full((CTX, lat.shape[1]), ctx_value, F32), lat], axis=0)

    return table(jnp.cos, 1.0), table(jnp.sin, 0.0)


def _rot_columns(w):
    q = DIFF_DH // 4
    w4 = w.reshape(w.shape[0], -1, 2, q)
    return jnp.stack([-w4[:, :, 1], w4[:, :, 0]], axis=2).reshape(w.shape)


def _regroup_w_in(w):
    sizes = (128, 128, 256, 256, 16, 16, 256, 256, 256, 512, 512, 512)
    pts = np.cumsum((0,) + sizes)
    p = [w[:, pts[i]:pts[i + 1]] for i in range(12)]
    downs = jnp.concatenate([p[4], p[5], jnp.zeros((D, 128 - 2 * GLA_RANK), w.dtype)], axis=1)
    cols = [p[0], p[1], p[2], p[3], downs, p[6], p[7], p[8], p[9], p[10], p[11], _rot_columns(p[9]), _rot_columns(p[10])]
    return jnp.concatenate(cols, axis=1).astype(BF16)


def _gate_up_padded(gate_up):
    z = jnp.zeros((128, 128), F32)
    gf = z.at[0:GLA_RANK].set(gate_up[0]).astype(BF16)
    gb = z.at[GLA_RANK:2 * GLA_RANK].set(gate_up[1]).astype(BF16)
    return gf, gb


def kernel(x, c, ctx, c_ctx, ada_w, ada_b, norm1_g, norm2_g, w_in, gla_gate_up, gla_gate_b, gla_norm_g, nat_rpb,
           diff_lambda, diff_norm_g, w_out, peer_wq, peer_subkeys, peer_u, peer_v, final_g):
    nb = x.shape[0]
    depth = ada_w.shape[0]
    assert x.shape[1:] == (SEQ, D) and ctx.shape[1:] == (CTX, D) and (nb * T) % PEER_TB == 0 and (nb * SEQ) % PEER_TB == 0

    rows = -(-(nb + 1) // 8) * 8
    cc = jnp.concatenate([c, c_ctx[None], jnp.zeros((rows - nb - 1, D), F32)], axis=0)
    mod_all = _modulation(cc, ada_w, ada_b)
    cos, sin = _rope_tables()

    u8 = _to_f8(peer_u * PEER_U_SCALE)
    v8t = jnp.swapaxes(lax.optimization_barrier(_to_f8(peer_v)), 1, 2)

    xa = jnp.concatenate([ctx, x], axis=1)
    peer = None
    mod_prev = None
    for l in range(depth):
        lam_init = 0.8 - 0.6 * math.exp(-0.3 * l)
        mod = mod_all[l].reshape(rows, 1, 6 * D)
        gf, gb = _gate_up_padded(gla_gate_up[l])
        xa, (gq, gk, ggf, ggb, gv, gr, nq, nk, nv, dq, dk, dv) = _norm_proj(
            xa, peer, mod_prev, mod, norm1_g[l][None], _regroup_w_in(w_in[l]), gf, gb, gla_gate_b[l], cos, sin)
        yg = _gla(gq, gk, ggf, ggb, gv, gr, gla_norm_g[l][None])
        need_ctx = l + 1 < depth
        yn = _nat(nq, nk, nv, _nat_bias_table(nat_rpb[l]), need_ctx)
        yd = _diff(dq, dk, dv, diff_lambda[l], diff_norm_g[l][None], lam_init, need_ctx)
        xa, ht = _out_proj(xa, yg, yn, yd, mod, w_out[l].astype(BF16), norm2_g[l][None], need_ctx)
        peer = _peer(ht, peer_wq[l].T.astype(BF16), peer_subkeys[l], u8[l], v8t[l]).reshape(xa.shape)
        mod_prev = mod
    return _final(xa, peer, mod_prev, final_g[None])
```

```python
import functools
import math

import numpy as np
import jax
import jax.numpy as jnp
from jax import lax
from jax.experimental import pallas as pl
from jax.experimental.pallas import tpu as pltpu

F32 = jnp.float32
BF16 = jnp.bfloat16
F8 = jnp.float8_e4m3fn
F8_MAX = 448.0
HIGHEST = lax.Precision.HIGHEST

D = 1024
SEQ = 2048
CTX = 256
T = SEQ + CTX
GRID_W = 64
GRID_ROWS = SEQ // GRID_W
EPS = 1e-6
TM = 256
NT_TILES = T // TM

GLA_HEADS, GLA_DK, GLA_DV, GLA_RANK, GLA_TAU, GLA_CHUNK = 4, 32, 64, 16, 16.0, 64
NAT_HEADS, NAT_DH, NAT_WIN_R, NAT_WIN_C = 4, 64, 8, 16
DIFF_HEADS, DIFF_DH = 4, 64
DIFF_GROUP = 4
ROPE_BASE = 10000.0
PEER_HEADS, PEER_NKEYS, PEER_HALF, PEER_TOPK = 8, 128, 64, 16
PEER_EXPERTS = PEER_NKEYS * PEER_NKEYS
PEER_TB = 512
PEER_ROWS = 16
PEER_EB = PEER_ROWS * PEER_NKEYS
PEER_U_SCALE = 32.0
PEER_W_SCALE = 4.0

C_GQ, C_GK, C_GV, C_GR, C_GD = 0, 128, 256, 512, 768
C_NQ, C_NK, C_NV = 896, 1152, 1408
C_DQ, C_DK, C_DV, C_DQR, C_DKR = 1664, 2176, 2688, 3200, 3712
PROJ_COLS = 4224

_NT_DIMS = (((1,), (1,)), ((), ()))


def _nt(a, b):
    return lax.dot_general(a, b, _NT_DIMS, preferred_element_type=F32)


def _params(*sem):
    return pltpu.CompilerParams(dimension_semantics=sem, vmem_limit_bytes=56 * 1024 * 1024)


def _to_f8(x):
    return jnp.clip(x, -F8_MAX, F8_MAX).astype(F8)


def _rms(x):
    return x * lax.rsqrt(jnp.mean(x * x, axis=-1, keepdims=True) + EPS)


def _sigmoid(x):
    return 1.0 / (1.0 + jnp.exp(-x))


def _log_sigmoid(x):
    return jnp.minimum(x, 0.0) - jnp.log1p(jnp.exp(-jnp.abs(x)))


def _mod_kernel(c_ref, w_ref, b_ref, o_ref):
    c = c_ref[...]
    o_ref[0] = jnp.dot(c * _sigmoid(c), w_ref[0], precision=HIGHEST, preferred_element_type=F32) + b_ref[0]


def _modulation(cc, ada_w, ada_b):
    depth = ada_w.shape[0]
    rows = cc.shape[0]
    return pl.pallas_call(
        _mod_kernel,
        grid=(depth, 6),
        in_specs=[pl.BlockSpec((rows, D), lambda l, j: (0, 0)),
                  pl.BlockSpec((1, D, D), lambda l, j: (l, 0, j)),
                  pl.BlockSpec((1, 1, D), lambda l, j: (l, 0, j))],
        out_specs=pl.BlockSpec((1, rows, D), lambda l, j: (l, 0, j)),
        out_shape=jax.ShapeDtypeStruct((depth, rows, 6 * D), F32),
        compiler_params=_params("arbitrary", "arbitrary"),
        name="adaln_mod",
    )(cc, ada_w, ada_b.reshape(depth, 1, 6 * D))


def _proj_kernel(*refs, fuse_res):
    if fuse_res:
        x_ref, pr_ref, modp_ref = refs[:3]
        refs = refs[3:]
    else:
        x_ref = refs[0]
        refs = refs[1:]
    (mod_ref, g_ref, w_ref, gf_ref, gb_ref, gbias_ref, cos_ref, sin_ref) = refs[:8]
    outs = refs[8:]
    if fuse_res:
        xo_ref = outs[0]
        outs = outs[1:]
    (gq_ref, gk_ref, ggf_ref, ggb_ref, gv_ref, gr_ref, nq_ref, nk_ref, nv_ref, dq_ref, dk_ref, dv_ref) = outs

    x = x_ref[0]
    if fuse_res:
        x = x + modp_ref[0][:, 5 * D:6 * D] * pr_ref[0]
        xo_ref[0] = x
    m = mod_ref[0]
    h = _rms(x) * g_ref[...] * (1.0 + m[:, D:2 * D]) + m[:, 0:D]
    p = jnp.dot(h.astype(BF16), w_ref[...], preferred_element_type=F32)

    gq_ref[0] = p[:, C_GQ:C_GQ + 128] * (GLA_DK ** -0.5)
    gk_ref[0] = p[:, C_GK:C_GK + 128]
    gv_ref[0] = p[:, C_GV:C_GV + 256]
    gr_ref[0] = p[:, C_GR:C_GR + 256]
    pd = p[:, C_GD:C_GD + 128].astype(BF16)
    gbias = gbias_ref[...]
    xf = jnp.dot(pd, gf_ref[...], preferred_element_type=F32) + gbias[0:1]
    xb = jnp.dot(pd, gb_ref[...], preferred_element_type=F32) + gbias[1:2]
    ggf_ref[0] = _log_sigmoid(xf) / GLA_TAU
    ggb_ref[0] = _log_sigmoid(xb) / GLA_TAU

    nq_ref[0] = (p[:, C_NQ:C_NQ + 256] * (NAT_DH ** -0.5)).astype(BF16)
    nk_ref[0] = p[:, C_NK:C_NK + 256].astype(BF16)
    nv_ref[0] = p[:, C_NV:C_NV + 256].astype(BF16)

    cos = cos_ref[...]
    sin = sin_ref[...]
    dq = p[:, C_DQ:C_DQ + 512] * cos + p[:, C_DQR:C_DQR + 512] * sin
    dq_ref[0] = (dq * (DIFF_DH ** -0.5)).astype(BF16)
    dk_ref[0] = (p[:, C_DK:C_DK + 512] * cos + p[:, C_DKR:C_DKR + 512] * sin).astype(BF16)
    dv_ref[0] = p[:, C_DV:C_DV + 512].astype(BF16)


def _mod_spec(nb, t0=0):
    return pl.BlockSpec((1, 1, 6 * D), lambda b, t: (jnp.where(t + t0 == 0, nb, b), 0, 0))


def _norm_proj(x, peer, mod_prev, mod, g1, w_all, gf, gb, gbias, cos, sin):
    nb = x.shape[0]
    fuse = peer is not None
    tile = lambda w: pl.BlockSpec((1, TM, w), lambda b, t: (b, t, 0))
    full = lambda a: pl.BlockSpec(a.shape, lambda b, t: (0,) * a.ndim)
    ins, specs = [x], [tile(D)]
    if fuse:
        ins += [peer, mod_prev]
        specs += [tile(D), _mod_spec(nb)]
    ins += [mod, g1, w_all, gf, gb, gbias, cos, sin]
    specs += [_mod_spec(nb), full(g1), full(w_all), full(gf), full(gb), full(gbias),
              pl.BlockSpec((TM, 512), lambda b, t: (t, 0)), pl.BlockSpec((TM, 512), lambda b, t: (t, 0))]
    widths = [(128, F32)] * 4 + [(256, F32)] * 2 + [(256, BF16)] * 3 + [(512, BF16)] * 3
    out_shape = [jax.ShapeDtypeStruct((nb, T, w), dt) for w, dt in widths]
    out_specs = [tile(w) for w, _ in widths]
    if fuse:
        out_shape = [jax.ShapeDtypeStruct((nb, T, D), F32)] + out_shape
        out_specs = [tile(D)] + out_specs
    res = pl.pallas_call(
        functools.partial(_proj_kernel, fuse_res=fuse),
        grid=(nb, NT_TILES),
        in_specs=specs, out_specs=out_specs, out_shape=out_shape,
        compiler_params=_params("arbitrary", "arbitrary"),
        name="norm_proj",
    )(*ins)
    if fuse:
        return res[0], res[1:]
    return x, res


def _gla_kernel(q_ref, k_ref, gf_ref, gb_ref, v_ref, r_ref, ng_ref, o_ref, of_s, ob_s, sf_s, sb_s):
    C = GLA_CHUNK
    nch = T // C
    nctx = CTX // C
    ii = lax.broadcasted_iota(jnp.int32, (C, C), 0)
    jj = lax.broadcasted_iota(jnp.int32, (C, C), 1)
    tri_f = (jj <= ii).astype(F32)
    tri_b = (jj >= ii).astype(F32)
    it = lax.broadcasted_iota(jnp.int32, (C, 4 * C), 0)
    jt = lax.broadcasted_iota(jnp.int32, (C, 4 * C), 1) & (C - 1)
    cm_f = jt <= it
    cm_b = jt >= it
    kmask = ((lax.broadcasted_iota(jnp.int32, (4 * C, 128), 0) >> 6)
             == (lax.broadcasted_iota(jnp.int32, (4 * C, 128), 1) >> 5)).astype(F32)
    vmask = ((lax.broadcasted_iota(jnp.int32, (4 * C, 256), 0) >> 6)
             == (lax.broadcasted_iota(jnp.int32, (4 * C, 256), 1) >> 6)).astype(F32)
    smask = ((lax.broadcasted_iota(jnp.int32, (256, 128), 0) >> 6)
             == (lax.broadcasted_iota(jnp.int32, (256, 128), 1) >> 5)).astype(F32)

    sf_s[...] = jnp.zeros_like(sf_s)
    sb_s[...] = jnp.zeros_like(sb_s)

    def chunk(off, g_ref, s_ref, dst, tri, cm, last, mid):
        rows = pl.ds(off, C)
        q = q_ref[0, rows, :]
        k = k_ref[0, rows, :]
        v = v_ref[0, rows, :]
        g = g_ref[0, rows, :]
        st = s_ref[...]
        b = jnp.dot(tri, g, precision=HIGHEST, preferred_element_type=F32)
        yield
        b_last = b[last:last + 1]
        b_mid = b[mid:mid + 1]
        qi = q * jnp.exp(jnp.clip(b - b_mid, -80.0, 80.0))
        ki = k * jnp.exp(jnp.clip(b_mid - b, -80.0, 80.0))
        krows = jnp.concatenate([ki, ki, ki, ki], axis=0) * kmask
        a = jnp.where(cm, _nt(qi.astype(BF16), krows.astype(BF16)), 0.0)
        yield
        vblk = jnp.concatenate([v, v, v, v], axis=0) * vmask
        o = jnp.dot(a.astype(BF16), vblk.astype(BF16), preferred_element_type=F32)
        o = o + _nt((q * jnp.exp(b)).astype(BF16), st.astype(BF16))
        dst[rows, :] = o
        yield
        kd = k * jnp.exp(b_last - b)
        upd = jnp.dot(v.T.astype(BF16), kd.astype(BF16), preferred_element_type=F32)
        s_ref[...] = st * jnp.exp(b_last) + upd * smask
        yield

    def body(t, carry):
        cb = jnp.where(t < nctx, nctx - 1 - t, nch + nctx - 1 - t)
        fwd = chunk(pl.multiple_of(t * C, C), gf_ref, sf_s, of_s, tri_f, cm_f, C - 1, C // 2 - 1)
        bwd = chunk(pl.multiple_of(cb * C, C), gb_ref, sb_s, ob_s, tri_b, cm_b, 0, C // 2)
        for _ in zip(fwd, bwd):
            pass
        return carry

    lax.fori_loop(0, nch, body, 0, unroll=4)

    hones = ((lax.broadcasted_iota(jnp.int32, (256, 256), 0) >> 6)
             == (lax.broadcasted_iota(jnp.int32, (256, 256), 1) >> 6)).astype(F32) * (1.0 / GLA_DV)
    ng = ng_ref[...]
    for t in range(NT_TILES):
        rows = pl.ds(t * TM, TM)
        o = of_s[rows, :] + ob_s[rows, :]
        ms = jnp.dot(o * o, hones, precision=HIGHEST, preferred_element_type=F32)
        r = r_ref[0, rows, :]
        o_ref[0, rows, :] = (o * lax.rsqrt(ms + EPS) * ng * (r * _sigmoid(r))).astype(BF16)


def _gla(gq, gk, ggf, ggb, gv, gr, ng):
    nb = gq.shape[0]
    seq = lambda w: pl.BlockSpec((1, T, w), lambda b: (b, 0, 0))
    return pl.pallas_call(
        _gla_kernel,
        grid=(nb,),
        in_specs=[seq(128), seq(128), seq(128), seq(128), seq(256), seq(256),
                  pl.BlockSpec((1, 256), lambda b: (0, 0))],
        out_specs=seq(256),
        out_shape=jax.ShapeDtypeStruct((nb, T, 256), BF16),
        scratch_shapes=[pltpu.VMEM((T, 256), F32), pltpu.VMEM((T, 256), F32),
                        pltpu.VMEM((256, 128), F32), pltpu.VMEM((256, 128), F32)],
        compiler_params=_params("arbitrary"),
        name="gla",
    )(gq, gk, ggf, ggb, gv, gr, ng)


def _stacked_heads(q, n):
    rows = lax.broadcasted_iota(jnp.int32, (NAT_HEADS * n, 256), 0)
    lanes = lax.broadcasted_iota(jnp.int32, (NAT_HEADS * n, 256), 1) >> 6
    head = jnp.zeros_like(rows)
    for h in range(1, NAT_HEADS):
        head = head + (rows >= h * n).astype(jnp.int32)
    own = head == lanes
    q4 = jnp.concatenate([q] * NAT_HEADS, axis=0)
    return jnp.where(own, q4, jnp.zeros_like(q4)), own


def _unstack_heads(pv, own, n):
    pv = jnp.where(own, pv, 0.0)
    out = pv[0:n]
    for h in range(1, NAT_HEADS):
        out = out + pv[h * n:(h + 1) * n]
    return out


def _nat_kernel(q_ref, k_ref, v_ref, bias_ref, o_ref, *, need_ctx):
    out0 = CTX if need_ctx else 0
    if need_ctx:
        kc = k_ref[0, 0:CTX, :]
        vc = v_ref[0, 0:CTX, :]
        q4, own = _stacked_heads(q_ref[0, 0:CTX, :], CTX)
        s = _nt(q4, kc)
        e = jnp.exp(s - jnp.max(s, axis=-1, keepdims=True))
        p = (e * (1.0 / jnp.sum(e, axis=-1, keepdims=True))).astype(BF16)
        o_ref[0, 0:CTX, :] = _unstack_heads(jnp.dot(p, vc, preferred_element_type=F32), own, CTX).astype(BF16)

    nwin = NAT_WIN_R * GRID_W

    def row(r, carry):
        rs = jnp.clip(r - NAT_WIN_R // 2, 0, GRID_ROWS - NAT_WIN_R)
        case = rs - r + NAT_WIN_R - 1
        qrows = pl.ds(pl.multiple_of(CTX + r * GRID_W, GRID_W), GRID_W)
        wrows = pl.ds(pl.multiple_of(CTX + rs * GRID_W, GRID_W), nwin)
        q4, own = _stacked_heads(q_ref[0, qrows, :], GRID_W)
        sw = _nt(q4, k_ref[0, wrows, :]) + bias_ref[case]
        sc = _nt(q4, k_ref[0, 0:CTX, :])
        m = jnp.maximum(jnp.max(sw, axis=-1, keepdims=True), jnp.max(sc, axis=-1, keepdims=True))
        ew = jnp.exp(sw - m)
        ec = jnp.exp(sc - m)
        inv = 1.0 / (jnp.sum(ew, axis=-1, keepdims=True) + jnp.sum(ec, axis=-1, keepdims=True))
        pv = jnp.dot((ew * inv).astype(BF16), v_ref[0, wrows, :], preferred_element_type=F32)
        pv = pv + jnp.dot((ec * inv).astype(BF16), v_ref[0, 0:CTX, :], preferred_element_type=F32)
        orows = pl.ds(pl.multiple_of(out0 + r * GRID_W, GRID_W), GRID_W)
        o_ref[0, orows, :] = _unstack_heads(pv, own, GRID_W).astype(BF16)
        return carry

    lax.fori_loop(0, GRID_ROWS, row, 0, unroll=8)


def _nat(nq, nk, nv, bias, need_ctx):
    nb = nq.shape[0]
    rows_out = T if need_ctx else SEQ
    seq = pl.BlockSpec((1, T, 256), lambda b: (b, 0, 0))
    return pl.pallas_call(
        functools.partial(_nat_kernel, need_ctx=need_ctx),
        grid=(nb,),
        in_specs=[seq, seq, seq, pl.BlockSpec(bias.shape, lambda b: (0, 0, 0))],
        out_specs=pl.BlockSpec((1, rows_out, 256), lambda b: (b, 0, 0)),
        out_shape=jax.ShapeDtypeStruct((nb, rows_out, 256), BF16),
        compiler_params=_params("arbitrary"),
        name="nat",
    )(nq, nk, nv, bias)


def _nat_bias_table(rpb):
    cidx = np.arange(GRID_W)
    col_start = np.clip(cidx - NAT_WIN_C // 2, 0, GRID_W - NAT_WIN_C)
    col_mask = (cidx[None, :] >= col_start[:, None]) & (cidx[None, :] < col_start[:, None] + NAT_WIN_C)
    dc = np.clip(cidx[None, :] - cidx[:, None] + NAT_WIN_C - 1, 0, 2 * NAT_WIN_C - 2)
    dr = np.arange(NAT_WIN_R)[:, None] + np.arange(NAT_WIN_R)[None, :]
    pick_r = np.eye(2 * NAT_WIN_R - 1, dtype=np.float32)[dr]
    pick_c = np.eye(2 * NAT_WIN_C - 1, dtype=np.float32)[dc]
    t = jnp.einsum('hab,cja,qkb->hcjqk', rpb.astype(F32), pick_r, pick_c, precision=HIGHEST)
    t = jnp.where(col_mask[None, None, None], t, -jnp.inf)
    return t.transpose(1, 0, 3, 2, 4).reshape(NAT_WIN_R, NAT_HEADS * GRID_W, NAT_WIN_R * GRID_W)


def _diff_kernel(lam_ref, q_ref, k_ref, v_ref, g_ref, o_ref, *, lam_init, need_ctx):
    lv = lam_ref[...]
    lam = (jnp.exp(jnp.sum(lv[0:1] * lv[1:2], axis=-1, keepdims=True))
           - jnp.exp(jnp.sum(lv[2:3] * lv[3:4], axis=-1, keepdims=True)) + lam_init)
    w = 2 * DIFF_DH
    first = lax.broadcasted_iota(jnp.int32, (1, w), 1) < DIFF_DH

    def attend(nkeys):
        def scores(hh):
            cols = slice(hh * w, (hh + 1) * w)
            q = q_ref[0, :, cols]
            k = k_ref[0, 0:nkeys, cols]
            return (_nt(jnp.where(first, q, jnp.zeros_like(q)), k), _nt(jnp.where(first, jnp.zeros_like(q), q), k))

        def unnormalised(sc, scale):
            e = jnp.exp(sc - jnp.max(sc, axis=-1, keepdims=True))
            return e, scale / jnp.sum(e, axis=-1, keepdims=True)

        def output(hh, p):
            cols = slice(hh * w, (hh + 1) * w)
            o = jnp.dot(p, v_ref[0, 0:nkeys, cols], preferred_element_type=F32)
            o_ref[0, :, cols] = (_rms(o) * g_ref[:, cols] * (1.0 - lam_init)).astype(BF16)

        s = [scores(0)]
        p_prev = None
        for hh in range(DIFF_GROUP):
            e1, w1 = unnormalised(s[hh][0], 1.0)
            if hh + 1 < DIFF_GROUP:
                s.append(scores(hh + 1))
            e2, w2 = unnormalised(s[hh][1], lam)
            if p_prev is not None:
                output(hh - 1, p_prev)
            p_prev = (e1 * w1 - e2 * w2).astype(BF16)
        output(DIFF_GROUP - 1, p_prev)

    if need_ctx:
        @pl.when(pl.program_id(2) == 0)
        def _():
            attend(CTX)

        @pl.when(pl.program_id(2) > 0)
        def _():
            attend(T)
    else:
        attend(T)


def _diff(dq, dk, dv, lam_vec, dg, lam_init, need_ctx):
    nb = dq.shape[0]
    w = DIFF_GROUP * 2 * DIFF_DH
    t0 = 0 if need_ctx else 1
    return pl.pallas_call(
        functools.partial(_diff_kernel, lam_init=lam_init, need_ctx=need_ctx),
        grid=(nb, DIFF_HEADS // DIFF_GROUP, NT_TILES - t0),
        in_specs=[pl.BlockSpec((4, DIFF_DH), lambda b, h, t: (0, 0)),
                  pl.BlockSpec((1, TM, w), lambda b, h, t: (b, t + t0, h)),
                  pl.BlockSpec((1, T, w), lambda b, h, t: (b, 0, h)),
                  pl.BlockSpec((1, T, w), lambda b, h, t: (b, 0, h)),
                  pl.BlockSpec((1, w), lambda b, h, t: (0, h))],
        out_specs=pl.BlockSpec((1, TM, w), lambda b, h, t: (b, t, h)),
        out_shape=jax.ShapeDtypeStruct((nb, T - t0 * TM, DIFF_HEADS * 2 * DIFF_DH), BF16),
        compiler_params=_params("arbitrary", "arbitrary", "arbitrary"),
        name="diff_attn",
    )(lam_vec, dq, dk, dv, dg)


def _out_kernel(x_ref, yg_ref, yn_ref, yd_ref, mod_ref, w_ref, g_ref, xo_ref, ht_ref):
    attn = jnp.dot(yg_ref[0], w_ref[0:256, :], preferred_element_type=F32)
    attn = attn + jnp.dot(yn_ref[0], w_ref[256:512, :], preferred_element_type=F32)
    attn = attn + jnp.dot(yd_ref[0], w_ref[512:1024, :], preferred_element_type=F32)
    m = mod_ref[0]
    x = x_ref[0] + m[:, 2 * D:3 * D] * attn
    xo_ref[0] = x
    h2 = _rms(x) * g_ref[...] * (1.0 + m[:, 4 * D:5 * D]) + m[:, 3 * D:4 * D]
    ht_ref[...] = h2.T.astype(BF16)


def _out_proj(x, yg, yn, yd, mod, w_out, g2, need_ctx):
    nb = x.shape[0]
    t0 = 0 if need_ctx else 1
    nt = NT_TILES - t0
    full = lambda w: pl.BlockSpec((1, TM, w), lambda b, t: (b, t + t0, 0))
    tile = lambda w: pl.BlockSpec((1, TM, w), lambda b, t: (b, t, 0))
    return pl.pallas_call(
        _out_kernel,
        grid=(nb, nt),
        in_specs=[full(D), full(256), tile(256), tile(512), _mod_spec(nb, t0),
                  pl.BlockSpec((D, D), lambda b, t: (0, 0)), pl.BlockSpec((1, D), lambda b, t: (0, 0))],
        out_specs=[tile(D), pl.BlockSpec((D, TM), lambda b, t: (0, b * nt + t))],
        out_shape=[jax.ShapeDtypeStruct((nb, nt * TM, D), F32), jax.ShapeDtypeStruct((D, nb * nt * TM), BF16)],
        compiler_params=_params("arbitrary", "arbitrary"),
        name="out_proj",
    )(x, yg, yn, yd, mod, w_out, g2)


def _oddeven_pairs(n):
    pairs = []

    def merge(lo, hi, r):
        step = r * 2
        if step < hi - lo:
            merge(lo, hi, step)
            merge(lo + r, hi, step)
            pairs.extend((i, i + r) for i in range(lo + r, hi - r, step))
        else:
            pairs.append((lo, lo + r))

    def sort(lo, hi):
        if hi - lo >= 1:
            mid = lo + (hi - lo) // 2
            sort(lo, mid)
            sort(mid + 1, hi)
            merge(lo, hi, 1)

    sort(0, n - 1)
    return pairs


def _bitonic_pairs(n):
    pairs = []
    d = n // 2
    while d >= 1:
        pairs.extend((i, i + d) for i in range(n) if (i // d) % 2 == 0)
        d //= 2
    return pairs


_SORT16 = _oddeven_pairs(16)
_BITONIC16 = _bitonic_pairs(16)


def _compare_exchange(xs, pairs):
    for i, j in pairs:
        a, b = xs[i], xs[j]
        if b is None:
            continue
        if a is None:
            xs[i], xs[j] = b, None
        else:
            xs[i], xs[j] = jnp.maximum(a, b), jnp.minimum(a, b)


def _merge_top16(xs, ys):
    zs = []
    for k in range(16):
        a, b = xs[k], ys[15 - k]
        zs.append(a if b is None else (b if a is None else jnp.maximum(a, b)))
    _compare_exchange(zs, _BITONIC16)
    return zs


def _top16_of_128(x3):
    xs = [x3[r] for r in range(16)]
    _compare_exchange(xs, _SORT16)
    for shift in (4, 2, 1):
        xs = _merge_top16(xs, [pltpu.roll(x, shift, 0) for x in xs])
    return xs


def _peer_select(s1, s2):
    n = s1.shape[-1]
    a3 = s1.reshape(16, 8, n)
    b3 = s2.reshape(16, 8, n)
    t1 = _top16_of_128(a3)
    t2 = _top16_of_128(b3)
    cand = [[t1[a] + t2[b] for b in range(PEER_TOPK // (a + 1))] for a in range(PEER_TOPK)]
    pad = lambda xs: xs + [None] * (16 - len(xs))
    top = cand[0]
    for a in range(1, 8):
        top = _merge_top16(top, pad(cand[a]))
    top = _merge_top16(top, pad([cand[a][0] for a in range(8, 16)]))
    tau = top[15]
    cmax = cand[0][0]
    inf = jnp.full_like(tau, jnp.inf)
    z = jnp.zeros_like(tau)
    thr = []
    for a in range(PEER_TOPK):
        th = inf
        for b, cv in enumerate(cand[a]):
            sel = cv >= tau
            th = jnp.where(sel, jnp.minimum(th, t2[b]), th)
            z = z + jnp.where(sel, jnp.exp(cv - cmax), 0.0)
        thr.append(th)
    theta = jnp.full_like(a3, jnp.inf)
    for a in range(PEER_TOPK):
        theta = jnp.where(a3 == t1[a][None], thr[a][None], theta)
    c = jnp.exp(a3 - t1[0][None]) / z[None]
    e2 = jnp.exp(b3 - t2[0][None])
    return theta, c, e2.reshape(PEER_NKEYS, n)


def _peer_kernel(ht_ref, wq_ref, sk_ref, u_ref, vt_ref, o_ref, th_s, c_s, s2_s, e2_s, h8_s, a_s, w_s, acc_s):
    e = pl.program_id(1)
    tb = ht_ref.shape[1]

    @pl.when(e == 0)
    def _():
        qt = jnp.dot(wq_ref[...], ht_ref[...], preferred_element_type=F32)
        sk = sk_ref[...]
        for h in range(PEER_HEADS):
            s1 = jnp.dot(sk[0], qt[h * 128:h * 128 + 64], precision=HIGHEST, preferred_element_type=F32)
            s2 = jnp.dot(sk[1], qt[h * 128 + 64:h * 128 + 128], precision=HIGHEST, preferred_element_type=F32)
            theta, c, e2 = _peer_select(s1, s2)
            th_s[h] = theta
            c_s[h] = c * PEER_W_SCALE
            for lc in range(tb // 128):
                s2_s[h, lc] = s2[:, lc * 128:(lc + 1) * 128]
                e2_s[h, lc] = e2[:, lc * 128:(lc + 1) * 128]
        acc_s[...] = jnp.zeros_like(acc_s)
        h8_s[...] = jnp.clip(ht_ref[...], -F8_MAX, F8_MAX).astype(F8)

    k_half = 0.5 / PEER_U_SCALE
    k_lin = math.sqrt(2.0 / math.pi) / PEER_U_SCALE
    k_cub = math.sqrt(2.0 / math.pi) * 0.044715 / PEER_U_SCALE ** 3
    rows_per_chunk = 4

    def a_chunk(k):
        rows = slice(k * rows_per_chunk * PEER_NKEYS, (k + 1) * rows_per_chunk * PEER_NKEYS)
        a = jnp.dot(u_ref[rows, :], h8_s[...], preferred_element_type=F32)
        for lc in range(tb // 128):
            a_s[lc, rows, :] = a[:, lc * 128:(lc + 1) * 128]

    a_chunk(0)
    for k in range(PEER_ROWS // rows_per_chunk):
        for lc in range(tb // 128):
            lanes = slice(lc * 128, (lc + 1) * 128)
            if lc == 1 and k + 1 < PEER_ROWS // rows_per_chunk:
                a_chunk(k + 1)
            for i in range(k * rows_per_chunk, (k + 1) * rows_per_chunk):
                g = jnp.zeros((PEER_NKEYS, 128), F32)
                for h in range(PEER_HEADS):
                    th = th_s[h, e * (PEER_ROWS // 8) + i // 8, i % 8:i % 8 + 1, lanes]
                    cw = c_s[h, e * (PEER_ROWS // 8) + i // 8, i % 8:i % 8 + 1, lanes]
                    g = g + jnp.where(s2_s[h, lc] >= th, e2_s[h, lc] * cw, 0.0)
                rows = slice(i * PEER_NKEYS, (i + 1) * PEER_NKEYS)
                xr = a_s[lc, rows, :]
                w = g * (k_half * xr * (1.0 + jnp.tanh(xr * (k_lin + k_cub * (xr * xr)))))
                w_s[rows, lanes] = jnp.minimum(w, F8_MAX).astype(F8)
    acc_s[...] += jnp.dot(vt_ref[...], w_s[...], preferred_element_type=F32)

    @pl.when(e == pl.num_programs(1) - 1)
    def _():
        o_ref[...] = acc_s[...].T * (1.0 / PEER_W_SCALE)


def _peer(ht, wq_t, sk, u, v_t):
    n = ht.shape[1]
    tb = PEER_TB
    ne = PEER_EXPERTS // PEER_EB
    sel = pltpu.VMEM((PEER_HEADS, tb // 128, PEER_NKEYS, 128), F32)
    sel8 = pltpu.VMEM((PEER_HEADS, PEER_NKEYS // 8, 8, tb), F32)
    return pl.pallas_call(
        _peer_kernel,
        grid=(n // tb, ne),
        in_specs=[pl.BlockSpec((D, tb), lambda t, e: (0, t)),
                  pl.BlockSpec((D, D), lambda t, e: (0, 0)),
                  pl.BlockSpec((2, PEER_NKEYS, PEER_HALF), lambda t, e: (0, 0, 0)),
                  pl.BlockSpec((PEER_EB, D), lambda t, e: (e, 0)),
                  pl.BlockSpec((D, PEER_EB), lambda t, e: (0, e))],
        out_specs=pl.BlockSpec((tb, D), lambda t, e: (t, 0)),
        out_shape=jax.ShapeDtypeStruct((n, D), F32),
        scratch_shapes=[sel8, sel8, sel, sel, pltpu.VMEM((D, tb), F8),
                        pltpu.VMEM((tb // 128, PEER_EB, 128), F32), pltpu.VMEM((PEER_EB, tb), F8),
                        pltpu.VMEM((D, tb), F32)],
        compiler_params=_params("arbitrary", "arbitrary"),
        name="peer",
    )(ht, wq_t, sk, u, v_t)


def _final_kernel(x_ref, pr_ref, mod_ref, g_ref, o_ref):
    x = x_ref[0] + mod_ref[0][:, 5 * D:6 * D] * pr_ref[0]
    o_ref[0] = _rms(x) * g_ref[...]


def _final(x, peer, mod, g):
    nb = x.shape[0]
    lat = pl.BlockSpec((1, TM, D), lambda b, t: (b, t, 0))
    return pl.pallas_call(
        _final_kernel,
        grid=(nb, SEQ // TM),
        in_specs=[lat, lat, pl.BlockSpec((1, 1, 6 * D), lambda b, t: (b, 0, 0)),
                  pl.BlockSpec((1, D), lambda b, t: (0, 0))],
        out_specs=lat,
        out_shape=jax.ShapeDtypeStruct((nb, SEQ, D), F32),
        compiler_params=_params("arbitrary", "arbitrary"),
        name="final_norm",
    )(x, peer, mod, g)


def _rope_tables():
    half = DIFF_DH // 4
    inv = ROPE_BASE ** (-jnp.arange(half, dtype=F32) / half)
    ang = jnp.arange(GRID_W, dtype=F32)[:, None] * inv[None, :]

    def table(fn, ctx_value):
        small = fn(ang)
        by_row = jnp.broadcast_to(small[:GRID_ROWS, None, :], (GRID_ROWS, GRID_W, half))
        by_col = jnp.broadcast_to(small[None, :, :], (GRID_ROWS, GRID_W, half))
        per_map = jnp.concatenate([by_row, by_row, by_col, by_col], axis=-1)
        lat = jnp.tile(per_map.reshape(SEQ, DIFF_DH), (1, 2 * DIFF_HEADS))
        return jnp.concatenate([jnp.full((CTX, lat.shape[1]), ctx_value, F32), lat], axis=0)

    return table(jnp.cos, 1.0), table(jnp.sin, 0.0)


def _rot_columns(w):
    q = DIFF_DH // 4
    w4 = w.reshape(w.shape[0], -1, 2, q)
    return jnp.stack([-w4[:, :, 1], w4[:, :, 0]], axis=2).reshape(w.shape)


def _regroup_w_in(w):
    sizes = (128, 128, 256, 256, 16, 16, 256, 256, 256, 512, 512, 512)
    pts = np.cumsum((0,) + sizes)
    p = [w[:, pts[i]:pts[i + 1]] for i in range(12)]
    downs = jnp.concatenate([p[4], p[5], jnp.zeros((D, 128 - 2 * GLA_RANK), w.dtype)], axis=1)
    cols = [p[0], p[1], p[2], p[3], downs, p[6], p[7], p[8], p[9], p[10], p[11], _rot_columns(p[9]), _rot_columns(p[10])]
    return jnp.concatenate(cols, axis=1).astype(BF16)


def _gate_up_padded(gate_up):
    z = jnp.zeros((128, 128), F32)
    gf = z.at[0:GLA_RANK].set(gate_up[0]).astype(BF16)
    gb = z.at[GLA_RANK:2 * GLA_RANK].set(gate_up[1]).astype(BF16)
    return gf, gb


def kernel(x, c, ctx, c_ctx, ada_w, ada_b, norm1_g, norm2_g, w_in, gla_gate_up, gla_gate_b, gla_norm_g, nat_rpb,
           diff_lambda, diff_norm_g, w_out, peer_wq, peer_subkeys, peer_u, peer_v, final_g):
    nb = x.shape[0]
    depth = ada_w.shape[0]
    assert x.shape[1:] == (SEQ, D) and ctx.shape[1:] == (CTX, D) and (nb * T) % PEER_TB == 0 and (nb * SEQ) % PEER_TB == 0

    rows = -(-(nb + 1) // 8) * 8
    cc = jnp.concatenate([c, c_ctx[None], jnp.zeros((rows - nb - 1, D), F32)], axis=0)
    mod_all = _modulation(cc, ada_w, ada_b)
    cos, sin = _rope_tables()

    u8 = _to_f8(peer_u * PEER_U_SCALE)
    v8t = jnp.swapaxes(lax.optimization_barrier(_to_f8(peer_v)), 1, 2)

    xa = jnp.concatenate([ctx, x], axis=1)
    peer = None
    mod_prev = None
    for l in range(depth):
        lam_init = 0.8 - 0.6 * math.exp(-0.3 * l)
        mod = mod_all[l].reshape(rows, 1, 6 * D)
        gf, gb = _gate_up_padded(gla_gate_up[l])
        xa, (gq, gk, ggf, ggb, gv, gr, nq, nk, nv, dq, dk, dv) = _norm_proj(
            xa, peer, mod_prev, mod, norm1_g[l][None], _regroup_w_in(w_in[l]), gf, gb, gla_gate_b[l], cos, sin)
        yg = _gla(gq, gk, ggf, ggb, gv, gr, gla_norm_g[l][None])
        need_ctx = l + 1 < depth
        yn = _nat(nq, nk, nv, _nat_bias_table(nat_rpb[l]), need_ctx)
        yd = _diff(dq, dk, dv, diff_lambda[l], diff_norm_g[l][None], lam_init, need_ctx)
        xa, ht = _out_proj(xa, yg, yn, yd, mod, w_out[l].astype(BF16), norm2_g[l][None], need_ctx)
        peer = _peer(ht, peer_wq[l].T.astype(BF16), peer_subkeys[l], u8[l], v8t[l]).reshape(xa.shape)
        mod_prev = mod
    return _final(xa, peer, mod_prev, final_g[None])
```
